```python
import math
import numpy as np
import jax
import jax.numpy as jnp
from jax import lax

D_MODEL = 4096
BATCH = 4
SEQ = 4096
DEPTH = 1

RMS_EPS = 1e-6
NEG_INF = -1e30
FORCE_SCORE = 1e6

GDN_HEADS = 16
GDN_HEAD_DIM = 128
GDN_WIDTH = GDN_HEADS * GDN_HEAD_DIM
GDN_CONV = 4
GDN_CHUNK = 64

NSA_HEADS = 16
NSA_GROUPS = 4
NSA_HPG = NSA_HEADS // NSA_GROUPS
NSA_HEAD_DIM = 128
NSA_WIDTH = NSA_HEADS * NSA_HEAD_DIM
NSA_KV_WIDTH = NSA_GROUPS * NSA_HEAD_DIM
CMP_BLOCK = 32
CMP_STRIDE = 16
SEL_BLOCK = 64
SEL_TOP_N = 16
WINDOW = 512
NSA_Q_BLOCK = 32

N_GROUPS = 8
EXPERTS_PER_GROUP = 8
N_EXPERTS = N_GROUPS * EXPERTS_PER_GROUP
TOP_K = 2
D_FF = 768
MOE_BLOCK = 128

IN_SIZES = (3 * GDN_WIDTH,
            GDN_WIDTH,
            GDN_HEADS,
            GDN_HEADS,
            NSA_WIDTH,
            6 * NSA_KV_WIDTH,
            3 * NSA_HEADS,
            2 * D_MODEL)
IN_COLS = sum(IN_SIZES)

kernel_name = 'hybrid_gdn_nsa_hmoe_block'


def rms_norm(x, gain):
    xf = x.astype(jnp.float32)
    y = xf * lax.rsqrt(jnp.mean(jnp.square(xf), axis=-1, keepdims=True) + RMS_EPS)
    return (y * gain.astype(jnp.float32)).astype(x.dtype)


def l2_normalize(x):
    xf = x.astype(jnp.float32)
    return xf * lax.rsqrt(jnp.sum(jnp.square(xf), axis=-1, keepdims=True) + RMS_EPS)


def causal_depthwise_conv(x, w):
    width = w.shape[0]
    return lax.conv_general_dilated(x, w[:, None, :].astype(x.dtype), window_strides=(1,),
                                    padding=((width - 1, 0),), dimension_numbers=('NWC', 'WIO', 'NWC'),
                                    feature_group_count=x.shape[-1])


def masked_softmax(scores, mask):
    s = jnp.where(mask, scores.astype(jnp.float32), NEG_INF)
    return jnp.where(mask, jax.nn.softmax(s, axis=-1), 0.0)


def gated_delta_rule_chunked(q, k, v, g, beta):
    B, H, T, dk = k.shape
    dv = v.shape[-1]
    C = GDN_CHUNK
    N = T // C
    q, k, v = (t.reshape(B, H, N, C, t.shape[-1]) for t in (q, k, v))
    g = g.reshape(B, H, N, C)
    beta = beta.reshape(B, H, N, C)
    gc = jnp.cumsum(g, axis=-1)
    incl = jnp.tril(jnp.ones((C, C), bool))
    strict = jnp.tril(jnp.ones((C, C), bool), -1)
    decay = jnp.exp(jnp.where(incl, gc[..., :, None] - gc[..., None, :], -jnp.inf))
    k_beta = k * beta[..., None]
    lower = jnp.where(strict, jnp.einsum('bhncd,bhnsd->bhncs', k_beta, k) * decay, 0.0)
    rhs = jnp.concatenate([v * beta[..., None], k_beta * jnp.exp(gc)[..., None]], axis=-1)
    sol = lax.linalg.triangular_solve(lower + jnp.eye(C, dtype=lower.dtype), rhs,
                                      left_side=True, lower=True, unit_diagonal=True)
    u, w = sol[..., :dv], sol[..., dv:]
    attn = jnp.where(incl, jnp.einsum('bhncd,bhnsd->bhncs', q, k) * decay, 0.0)
    q_in = q * jnp.exp(gc)[..., None]
    k_out = k * jnp.exp(gc[..., -1:] - gc)[..., None]
    chunk_decay = jnp.exp(gc[..., -1])

    def step(state, xs):
        q_i, k_i, u_i, w_i, a_i, d_i = xs
        v_new = u_i - jnp.einsum('bhcd,bhde->bhce', w_i, state)
        o_i = jnp.einsum('bhcd,bhde->bhce', q_i, state) + jnp.einsum('bhcs,bhse->bhce', a_i, v_new)
        state = state * d_i[..., None, None] + jnp.einsum('bhcd,bhce->bhde', k_i, v_new)
        return state, o_i

    xs = tuple(jnp.moveaxis(t, 2, 0) for t in (q_in, k_out, u, w, attn, chunk_decay))
    _, o = lax.scan(step, jnp.zeros((B, H, dk, dv), jnp.float32), xs)
    return jnp.moveaxis(o, 0, 2).reshape(B, H, T, dv)


def gdn_mixer(qkv, z, a_raw, b_raw, conv_w, a_log, dt_bias, norm_w):
    B, T, _ = qkv.shape
    H, d = GDN_HEADS, GDN_HEAD_DIM
    qkv = jax.nn.silu(causal_depthwise_conv(qkv, conv_w))
    q, k, v = jnp.split(qkv, 3, axis=-1)
    heads = lambda t: t.reshape(B, T, H, d).transpose(0, 2, 1, 3)
    q = l2_normalize(heads(q)) * (d ** -0.5)
    k = l2_normalize(heads(k))
    v = heads(v).astype(jnp.float32)
    beta = jax.nn.sigmoid(b_raw.astype(jnp.float32)).transpose(0, 2, 1)
    g = (-jnp.exp(a_log.astype(jnp.float32))
         * jax.nn.softplus(a_raw.astype(jnp.float32) + dt_bias.astype(jnp.float32))).transpose(0, 2, 1)
    o = gated_delta_rule_chunked(q, k, v, g, beta).transpose(0, 2, 1, 3)
    o = rms_norm(o, norm_w) * jax.nn.silu(z.reshape(B, T, H, d).astype(jnp.float32))
    return o.reshape(B, T, GDN_WIDTH).astype(qkv.dtype)


def nsa_compress(blocks, pe, w1, w2):
    hid = jax.nn.silu(jnp.einsum('bgnld,lde->bgne', blocks + pe.astype(blocks.dtype), w1))
    return jnp.einsum('bgne,ef->bgnf', hid, w2)


def nsa_mixer(q, kv, gate_logits, pe_k, w1_k, w2_k, pe_v, w1_v, w2_v):
    B, T, _ = q.shape
    G, Hg, d, QB = NSA_GROUPS, NSA_HPG, NSA_HEAD_DIM, NSA_Q_BLOCK
    n_cmp = (T - CMP_BLOCK) // CMP_STRIDE + 1
    n_sel = T // SEL_BLOCK
    n_top = min(SEL_TOP_N, n_sel)
    n_qb = T // QB
    q = q.reshape(B, T, G, Hg, d).transpose(0, 2, 3, 1, 4) * (d ** -0.5)
    k_c, v_c, k_s, v_s, k_w, v_w = [t.reshape(B, T, G, d).transpose(0, 2, 1, 3)
                                    for t in jnp.split(kv, 6, axis=-1)]
    gates = jax.nn.sigmoid(gate_logits).reshape(B, T, G, Hg, 3).transpose(0, 2, 3, 1, 4)

    cmp_idx = np.arange(n_cmp)[:, None] * CMP_STRIDE + np.arange(CMP_BLOCK)[None, :]
    k_cmp = nsa_compress(k_c[:, :, cmp_idx], pe_k, w1_k, w2_k)
    v_cmp = nsa_compress(v_c[:, :, cmp_idx], pe_v, w1_v, w2_v)
    cmp_last = jnp.asarray(cmp_idx[:, -1], jnp.int32)
    cmp_start = np.arange(n_cmp) * CMP_STRIDE
    sel_start = np.arange(n_sel) * SEL_BLOCK
    overlap = (np.minimum(cmp_start[:, None] + CMP_BLOCK, sel_start[None, :] + SEL_BLOCK)
               - np.maximum(cmp_start[:, None], sel_start[None, :]))
    sel_map = jnp.asarray(np.clip(overlap, 0, None) / CMP_STRIDE, jnp.float32)

    k_sel_blocks = k_s.reshape(B, G, n_sel, SEL_BLOCK, d)
    v_sel_blocks = v_s.reshape(B, G, n_sel, SEL_BLOCK, d)
    k_win = jnp.pad(k_w, ((0, 0), (0, 0), (WINDOW, 0), (0, 0)))
    v_win = jnp.pad(v_w, ((0, 0), (0, 0), (WINDOW, 0), (0, 0)))
    b_idx = jnp.arange(B)[:, None, None, None]
    g_idx = jnp.arange(G)[None, :, None, None]
    blk_ids = jnp.arange(n_sel)
    in_block = jnp.arange(SEL_BLOCK)
    win_offsets = jnp.arange(WINDOW + QB) - WINDOW

    def query_block(args):
        qb, q_b, g_b = args
        t0 = qb * QB
        t = t0 + jnp.arange(QB)
        s = jnp.einsum('bghqd,bgnd->bghqn', q_b, k_cmp)
        p_cmp = masked_softmax(s, cmp_last[None, :] <= t[:, None])
        o_cmp = jnp.einsum('bghqn,bgnd->bghqd', p_cmp.astype(v_cmp.dtype), v_cmp)
        importance = jnp.einsum('bghqn,nj->bgqj', p_cmp, sel_map)
        cur = (t // SEL_BLOCK)[:, None]
        forced = (blk_ids == 0) | (blk_ids == cur) | (blk_ids == cur - 1)
        causal = blk_ids * SEL_BLOCK <= t[:, None]
        score = jnp.where(forced, FORCE_SCORE, jnp.where(causal, importance, NEG_INF))
        _, sel = lax.top_k(score, n_top)
        k_g = k_sel_blocks[b_idx, g_idx, sel].reshape(B, G, QB, n_top * SEL_BLOCK, d)
        v_g = v_sel_blocks[b_idx, g_idx, sel].reshape(B, G, QB, n_top * SEL_BLOCK, d)
        pos = (sel[..., None] * SEL_BLOCK + in_block).reshape(B, G, QB, n_top * SEL_BLOCK)
        s = jnp.einsum('bghqd,bgqkd->bghqk', q_b, k_g)
        p = masked_softmax(s, (pos <= t[:, None])[:, :, None])
        o_sel = jnp.einsum('bghqk,bgqkd->bghqd', p.astype(v_g.dtype), v_g)
        k_b = lax.dynamic_slice_in_dim(k_win, t0, WINDOW + QB, axis=2)
        v_b = lax.dynamic_slice_in_dim(v_win, t0, WINDOW + QB, axis=2)
        kpos = t0 + win_offsets
        rel = t[:, None] - kpos[None, :]
        vis = (kpos[None, :] >= 0) & (rel >= 0) & (rel < WINDOW)
        s = jnp.einsum('bghqd,bgkd->bghqk', q_b, k_b)
        p = masked_softmax(s, vis)
        o_win = jnp.einsum('bghqk,bgkd->bghqd', p.astype(v_b.dtype), v_b)
        return g_b[..., 0:1] * o_cmp + g_b[..., 1:2] * o_sel + g_b[..., 2:3] * o_win

    q_blk = q.reshape(B, G, Hg, n_qb, QB, d).transpose(3, 0, 1, 2, 4, 5)
    g_blk = gates.reshape(B, G, Hg, n_qb, QB, 3).transpose(3, 0, 1, 2, 4, 5)
    o = lax.map(query_block, (jnp.arange(n_qb), q_blk, g_blk))
    return o.transpose(1, 0, 4, 2, 3, 5).reshape(B, T, NSA_WIDTH)


def hier_moe(x, w_group, b_group, w_expert, b_expert, w_gate_up, w_down):
    n_tok, d = x.shape
    xf = x.astype(jnp.float32)
    p_group = jax.nn.softmax(xf @ w_group.astype(jnp.float32) + b_group.astype(jnp.float32), axis=-1)
    p_g, grp = lax.top_k(p_group, 1)
    e_logits = (xf @ w_expert.astype(jnp.float32) + b_expert.astype(jnp.float32)).reshape(
        n_tok, N_GROUPS, EXPERTS_PER_GROUP)
    e_logits = jnp.take_along_axis(e_logits, grp[:, :, None], axis=1)[:, 0]
    p_e, local = lax.top_k(jax.nn.softmax(e_logits, axis=-1), TOP_K)
    weight = (p_g * p_e / jnp.sum(p_e, axis=-1, keepdims=True)).astype(x.dtype)
    expert = grp * EXPERTS_PER_GROUP + local

    n_slot = n_tok * TOP_K
    e_flat = expert.reshape(n_slot)
    order = jnp.argsort(e_flat)
    e_sorted = e_flat[order]
    tok_sorted = (order // TOP_K).astype(jnp.int32)
    w_sorted = weight.reshape(n_slot)[order]
    counts = jnp.bincount(e_flat, length=N_EXPERTS)
    padded = (counts + MOE_BLOCK - 1) // MOE_BLOCK * MOE_BLOCK
    pad_end = jnp.cumsum(padded)
    dest = (pad_end - padded)[e_sorted] + jnp.arange(n_slot) - (jnp.cumsum(counts) - counts)[e_sorted]
    n_blk = -(-n_slot // MOE_BLOCK) + N_EXPERTS
    tok_buf = jnp.full((n_blk * MOE_BLOCK,), n_tok, jnp.int32).at[dest].set(tok_sorted)
    w_buf = jnp.zeros((n_blk * MOE_BLOCK,), x.dtype).at[dest].set(w_sorted)
    blk_expert = jnp.minimum(jnp.searchsorted(pad_end, jnp.arange(n_blk) * MOE_BLOCK, side='right'),
                             N_EXPERTS - 1)
    x_pad = jnp.concatenate([x, jnp.zeros((1, d), x.dtype)], axis=0)

    def expert_block(args):
        tok_b, w_b, e = args
        xb = x_pad[tok_b]
        gate, up = jnp.split(xb @ w_gate_up[e], 2, axis=-1)
        return ((jax.nn.silu(gate) * up) @ w_down[e]) * w_b[:, None]

    y_buf = lax.map(expert_block, (tok_buf.reshape(n_blk, MOE_BLOCK), w_buf.reshape(n_blk, MOE_BLOCK),
                                   blk_expert))
    return jax.ops.segment_sum(y_buf.reshape(-1, d), tok_buf, num_segments=n_tok + 1)[:n_tok]


def hybrid_layer(x, g_mix, w_in, gdn_conv_w, gdn_a_log, gdn_dt_bias, gdn_norm_w,
                 cmp_pe_k, cmp_w1_k, cmp_w2_k, cmp_pe_v, cmp_w1_v, cmp_w2_v,
                 w_branch_gdn, w_branch_nsa, w_out, g_ffn,
                 w_group, b_group, w_expert, b_expert, w_gate_up, w_down):
    B, T, D = x.shape
    xn = rms_norm(x, g_mix)
    proj = jnp.einsum('btd,de->bte', xn, w_in)
    offsets = [int(o) for o in np.cumsum(IN_SIZES)[:-1]]
    qkv, z, a_raw, b_raw, nsa_q, nsa_kv, nsa_gate, merge_gate = jnp.split(proj, offsets, axis=-1)
    o_gdn = gdn_mixer(qkv, z, a_raw, b_raw, gdn_conv_w, gdn_a_log, gdn_dt_bias, gdn_norm_w)
    o_nsa = nsa_mixer(nsa_q, nsa_kv, nsa_gate, cmp_pe_k, cmp_w1_k, cmp_w2_k, cmp_pe_v, cmp_w1_v, cmp_w2_v)
    gate_gdn, gate_nsa = jnp.split(jax.nn.sigmoid(merge_gate), 2, axis=-1)
    merged = gate_gdn * (o_gdn @ w_branch_gdn) + gate_nsa * (o_nsa @ w_branch_nsa)
    h = x + merged @ w_out
    ffn = hier_moe(rms_norm(h, g_ffn).reshape(B * T, D), w_group, b_group, w_expert, b_expert,
                   w_gate_up, w_down)
    return h + ffn.reshape(B, T, D)


def setup_inputs(seed: int = 0) -> dict:
    key = jax.random.key(seed)
    ks = jax.random.split(key, 24)
    f32 = jnp.float32
    L, D, H, d = DEPTH, D_MODEL, GDN_HEADS, NSA_HEAD_DIM

    def normal(k, shape, scale):
        return jax.random.normal(k, shape, f32) * scale

    def gain(k, shape):
        return 1.0 + 0.02 * jax.random.normal(k, shape, f32)

    dt = jnp.exp(jax.random.uniform(ks[5], (L, H), f32, math.log(1e-3), math.log(1e-1)))
    return {
        'x': jax.random.normal(ks[0], (BATCH, SEQ, D), f32),
        'g_mix': gain(ks[1], (L, D)),
        'w_in': normal(ks[2], (L, D, IN_COLS), D ** -0.5),
        'gdn_conv_w': normal(ks[3], (L, GDN_CONV, 3 * GDN_WIDTH), GDN_CONV ** -0.5),
        'gdn_a_log': jnp.log(jax.random.uniform(ks[4], (L, H), f32, 1.0, 16.0)),
        'gdn_dt_bias': dt + jnp.log(-jnp.expm1(-dt)),
        'gdn_norm_w': gain(ks[6], (L, GDN_HEAD_DIM)),
        'cmp_pe_k': normal(ks[7], (L, CMP_BLOCK, d), 0.02),
        'cmp_w1_k': normal(ks[8], (L, CMP_BLOCK, d, d), (CMP_BLOCK * d) ** -0.5),
        'cmp_w2_k': normal(ks[9], (L, d, d), d ** -0.5),
        'cmp_pe_v': normal(ks[10], (L, CMP_BLOCK, d), 0.02),
        'cmp_w1_v': normal(ks[11], (L, CMP_BLOCK, d, d), (CMP_BLOCK * d) ** -0.5),
        'cmp_w2_v': normal(ks[12], (L, d, d), d ** -0.5),
        'w_branch_gdn': normal(ks[13], (L, GDN_WIDTH, D), GDN_WIDTH ** -0.5),
        'w_branch_nsa': normal(ks[14], (L, NSA_WIDTH, D), NSA_WIDTH ** -0.5),
        'w_out': normal(ks[15], (L, D, D), D ** -0.5),
        'g_ffn': gain(ks[16], (L, D)),
        'w_group': normal(ks[17], (L, D, N_GROUPS), D ** -0.5),
        'b_group': normal(ks[18], (L, N_GROUPS), 0.01),
        'w_expert': normal(ks[19], (L, D, N_EXPERTS), D ** -0.5),
        'b_expert': normal(ks[20], (L, N_EXPERTS), 0.01),
        'w_gate_up': normal(ks[21], (L, N_EXPERTS, D, 2 * D_FF), D ** -0.5),
        'w_down': normal(ks[22], (L, N_EXPERTS, D_FF, D), D_FF ** -0.5),
        'g_final': gain(ks[23], (D,)),
    }


def reference(x, g_mix, w_in, gdn_conv_w, gdn_a_log, gdn_dt_bias, gdn_norm_w,
              cmp_pe_k, cmp_w1_k, cmp_w2_k, cmp_pe_v, cmp_w1_v, cmp_w2_v,
              w_branch_gdn, w_branch_nsa, w_out, g_ffn,
              w_group, b_group, w_expert, b_expert, w_gate_up, w_down, g_final):
    h = x
    for layer in range(DEPTH):
        h = hybrid_layer(h, g_mix[layer], w_in[layer], gdn_conv_w[layer], gdn_a_log[layer],
                         gdn_dt_bias[layer], gdn_norm_w[layer],
                         cmp_pe_k[layer], cmp_w1_k[layer], cmp_w2_k[layer],
                         cmp_pe_v[layer], cmp_w1_v[layer], cmp_w2_v[layer],
                         w_branch_gdn[layer], w_branch_nsa[layer], w_out[layer], g_ffn[layer],
                         w_group[layer], b_group[layer], w_expert[layer], b_expert[layer],
                         w_gate_up[layer], w_down[layer])
    return rms_norm(h, g_final)
```

```python
import functools
import math

import numpy as np
import jax
import jax.numpy as jnp
from jax import lax
from jax.experimental import pallas as pl
from jax.experimental.pallas import tpu as pltpu

F32 = jnp.float32
BF16 = jnp.bfloat16

RMS_EPS = 1e-6
NEG_INF = -1e30
FORCE_SCORE = 1e6

HEAD_DIM = 128
GDN_CHUNK = 64
NSA_GROUPS = 4
CMP_BLOCK = 32
CMP_STRIDE = 16
SEL_BLOCK = 64
SEL_TOP_N = 16
WINDOW = 512
N_GROUPS = 8
EXPERTS_PER_GROUP = 8
TOP_K = 2
MOE_BLOCK = 128
LANES = 128

V7X_VMEM_BYTES = 64 * 1024 * 1024


def _cparams(semantics, vmem_mb):
    assert vmem_mb * 1024 * 1024 < V7X_VMEM_BYTES
    return pltpu.CompilerParams(dimension_semantics=semantics, vmem_limit_bytes=vmem_mb * 1024 * 1024)


def _dot(a, b, **kw):
    return jnp.dot(a, b, preferred_element_type=F32, **kw)


def _dot_nt(a, b):
    return lax.dot_general(a, b, (((1,), (1,)), ((), ())), preferred_element_type=F32)


def _dot_tn(a, b):
    return lax.dot_general(a, b, (((0,), (0,)), ((), ())), preferred_element_type=F32)


def _sigmoid(x):
    return 1.0 / (1.0 + jnp.exp(-x))


def _silu(x):
    return x * _sigmoid(x)


def _rmsnorm_kernel(x_ref, g_ref, o_ref):
    x = x_ref[...].astype(F32)
    ms = jnp.mean(x * x, axis=-1, keepdims=True)
    o_ref[...] = (x * lax.rsqrt(ms + RMS_EPS) * g_ref[...]).astype(o_ref.dtype)


def _rmsnorm(x, gain, out_dtype, tm=512):
    n, d = x.shape
    return pl.pallas_call(
        _rmsnorm_kernel,
        out_shape=jax.ShapeDtypeStruct((n, d), out_dtype),
        grid=(n // tm,),
        in_specs=[pl.BlockSpec((tm, d), lambda i: (i, 0)), pl.BlockSpec((1, d), lambda i: (0, 0))],
        out_specs=pl.BlockSpec((tm, d), lambda i: (i, 0)),
        compiler_params=_cparams(("parallel",), 40),
        name="rmsnorm",
    )(x, gain.reshape(1, d).astype(F32))


def _mm_kernel(x_ref, w_ref, o_ref):
    o_ref[...] = _dot(x_ref[...], w_ref[...]).astype(o_ref.dtype)


def _matmul(x, w, out_dtype, tm, tn, name):
    m, k = x.shape
    n = w.shape[1]
    return pl.pallas_call(
        _mm_kernel,
        out_shape=jax.ShapeDtypeStruct((m, n), out_dtype),
        grid=(m // tm, n // tn),
        in_specs=[pl.BlockSpec((tm, k), lambda i, j: (i, 0)), pl.BlockSpec((k, tn), lambda i, j: (0, j))],
        out_specs=pl.BlockSpec((tm, tn), lambda i, j: (i, j)),
        compiler_params=_cparams(("parallel", "arbitrary"), 48),
        name=name,
    )(x, w)


def _merge_kernel(a_ref, b_ref, wa_ref, wb_ref, ga_ref, gb_ref, o_ref):
    ya = _dot(a_ref[...], wa_ref[...])
    yb = _dot(b_ref[...], wb_ref[...])
    o_ref[...] = (_sigmoid(ga_ref[...].astype(F32)) * ya + _sigmoid(gb_ref[...].astype(F32)) * yb).astype(o_ref.dtype)


def _merge(o_gdn, o_nsa, w_a, w_b, proj, gate_col0, tm=1024, tn=512):
    n, ka = o_gdn.shape
    kb = o_nsa.shape[1]
    d = w_a.shape[1]
    ja, jb = gate_col0 // tn, (gate_col0 + d) // tn
    return pl.pallas_call(
        _merge_kernel,
        out_shape=jax.ShapeDtypeStruct((n, d), BF16),
        grid=(n // tm, d // tn),
        in_specs=[pl.BlockSpec((tm, ka), lambda i, j: (i, 0)),
                  pl.BlockSpec((tm, kb), lambda i, j: (i, 0)),
                  pl.BlockSpec((ka, tn), lambda i, j: (0, j)),
                  pl.BlockSpec((kb, tn), lambda i, j: (0, j)),
                  pl.BlockSpec((tm, tn), lambda i, j: (i, ja + j)),
                  pl.BlockSpec((tm, tn), lambda i, j: (i, jb + j))],
        out_specs=pl.BlockSpec((tm, tn), lambda i, j: (i, j)),
        compiler_params=_cparams(("parallel", "arbitrary"), 48),
        name="branch_merge",
    )(o_gdn, o_nsa, w_a, w_b, proj, proj)


def _resid_mm_kernel(x_ref, w_ref, r_ref, o_ref):
    o_ref[...] = r_ref[...] + _dot(x_ref[...], w_ref[...])


def _resid_matmul(x, w, resid, tm=1024, tn=512):
    m, k = x.shape
    n = w.shape[1]
    return pl.pallas_call(
        _resid_mm_kernel,
        out_shape=jax.ShapeDtypeStruct((m, n), F32),
        grid=(m // tm, n // tn),
        in_specs=[pl.BlockSpec((tm, k), lambda i, j: (i, 0)),
                  pl.BlockSpec((k, tn), lambda i, j: (0, j)),
                  pl.BlockSpec((tm, tn), lambda i, j: (i, j))],
        out_specs=pl.BlockSpec((tm, tn), lambda i, j: (i, j)),
        compiler_params=_cparams(("parallel", "arbitrary"), 48),
        name="out_proj_residual",
    )(x, w, resid)


def _gdn_gate_kernel(a_ref, b_ref, alog_ref, dtb_ref, gc_ref, beta_ref, *, chunk):
    tm = a_ref.shape[0]
    x = a_ref[...] + dtb_ref[...]
    softplus = jnp.maximum(x, 0.0) + jnp.log(1.0 + jnp.exp(-jnp.abs(x)))
    g = -jnp.exp(alog_ref[...]) * softplus
    row = lax.broadcasted_iota(jnp.int32, (tm, tm), 0)
    col = lax.broadcasted_iota(jnp.int32, (tm, tm), 1)
    tri = jnp.where((col <= row) & (row // chunk == col // chunk), 1.0, 0.0).astype(F32)
    gc_ref[...] = _dot(tri, g, precision=lax.Precision.HIGHEST)
    beta_ref[...] = _sigmoid(b_ref[...])


def _gdn_gates(a_raw, b_raw, a_log, dt_bias, chunk, tm=512):
    n, h = a_raw.shape
    spec = pl.BlockSpec((tm, h), lambda i: (i, 0))
    vec = pl.BlockSpec((1, h), lambda i: (0, 0))
    return pl.pallas_call(
        functools.partial(_gdn_gate_kernel, chunk=chunk),
        out_shape=(jax.ShapeDtypeStruct((n, h), F32), jax.ShapeDtypeStruct((n, h), F32)),
        grid=(n // tm,),
        in_specs=[spec, spec, vec, vec],
        out_specs=(spec, spec),
        compiler_params=_cparams(("parallel",), 32),
        name="gdn_gates",
    )(a_raw, b_raw, a_log.reshape(1, h).astype(F32), dt_bias.reshape(1, h).astype(F32))


_CONV_PAD = 8


def _gdn_kernel(q_ref, k_ref, v_ref, z_ref, cwq_ref, cwk_ref, cwv_ref, gc_ref, beta_ref, gct_ref, nw_ref,
                o_ref, state, xbuf, *, chunk):
    h = pl.program_id(1)
    t = pl.program_id(2)
    tt, d = q_ref.shape
    width = cwq_ref.shape[0]
    c = chunk

    @pl.when(t == 0)
    def _():
        state[...] = jnp.zeros_like(state)
        xbuf[:, 0:_CONV_PAD, :] = jnp.zeros((3, _CONV_PAD, d), F32)

    def conv_silu(s, ref, cw_ref):
        xbuf[s, _CONV_PAD:_CONV_PAD + tt, :] = ref[...].astype(F32)
        w = cw_ref[...]
        y = None
        for j in range(width):
            term = xbuf[s, pl.ds(_CONV_PAD - (width - 1) + j, tt), :] * w[j:j + 1, :]
            y = term if y is None else y + term
        xbuf[s, 0:_CONV_PAD, :] = xbuf[s, tt:tt + _CONV_PAD, :]
        return _silu(y)

    def l2n(x):
        return x * lax.rsqrt(jnp.sum(x * x, axis=-1, keepdims=True) + RMS_EPS)

    q_all = l2n(conv_silu(0, q_ref, cwq_ref)) * (d ** -0.5)
    k_all = l2n(conv_silu(1, k_ref, cwk_ref))
    v_all = conv_silu(2, v_ref, cwv_ref)

    lane = lax.broadcasted_iota(jnp.int32, gc_ref.shape, 1)
    gc_col = jnp.sum(jnp.where(lane == h, gc_ref[...], 0.0), axis=1, keepdims=True)
    beta_col = jnp.sum(jnp.where(lane == h, beta_ref[...], 0.0), axis=1, keepdims=True)
    gc_row_all = gct_ref[0]

    ri = lax.broadcasted_iota(jnp.int32, (c, c), 0)
    ci = lax.broadcasted_iota(jnp.int32, (c, c), 1)
    incl = ri >= ci
    strict = ri > ci
    eye = jnp.where(ri == ci, 1.0, 0.0).astype(F32)
    nw = nw_ref[...]
    n_sq = int(math.log2(c)) - 1
    assert 2 ** (n_sq + 1) == c

    for ch in range(tt // c):
        sl = slice(ch * c, (ch + 1) * c)
        q, k, v = q_all[sl], k_all[sl], v_all[sl]
        gcc, bc, gr = gc_col[sl], beta_col[sl], gc_row_all[:, sl]
        kb = k.astype(BF16)
        qk = _dot_nt(jnp.concatenate([q.astype(BF16), kb], axis=0), kb)
        dm = jnp.exp(jnp.where(incl, gcc - gr, NEG_INF))
        neg_l = jnp.where(strict, -(qk[c:] * dm * bc), 0.0)
        p = eye + neg_l
        npow = neg_l
        for _ in range(n_sq):
            nb = npow.astype(BF16)
            npow = _dot(nb, nb)
            p = p + _dot(p.astype(BF16), npow.astype(BF16))
        egc = jnp.exp(gcc)
        rhs = jnp.concatenate([v * bc, k * (bc * egc)], axis=1).astype(BF16)
        sol = _dot(p.astype(BF16), rhs)
        u, w = sol[:, :d], sol[:, d:]
        attn = jnp.where(incl, qk[:c] * dm, 0.0)
        s_old = state[...]
        wq = _dot(jnp.concatenate([w, q * egc], axis=0).astype(BF16), s_old.astype(BF16))
        v_new = u - wq[:c]
        o = wq[c:] + _dot(attn.astype(BF16), v_new.astype(BF16))
        g_last = gcc[c - 1:c, :]
        k_out = k * jnp.exp(g_last - gcc)
        state[...] = s_old * jnp.exp(g_last) + _dot_tn(k_out.astype(BF16), v_new.astype(BF16))
        on = o * lax.rsqrt(jnp.mean(o * o, axis=-1, keepdims=True) + RMS_EPS) * nw
        o_ref[sl, :] = (on * _silu(z_ref[sl, :].astype(F32))).astype(o_ref.dtype)


def _gdn_mixer(proj, conv_w, gc, beta, gct, norm_w, batch, seq, heads, chunk, tt=256):
    d = HEAD_DIM
    n = batch * seq
    nt = seq // tt
    width = conv_w.shape[0]
    assert width - 1 <= _CONV_PAD and tt % chunk == 0 and seq % tt == 0

    def slab(off):
        return pl.BlockSpec((tt, d), lambda b, h, t: (b * nt + t, off + h))

    def cw(off):
        return pl.BlockSpec((width, d), lambda b, h, t: (0, off + h))

    gspec = pl.BlockSpec((tt, heads), lambda b, h, t: (b * nt + t, 0))
    return pl.pallas_call(
        functools.partial(_gdn_kernel, chunk=chunk),
        out_shape=jax.ShapeDtypeStruct((n, heads * d), BF16),
        grid=(batch, heads, nt),
        in_specs=[slab(0), slab(heads), slab(2 * heads), slab(3 * heads),
                  cw(0), cw(heads), cw(2 * heads),
                  gspec, gspec,
                  pl.BlockSpec((1, 1, tt), lambda b, h, t: (b * heads + h, 0, t)),
                  pl.BlockSpec((1, d), lambda b, h, t: (0, 0))],
        out_specs=pl.BlockSpec((tt, d), lambda b, h, t: (b * nt + t, h)),
        scratch_shapes=[pltpu.VMEM((d, d), F32), pltpu.VMEM((3, tt + _CONV_PAD, d), F32)],
        compiler_params=_cparams(("parallel", "parallel", "arbitrary"), 32),
        name="gdn_mixer",
    )(proj, proj, proj, proj, conv_w, conv_w, conv_w, gc, beta, gct, norm_w.reshape(1, d).astype(F32))


def _nsa_cmp_kernel(kc_ref, vc_ref, pek_ref, w1k_ref, w2k_ref, pev_ref, w1v_ref, w2v_ref,
                    ko_ref, vo_ref, xf, sh, *, stride, block):
    seq, d = kc_ref.shape
    nh = seq // stride
    reps = block // stride
    assert nh % 8 == 0

    def compress(src_ref, pe_ref, w1_ref, w2_ref, dst_ref):
        xf[...] = src_ref[...].astype(F32)
        pe = pe_ref[...]
        hid = None
        for r in range(reps):
            acc = None
            for l in range(stride):
                li = r * stride + l
                rows = xf[pl.ds(l, nh, stride=stride), :] + pe[li:li + 1, :]
                term = _dot(rows.astype(BF16), w1_ref[li].astype(BF16))
                acc = term if acc is None else acc + term
            if r == 0:
                hid = acc
            else:
                sh[0:nh, :] = acc
                sh[nh:nh + 8, :] = jnp.zeros((8, d), F32)
                hid = hid + sh[pl.ds(r, nh), :]
        hid = _silu(hid)
        dst_ref[0] = _dot(hid.astype(BF16), w2_ref[...].astype(BF16)).astype(dst_ref.dtype)

    compress(kc_ref, pek_ref, w1k_ref, w2k_ref, ko_ref)
    compress(vc_ref, pev_ref, w1v_ref, w2v_ref, vo_ref)


def _nsa_compress(proj, kc_col, vc_col, pe_k, w1_k, w2_k, pe_v, w1_v, w2_v, batch, seq, groups):
    d = HEAD_DIM
    nh = seq // CMP_STRIDE
    full2 = lambda b, g: (0, 0)
    full3 = lambda b, g: (0, 0, 0)
    out_sds = jax.ShapeDtypeStruct((batch * groups, nh, d), BF16)
    out_spec = pl.BlockSpec((1, nh, d), lambda b, g: (b * groups + g, 0, 0))
    return pl.pallas_call(
        functools.partial(_nsa_cmp_kernel, stride=CMP_STRIDE, block=CMP_BLOCK),
        out_shape=(out_sds, out_sds),
        grid=(batch, groups),
        in_specs=[pl.BlockSpec((seq, d), lambda b, g: (b, kc_col + g)),
                  pl.BlockSpec((seq, d), lambda b, g: (b, vc_col + g)),
                  pl.BlockSpec((CMP_BLOCK, d), full2), pl.BlockSpec((CMP_BLOCK, d, d), full3), pl.BlockSpec((d, d), full2),
                  pl.BlockSpec((CMP_BLOCK, d), full2), pl.BlockSpec((CMP_BLOCK, d, d), full3), pl.BlockSpec((d, d), full2)],
        out_specs=(out_spec, out_spec),
        scratch_shapes=[pltpu.VMEM((seq, d), F32), pltpu.VMEM((nh + 8, d), F32)],
        compiler_params=_cparams(("parallel", "parallel"), 32),
        name="nsa_compress",
    )(proj, proj, pe_k, w1_k, w2_k, pe_v, w1_v, w2_v)


def _nsa_kernel(q_ref, kc_ref, vc_ref, ks_ref, vs_ref, kw_ref, vw_ref, gate_ref, selmap_ref, expand_ref,
                o_ref, *, hpg, tk, n_sel, n_top):
    tq = q_ref.shape[0]
    d = HEAD_DIM
    seq = ks_ref.shape[0]
    n_cmp_pad = kc_ref.shape[1]
    t0 = pl.program_id(2) * tq
    scale = d ** -0.5
    rows = hpg * tq

    qb = q_ref[...]
    q4 = jnp.concatenate([qb[:, hh * d:(hh + 1) * d] for hh in range(hpg)], axis=0)
    trow = t0 + lax.broadcasted_iota(jnp.int32, (tq, 1), 0)

    def masked_softmax3(s, ok):
        n = s.shape[-1]
        s3 = jnp.where(ok[None], s.reshape(hpg, tq, n), NEG_INF)
        m = jnp.max(s3, axis=-1, keepdims=True)
        e = jnp.where(ok[None], jnp.exp(s3 - m), 0.0)
        den = jnp.sum(e, axis=-1, keepdims=True)
        return e / jnp.where(den > 0.0, den, 1.0)

    ncol = lax.broadcasted_iota(jnp.int32, (1, n_cmp_pad), 1)
    vis_c = (ncol * CMP_STRIDE + (CMP_BLOCK - 1)) <= trow
    p_c = masked_softmax3(_dot_nt(q4, kc_ref[0]) * scale, vis_c)
    o_cmp = _dot(p_c.reshape(rows, n_cmp_pad).astype(BF16), vc_ref[0])

    imp = _dot(jnp.sum(p_c, axis=0), selmap_ref[...], precision=lax.Precision.HIGHEST)
    jcol = lax.broadcasted_iota(jnp.int32, (1, LANES), 1)
    cur = trow // SEL_BLOCK
    forced = (jcol == 0) | (jcol == cur) | (jcol == cur - 1)
    causal = (jcol * SEL_BLOCK <= trow) & (jcol < n_sel)
    score = jnp.where(forced & (jcol < n_sel), FORCE_SCORE, jnp.where(causal, imp, NEG_INF))
    cnt = jnp.zeros((tq, LANES), F32)
    for i in range(n_sel):
        ci = score[:, i:i + 1]
        ahead = (ci > score) | ((ci == score) & (jcol > i))
        cnt = cnt + jnp.where(ahead, 1.0, 0.0)
    sel = jnp.where((cnt < n_top) & (jcol < n_sel), 1.0, 0.0).astype(BF16)

    def sel_step(kt, carry):
        m, l, acc = carry
        k0 = pl.multiple_of(kt * tk, tk)
        s = _dot_nt(q4, ks_ref[pl.ds(k0, tk), :]) * scale
        picked = _dot(sel, expand_ref[:, pl.ds(k0, tk)])
        kpos = k0 + lax.broadcasted_iota(jnp.int32, (1, tk), 1)
        ok = (picked > 0.5) & (kpos <= trow)
        s3 = jnp.where(ok[None], s.reshape(hpg, tq, tk), NEG_INF)
        m_new = jnp.maximum(m, jnp.max(s3, axis=-1, keepdims=True))
        alpha = jnp.exp(m - m_new)
        e = jnp.where(ok[None], jnp.exp(s3 - m_new), 0.0)
        l = alpha * l + jnp.sum(e, axis=-1, keepdims=True)
        pv = _dot(e.reshape(rows, tk).astype(BF16), vs_ref[pl.ds(k0, tk), :])
        acc = alpha * acc + pv.reshape(hpg, tq, d)
        return m_new, l, acc

    n_kt = (t0 + tq + tk - 1) // tk
    m0 = jnp.full((hpg, tq, 1), NEG_INF, F32)
    l0 = jnp.zeros((hpg, tq, 1), F32)
    a0 = jnp.zeros((hpg, tq, d), F32)
    _, l_s, acc_s = lax.fori_loop(0, n_kt, sel_step, (m0, l0, a0))
    o_sel = acc_s / jnp.where(l_s > 0.0, l_s, 1.0)

    span = WINDOW + tq
    ws = pl.multiple_of(jnp.maximum(t0 - WINDOW, 0), tq)
    kpos = ws + lax.broadcasted_iota(jnp.int32, (1, span), 1)
    rel = trow - kpos
    vis_w = (rel >= 0) & (rel < WINDOW)
    p_w = masked_softmax3(_dot_nt(q4, kw_ref[pl.ds(ws, span), :]) * scale, vis_w)
    o_win = _dot(p_w.reshape(rows, span).astype(BF16), vw_ref[pl.ds(ws, span), :])

    gates = _sigmoid(gate_ref[0])
    o_cmp3 = o_cmp.reshape(hpg, tq, d)
    o_win3 = o_win.reshape(hpg, tq, d)
    for hh in range(hpg):
        g0 = gates[:, 3 * hh:3 * hh + 1]
        g1 = gates[:, 3 * hh + 1:3 * hh + 2]
        g2 = gates[:, 3 * hh + 2:3 * hh + 3]
        o_ref[:, hh * d:(hh + 1) * d] = (g0 * o_cmp3[hh] + g1 * o_sel[hh] + g2 * o_win3[hh]).astype(o_ref.dtype)


def _nsa_mixer(proj, q_col, ks_col, vs_col, kw_col, vw_col, k_cmp, v_cmp, gates, batch, seq, groups, hpg,
               tq=128, tk=512):
    d = HEAD_DIM
    n = batch * seq
    nq = seq // tq
    n_sel = seq // SEL_BLOCK
    n_top = min(SEL_TOP_N, n_sel)
    n_cmp = (seq - CMP_BLOCK) // CMP_STRIDE + 1
    n_cmp_pad = k_cmp.shape[1]
    assert n_sel <= LANES and seq >= WINDOW + tq and WINDOW % tq == 0 and seq % tk == 0 and tk % SEL_BLOCK == 0

    cmp_start = np.arange(n_cmp_pad) * CMP_STRIDE
    sel_start = np.arange(LANES) * SEL_BLOCK
    overlap = (np.minimum(cmp_start[:, None] + CMP_BLOCK, sel_start[None, :] + SEL_BLOCK)
               - np.maximum(cmp_start[:, None], sel_start[None, :]))
    sel_map = np.clip(overlap, 0, None) / CMP_STRIDE
    sel_map[n_cmp:, :] = 0.0
    sel_map[:, n_sel:] = 0.0
    expand = (np.arange(seq)[None, :] // SEL_BLOCK == np.arange(LANES)[:, None]).astype(np.float32)

    def kv(col):
        return pl.BlockSpec((seq, d), lambda b, g, t: (b, col + g))

    cmp_spec = pl.BlockSpec((1, n_cmp_pad, d), lambda b, g, t: (b * groups + g, 0, 0))
    return pl.pallas_call(
        functools.partial(_nsa_kernel, hpg=hpg, tk=tk, n_sel=n_sel, n_top=n_top),
        out_shape=jax.ShapeDtypeStruct((n, groups * hpg * d), BF16),
        grid=(batch, groups, nq),
        in_specs=[pl.BlockSpec((tq, hpg * d), lambda b, g, t: (b * nq + t, q_col // hpg + g)),
                  cmp_spec, cmp_spec, kv(ks_col), kv(vs_col), kv(kw_col), kv(vw_col),
                  pl.BlockSpec((1, tq, 3 * hpg), lambda b, g, t: (b * groups + g, t, 0)),
                  pl.BlockSpec((n_cmp_pad, LANES), lambda b, g, t: (0, 0)),
                  pl.BlockSpec((LANES, seq), lambda b, g, t: (0, 0))],
        out_specs=pl.BlockSpec((tq, hpg * d), lambda b, g, t: (b * nq + t, g)),
        compiler_params=_cparams(("parallel", "parallel", "arbitrary"), 48),
        name="nsa_attention",
    )(proj, k_cmp, v_cmp, proj, proj, proj, proj, gates, jnp.asarray(sel_map, F32), jnp.asarray(expand, BF16))


def _router_kernel(h_ref, g_ref, wr_ref, br_ref, xn_ref, ids_ref, wts_ref):
    x = h_ref[...]
    xn = x * lax.rsqrt(jnp.mean(x * x, axis=-1, keepdims=True) + RMS_EPS) * g_ref[...]
    xn_ref[...] = xn
    logits = _dot(xn, wr_ref[...], precision=lax.Precision.HIGHEST) + br_ref[...]
    lane = lax.broadcasted_iota(jnp.int32, logits.shape, 1)
    big = jnp.int32(2 * LANES)
    is_g = lane < N_GROUPS
    gl = jnp.where(is_g, logits, NEG_INF)
    gm = jnp.max(gl, axis=-1, keepdims=True)
    p_g = 1.0 / jnp.sum(jnp.where(is_g, jnp.exp(gl - gm), 0.0), axis=-1, keepdims=True)
    grp = jnp.min(jnp.where(gl == gm, lane, big), axis=-1, keepdims=True)
    eidx = lane - N_GROUPS
    in_grp = (eidx >= 0) & (eidx // EXPERTS_PER_GROUP == grp) & (eidx < N_GROUPS * EXPERTS_PER_GROUP)
    el = jnp.where(in_grp, logits, NEG_INF)
    em = jnp.max(el, axis=-1, keepdims=True)
    ee = jnp.where(in_grp, jnp.exp(el - em), 0.0)
    pe = ee / jnp.sum(ee, axis=-1, keepdims=True)
    p1 = jnp.max(jnp.where(in_grp, pe, -1.0), axis=-1, keepdims=True)
    i1 = jnp.min(jnp.where(in_grp & (pe == p1), lane, big), axis=-1, keepdims=True)
    rest = in_grp & (lane != i1)
    p2 = jnp.max(jnp.where(rest, pe, -1.0), axis=-1, keepdims=True)
    i2 = jnp.min(jnp.where(rest & (pe == p2), lane, big), axis=-1, keepdims=True)
    denom = p1 + p2
    ids_ref[...] = jnp.where(lane == 0, i1 - N_GROUPS, i2 - N_GROUPS)
    wts_ref[...] = jnp.where(lane == 0, p_g * p1 / denom, p_g * p2 / denom)


def _router(h, g_ffn, w_group, b_group, w_expert, b_expert, tm=256):
    n, d = h.shape
    n_r = w_group.shape[1] + w_expert.shape[1]
    assert n_r <= LANES and TOP_K == 2
    wr = jnp.concatenate([w_group, w_expert, jnp.zeros((d, LANES - n_r), F32)], axis=1).astype(F32)
    br = jnp.concatenate([b_group, b_expert, jnp.zeros((LANES - n_r,), F32)]).reshape(1, LANES).astype(F32)
    row = lambda i: (i, 0)
    fixed = lambda i: (0, 0)
    xn, ids, wts = pl.pallas_call(
        _router_kernel,
        out_shape=(jax.ShapeDtypeStruct((n, d), F32), jax.ShapeDtypeStruct((n, LANES), jnp.int32),
                   jax.ShapeDtypeStruct((n, LANES), F32)),
        grid=(n // tm,),
        in_specs=[pl.BlockSpec((tm, d), row), pl.BlockSpec((1, d), fixed),
                  pl.BlockSpec((d, LANES), fixed), pl.BlockSpec((1, LANES), fixed)],
        out_specs=(pl.BlockSpec((tm, d), row), pl.BlockSpec((tm, LANES), row), pl.BlockSpec((tm, LANES), row)),
        compiler_params=_cparams(("parallel",), 40),
        name="moe_router",
    )(h, g_ffn.reshape(1, d).astype(F32), wr, br)
    return xn, ids[:, :TOP_K], wts[:, :TOP_K]


def _row_gather(idx_ref, base, src_hbm, dst, sem, n_rows, wait):
    def body(r, carry):
        cp = pltpu.make_async_copy(src_hbm.at[pl.ds(idx_ref[base + r], 1), :], dst.at[pl.ds(r, 1), :], sem)
        if wait:
            cp.wait()
        else:
            cp.start()
        return carry
    lax.fori_loop(0, n_rows, body, 0)


def _expert_kernel(be_ref, tok_ref, nact_ref, x_hbm, wgu_ref, wd_ref, wrow_ref, y_ref, xbuf, sem, *, blk, d_ff):
    i = pl.program_id(0)
    n_act = nact_ref[0]
    slot = i % 2

    @pl.when(i == 0)
    def _():
        _row_gather(tok_ref, 0, x_hbm, xbuf.at[0], sem.at[0], blk, wait=False)

    @pl.when(i + 1 < n_act)
    def _():
        _row_gather(tok_ref, (i + 1) * blk, x_hbm, xbuf.at[1 - slot], sem.at[1 - slot], blk, wait=False)

    @pl.when(i < n_act)
    def _():
        _row_gather(tok_ref, i * blk, x_hbm, xbuf.at[slot], sem.at[slot], blk, wait=True)
        xb = xbuf[slot].astype(BF16)
        gu = _dot(xb, wgu_ref[0])
        act = _silu(gu[:, :d_ff]) * gu[:, d_ff:]
        y_ref[...] = _dot(act.astype(BF16), wd_ref[0]) * wrow_ref[...]

    @pl.when(i >= n_act)
    def _():
        y_ref[...] = jnp.zeros_like(y_ref)


def _experts(xn, w_gate_up, w_down, blk_expert, tok_buf, n_active, w_buf, blk):
    n, d = xn.shape
    n_exp, _, ff2 = w_gate_up.shape
    d_ff = ff2 // 2
    n_blk = blk_expert.shape[0]
    return pl.pallas_call(
        functools.partial(_expert_kernel, blk=blk, d_ff=d_ff),
        out_shape=jax.ShapeDtypeStruct((n_blk * blk, d), F32),
        grid_spec=pltpu.PrefetchScalarGridSpec(
            num_scalar_prefetch=3,
            grid=(n_blk,),
            in_specs=[pl.BlockSpec(memory_space=pl.ANY),
                      pl.BlockSpec((1, d, ff2), lambda i, be, tok, na: (be[i], 0, 0)),
                      pl.BlockSpec((1, d_ff, d), lambda i, be, tok, na: (be[i], 0, 0)),
                      pl.BlockSpec((blk, 1), lambda i, be, tok, na: (i, 0))],
            out_specs=pl.BlockSpec((blk, d), lambda i, be, tok, na: (i, 0)),
            scratch_shapes=[pltpu.VMEM((2, blk, d), F32), pltpu.SemaphoreType.DMA((2,))]),
        compiler_params=_cparams(("arbitrary",), 56),
        name="moe_experts",
    )(blk_expert, tok_buf, n_active, xn, w_gate_up, w_down, w_buf.reshape(n_blk * blk, 1))


def _combine_kernel(pos_ref, h_ref, g_ref, y_hbm, o_ref, ybuf, sem, *, tc):
    i = pl.program_id(0)
    n = pl.num_programs(0)
    slot = i % 2
    rows = TOP_K * tc

    @pl.when(i == 0)
    def _():
        _row_gather(pos_ref, 0, y_hbm, ybuf.at[0], sem.at[0], rows, wait=False)

    @pl.when(i + 1 < n)
    def _():
        _row_gather(pos_ref, (i + 1) * rows, y_hbm, ybuf.at[1 - slot], sem.at[1 - slot], rows, wait=False)

    _row_gather(pos_ref, i * rows, y_hbm, ybuf.at[slot], sem.at[slot], rows, wait=True)
    hh = h_ref[...]
    for kk in range(TOP_K):
        hh = hh + ybuf[slot, kk * tc:(kk + 1) * tc, :]
    o_ref[...] = hh * lax.rsqrt(jnp.mean(hh * hh, axis=-1, keepdims=True) + RMS_EPS) * g_ref[...]


def _combine(h, y_buf, pos, g_final, tc=128):
    n, d = h.shape
    return pl.pallas_call(
        functools.partial(_combine_kernel, tc=tc),
        out_shape=jax.ShapeDtypeStruct((n, d), F32),
        grid_spec=pltpu.PrefetchScalarGridSpec(
            num_scalar_prefetch=1,
            grid=(n // tc,),
            in_specs=[pl.BlockSpec((tc, d), lambda i, pos: (i, 0)),
                      pl.BlockSpec((1, d), lambda i, pos: (0, 0)),
                      pl.BlockSpec(memory_space=pl.ANY)],
            out_specs=pl.BlockSpec((tc, d), lambda i, pos: (i, 0)),
            scratch_shapes=[pltpu.VMEM((2, TOP_K * tc, d), F32), pltpu.SemaphoreType.DMA((2,))]),
        compiler_params=_cparams(("arbitrary",), 40),
        name="moe_combine_final_norm",
    )(pos, h, g_final.reshape(1, d).astype(F32), y_buf)


def _dispatch_plan(expert, weight, n_experts, blk):
    n_tok, top_k = expert.shape
    n_slot = n_tok * top_k
    e_flat = expert.reshape(n_slot)
    order = jnp.argsort(e_flat)
    e_sorted = e_flat[order]
    counts = jnp.bincount(e_flat, length=n_experts)
    padded = (counts + blk - 1) // blk * blk
    pad_end = jnp.cumsum(padded)
    dest = ((pad_end - padded)[e_sorted] + jnp.arange(n_slot) - (jnp.cumsum(counts) - counts)[e_sorted]).astype(jnp.int32)
    n_blk = -(-n_slot // blk) + n_experts
    tok_buf = jnp.zeros((n_blk * blk,), jnp.int32).at[dest].set((order // top_k).astype(jnp.int32))
    w_buf = jnp.zeros((n_blk * blk,), F32).at[dest].set(weight.reshape(n_slot)[order])
    blk_expert = jnp.minimum(jnp.searchsorted(pad_end, jnp.arange(n_blk) * blk, side='right'),
                             n_experts - 1).astype(jnp.int32)
    n_active = (pad_end[-1] // blk).astype(jnp.int32).reshape(1)
    pos = jnp.zeros((n_slot,), jnp.int32).at[order].set(dest).reshape(n_tok, top_k)
    return tok_buf, w_buf, blk_expert, n_active, pos


def _layer(x, g_mix, w_in, conv_w, a_log, dt_bias, norm_w, pe_k, w1_k, w2_k, pe_v, w1_v, w2_v,
           w_a, w_b, w_out, g_ffn, w_group, b_group, w_expert, b_expert, w_gate_up, w_down, g_out):
    batch, seq, dm = x.shape
    n = batch * seq
    d = HEAD_DIM
    gdn_w = w_a.shape[0]
    nsa_w = w_b.shape[0]
    heads = gdn_w // d
    nsa_heads = nsa_w // d
    groups = NSA_GROUPS
    hpg = nsa_heads // groups
    kvw = groups * d

    sizes = (3 * gdn_w, gdn_w, heads, heads, nsa_w, 6 * kvw, 3 * nsa_heads, 2 * dm)
    assert sum(sizes) == w_in.shape[1]
    offs = np.concatenate([[0], np.cumsum(sizes)])
    seg = lambda i: w_in[:, offs[i]:offs[i + 1]]
    w_big = jnp.concatenate([seg(0), seg(1), seg(4), seg(5), seg(7)], axis=1).astype(BF16)
    w_small = jnp.concatenate([seg(2), seg(3), seg(6)], axis=1).astype(BF16)
    z_col = 3 * heads
    nq_col = z_col + heads
    kv_col = nq_col + nsa_heads
    gate_col0 = (kv_col + 6 * groups) * d

    x2 = x.reshape(n, dm)
    xn = _rmsnorm(x2, g_mix, BF16)
    proj = _matmul(xn, w_big, BF16, tm=1024, tn=512, name="in_proj")
    small = _matmul(xn, w_small, F32, tm=1024, tn=w_small.shape[1], name="in_proj_small")
    a_raw, b_raw, nsa_gate = small[:, :heads], small[:, heads:2 * heads], small[:, 2 * heads:]

    gc, beta = _gdn_gates(a_raw, b_raw, a_log, dt_bias, GDN_CHUNK)
    gct = gc.reshape(batch, seq, heads).transpose(0, 2, 1).reshape(batch * heads, 1, seq)
    o_gdn = _gdn_mixer(proj, conv_w, gc, beta, gct, norm_w, batch, seq, heads, GDN_CHUNK)

    k_cmp, v_cmp = _nsa_compress(proj, kv_col, kv_col + groups, pe_k, w1_k, w2_k, pe_v, w1_v, w2_v,
                                 batch, seq, groups)
    gates = nsa_gate.reshape(batch, seq, groups, 3 * hpg).transpose(0, 2, 1, 3).reshape(batch * groups, seq, 3 * hpg)
    o_nsa = _nsa_mixer(proj, nq_col, kv_col + 2 * groups, kv_col + 3 * groups, kv_col + 4 * groups,
                       kv_col + 5 * groups, k_cmp, v_cmp, gates, batch, seq, groups, hpg)

    merged = _merge(o_gdn, o_nsa, w_a.astype(BF16), w_b.astype(BF16), proj, gate_col0)
    h = _resid_matmul(merged, w_out.astype(BF16), x2)

    n_experts = w_gate_up.shape[0]
    xn2, expert, weight = _router(h, g_ffn, w_group, b_group, w_expert, b_expert)
    tok_buf, w_buf, blk_expert, n_active, pos = _dispatch_plan(expert, weight, n_experts, MOE_BLOCK)
    y_buf = _experts(xn2, w_gate_up.astype(BF16), w_down.astype(BF16), blk_expert, tok_buf, n_active, w_buf, MOE_BLOCK)
    tc = 128
    pos_tiles = pos.reshape(n // tc, tc, TOP_K).transpose(0, 2, 1).reshape(n * TOP_K)
    out = _combine(h, y_buf, pos_tiles, g_out, tc)
    return out.reshape(batch, seq, dm)


def kernel(x, g_mix, w_in, gdn_conv_w, gdn_a_log, gdn_dt_bias, gdn_norm_w, cmp_pe_k, cmp_w1_k, cmp_w2_k,
           cmp_pe_v, cmp_w1_v, cmp_w2_v, w_branch_gdn, w_branch_nsa, w_out, g_ffn, w_group, b_group,
           w_expert, b_expert, w_gate_up, w_down, g_final):
    depth = g_mix.shape[0]
    assert depth == 1, "the fused final norm assumes a single layer"
    return _layer(x, g_mix[0], w_in[0], gdn_conv_w[0], gdn_a_log[0], gdn_dt_bias[0], gdn_norm_w[0],
                  cmp_pe_k[0], cmp_w1_k[0], cmp_w2_k[0], cmp_pe_v[0], cmp_w1_v[0], cmp_w2_v[0],
                  w_branch_gdn[0], w_branch_nsa[0], w_out[0], g_ffn[0], w_group[0], b_group[0],
                  w_expert[0], b_expert[0], w_gate_up[0], w_down[0], g_final)
```

```python
import functools
import math

import numpy as np
import jax
import jax.numpy as jnp
from jax import lax
from jax.experimental import pallas as pl
from jax.experimental.pallas import tpu as pltpu

F32 = jnp.float32
BF16 = jnp.bfloat16

RMS_EPS = 1e-6
NEG_INF = -1e30
FORCE_SCORE = 1e6

HEAD_DIM = 128
GDN_CHUNK = 64
NSA_GROUPS = 4
CMP_BLOCK = 32
CMP_STRIDE = 16
SEL_BLOCK = 64
SEL_TOP_N = 16
WINDOW = 512
N_GROUPS = 8
EXPERTS_PER_GROUP = 8
TOP_K = 2
MOE_BLOCK = 128
LANES = 128

V7X_VMEM_BYTES = 64 * 1024 * 1024


def _cparams(semantics, vmem_mb):
    assert vmem_mb * 1024 * 1024 < V7X_VMEM_BYTES
    return pltpu.CompilerParams(dimension_semantics=semantics, vmem_limit_bytes=vmem_mb * 1024 * 1024)


def _dot(a, b, **kw):
    return jnp.dot(a, b, preferred_element_type=F32, **kw)


def _dot_nt(a, b):
    return lax.dot_general(a, b, (((1,), (1,)), ((), ())), preferred_element_type=F32)


def _dot_tn(a, b):
    return lax.dot_general(a, b, (((0,), (0,)), ((), ())), preferred_element_type=F32)


def _sigmoid(x):
    return 1.0 / (1.0 + jnp.exp(-x))


def _silu(x):
    return x * _sigmoid(x)


def _rmsnorm_kernel(x_ref, g_ref, o_ref):
    x = x_ref[...].astype(F32)
    ms = jnp.mean(x * x, axis=-1, keepdims=True)
    o_ref[...] = (x * lax.rsqrt(ms + RMS_EPS) * g_ref[...]).astype(o_ref.dtype)


def _rmsnorm(x, gain, out_dtype, tm=512):
    n, d = x.shape
    return pl.pallas_call(
        _rmsnorm_kernel,
        out_shape=jax.ShapeDtypeStruct((n, d), out_dtype),
        grid=(n // tm,),
        in_specs=[pl.BlockSpec((tm, d), lambda i: (i, 0)), pl.BlockSpec((1, d), lambda i: (0, 0))],
        out_specs=pl.BlockSpec((tm, d), lambda i: (i, 0)),
        compiler_params=_cparams(("parallel",), 40),
        name="rmsnorm",
    )(x, gain.reshape(1, d).astype(F32))


def _mm_kernel(x_ref, w_ref, o_ref):
    o_ref[...] = _dot(x_ref[...], w_ref[...]).astype(o_ref.dtype)


def _matmul(x, w, out_dtype, tm, tn, name):
    m, k = x.shape
    n = w.shape[1]
    return pl.pallas_call(
        _mm_kernel,
        out_shape=jax.ShapeDtypeStruct((m, n), out_dtype),
        grid=(m // tm, n // tn),
        in_specs=[pl.BlockSpec((tm, k), lambda i, j: (i, 0)), pl.BlockSpec((k, tn), lambda i, j: (0, j))],
        out_specs=pl.BlockSpec((tm, tn), lambda i, j: (i, j)),
        compiler_params=_cparams(("parallel", "arbitrary"), 48),
        name=name,
    )(x, w)


def _merge_kernel(a_ref, b_ref, wa_ref, wb_ref, ga_ref, gb_ref, o_ref):
    ya = _dot(a_ref[...], wa_ref[...])
    yb = _dot(b_ref[...], wb_ref[...])
    o_ref[...] = (_sigmoid(ga_ref[...].astype(F32)) * ya + _sigmoid(gb_ref[...].astype(F32)) * yb).astype(o_ref.dtype)


def _merge(o_gdn, o_nsa, w_a, w_b, proj, gate_col0, tm=1024, tn=512):
    n, ka = o_gdn.shape
    kb = o_nsa.shape[1]
    d = w_a.shape[1]
    ja, jb = gate_col0 // tn, (gate_col0 + d) // tn
    return pl.pallas_call(
        _merge_kernel,
        out_shape=jax.ShapeDtypeStruct((n, d), BF16),
        grid=(n // tm, d // tn),
        in_specs=[pl.BlockSpec((tm, ka), lambda i, j: (i, 0)),
                  pl.BlockSpec((tm, kb), lambda i, j: (i, 0)),
                  pl.BlockSpec((ka, tn), lambda i, j: (0, j)),
                  pl.BlockSpec((kb, tn), lambda i, j: (0, j)),
                  pl.BlockSpec((tm, tn), lambda i, j: (i, ja + j)),
                  pl.BlockSpec((tm, tn), lambda i, j: (i, jb + j))],
        out_specs=pl.BlockSpec((tm, tn), lambda i, j: (i, j)),
        compiler_params=_cparams(("parallel", "arbitrary"), 48),
        name="branch_merge",
    )(o_gdn, o_nsa, w_a, w_b, proj, proj)


def _resid_mm_kernel(x_ref, w_ref, r_ref, o_ref):
    o_ref[...] = r_ref[...] + _dot(x_ref[...], w_ref[...])


def _resid_matmul(x, w, resid, tm=1024, tn=512):
    m, k = x.shape
    n = w.shape[1]
    return pl.pallas_call(
        _resid_mm_kernel,
        out_shape=jax.ShapeDtypeStruct((m, n), F32),
        grid=(m // tm, n // tn),
        in_specs=[pl.BlockSpec((tm, k), lambda i, j: (i, 0)),
                  pl.BlockSpec((k, tn), lambda i, j: (0, j)),
                  pl.BlockSpec((tm, tn), lambda i, j: (i, j))],
        out_specs=pl.BlockSpec((tm, tn), lambda i, j: (i, j)),
        compiler_params=_cparams(("parallel", "arbitrary"), 48),
        name="out_proj_residual",
    )(x, w, resid)


def _gdn_gate_kernel(a_ref, b_ref, alog_ref, dtb_ref, gc_ref, beta_ref, gl_ref, *, chunk):
    tm = a_ref.shape[0]
    x = a_ref[...] + dtb_ref[...]
    softplus = jnp.maximum(x, 0.0) + jnp.log(1.0 + jnp.exp(-jnp.abs(x)))
    g = -jnp.exp(alog_ref[...]) * softplus
    row = lax.broadcasted_iota(jnp.int32, (tm, tm), 0)
    col = lax.broadcasted_iota(jnp.int32, (tm, tm), 1)
    same = row // chunk == col // chunk
    tri = jnp.where((col <= row) & same, 1.0, 0.0).astype(F32)
    gc_ref[...] = _dot(tri, g, precision=lax.Precision.HIGHEST)
    gl_ref[...] = _dot(jnp.where(same, 1.0, 0.0).astype(F32), g, precision=lax.Precision.HIGHEST)
    beta_ref[...] = _sigmoid(b_ref[...])


def _gdn_gates(a_raw, b_raw, a_log, dt_bias, chunk, tm=512):
    n, h = a_raw.shape
    spec = pl.BlockSpec((tm, h), lambda i: (i, 0))
    vec = pl.BlockSpec((1, h), lambda i: (0, 0))
    sds = jax.ShapeDtypeStruct((n, h), F32)
    return pl.pallas_call(
        functools.partial(_gdn_gate_kernel, chunk=chunk),
        out_shape=(sds, sds, sds),
        grid=(n // tm,),
        in_specs=[spec, spec, vec, vec],
        out_specs=(spec, spec, spec),
        compiler_params=_cparams(("parallel",), 32),
        name="gdn_gates",
    )(a_raw, b_raw, a_log.reshape(1, h).astype(F32), dt_bias.reshape(1, h).astype(F32))


_CONV_PAD = 8


def _gdn_kernel(q_ref, k_ref, v_ref, z_ref, cwq_ref, cwk_ref, cwv_ref, gc_ref, beta_ref, gl_ref, gct_ref, nw_ref,
                o_ref, state, xbuf, *, chunk, hps):
    hblk = pl.program_id(1)
    t = pl.program_id(2)
    tt = q_ref.shape[0]
    d = HEAD_DIM
    width = cwq_ref.shape[0]
    c = chunk
    n_ch = tt // c
    n_lvl = int(math.log2(c))
    assert 2 ** n_lvl == c

    @pl.when(t == 0)
    def _():
        state[...] = jnp.zeros_like(state)
        xbuf[:, 0:_CONV_PAD, :] = jnp.zeros((3 * hps, _CONV_PAD, d), F32)

    def conv_silu(s, x_bf16, w):
        xbuf[s, _CONV_PAD:_CONV_PAD + tt, :] = x_bf16.astype(F32)
        y = None
        for j in range(width):
            term = xbuf[s, pl.ds(_CONV_PAD - (width - 1) + j, tt), :] * w[j:j + 1, :]
            y = term if y is None else y + term
        xbuf[s, 0:_CONV_PAD, :] = xbuf[s, tt:tt + _CONV_PAD, :]
        return _silu(y)

    def l2n(x):
        return x * lax.rsqrt(jnp.sum(x * x, axis=-1, keepdims=True) + RMS_EPS)

    ri = lax.broadcasted_iota(jnp.int32, (tt, tt), 0)
    ci = lax.broadcasted_iota(jnp.int32, (tt, tt), 1)
    incl = (ri >= ci) & (ri // c == ci // c)
    strict = ri > ci
    lane = lax.broadcasted_iota(jnp.int32, gc_ref.shape, 1)
    nw = nw_ref[...]

    hd = [dict() for _ in range(hps)]
    for hp, e in enumerate(hd):
        h = hblk * hps + hp
        hs = slice(hp * d, (hp + 1) * d)
        q = l2n(conv_silu(3 * hp, q_ref[:, hs], cwq_ref[:, hs])) * (d ** -0.5)
        k = l2n(conv_silu(3 * hp + 1, k_ref[:, hs], cwk_ref[:, hs]))
        v = conv_silu(3 * hp + 2, v_ref[:, hs], cwv_ref[:, hs])
        pick = lambda ref: jnp.sum(jnp.where(lane == h, ref[...], 0.0), axis=1, keepdims=True)
        gcc, bc, glc = pick(gc_ref), pick(beta_ref), pick(gl_ref)
        egc = jnp.exp(gcc)
        kb = k.astype(BF16)
        qk = _dot_nt(jnp.concatenate([q.astype(BF16), kb], axis=0), kb)
        dm = jnp.exp(jnp.where(incl, gcc - gct_ref[hp], NEG_INF))
        e["npow"] = jnp.where(strict, -(qk[tt:] * dm * bc), 0.0)
        e["x"] = jnp.concatenate([v * bc, k * (bc * egc)], axis=1)
        e["attn"] = (qk[:tt] * dm).astype(BF16)
        e["qe"] = q * egc
        e["kout"] = (k * jnp.exp(glc - gcc)).astype(BF16)
        e["dch"] = jnp.exp(glc)

    for lvl in range(n_lvl):
        for e in hd:
            nb = e["npow"].astype(BF16)
            if lvl + 1 < n_lvl:
                r = _dot(nb, jnp.concatenate([nb, e["x"].astype(BF16)], axis=1))
                e["npow"], e["x"] = r[:, :tt], e["x"] + r[:, tt:]
            else:
                e["x"] = e["x"] + _dot(nb, e["x"].astype(BF16))

    for hp, e in enumerate(hd):
        e["xb"] = e["x"].astype(BF16)
        au = _dot(e["attn"], e["xb"])
        e["o0"] = au[:, :d]
        e["qe"] = (e["qe"] - au[:, d:]).astype(BF16)
        e["s"] = state[hp]

    for ch in range(n_ch):
        sl = slice(ch * c, (ch + 1) * c)
        for hp, e in enumerate(hd):
            hs = slice(hp * d, (hp + 1) * d)
            ktx = _dot_tn(e["kout"][sl], e["xb"][sl])
            sb = e["s"].astype(BF16)
            o = _dot(e["qe"][sl], sb) + e["o0"][sl]
            e["s"] = e["s"] * e["dch"][ch * c:ch * c + 1, :] - _dot(ktx[:, d:].astype(BF16), sb) + ktx[:, :d]
            on = o * lax.rsqrt(jnp.mean(o * o, axis=-1, keepdims=True) + RMS_EPS) * nw
            o_ref[sl, hs] = (on * _silu(z_ref[sl, hs].astype(F32))).astype(o_ref.dtype)

    for hp, e in enumerate(hd):
        state[hp] = e["s"]


def _gdn_mixer(proj, conv_w, gc, beta, gl, gct, norm_w, batch, seq, heads, chunk, tt=256, hps=4):
    d = HEAD_DIM
    n = batch * seq
    nt = seq // tt
    hb = heads // hps
    width = conv_w.shape[0]
    assert width - 1 <= _CONV_PAD and tt % chunk == 0 and seq % tt == 0 and heads % hps == 0

    def slab(off):
        return pl.BlockSpec((tt, hps * d), lambda b, h, t: (b * nt + t, off + h))

    def cw(off):
        return pl.BlockSpec((width, hps * d), lambda b, h, t: (0, off + h))

    gspec = pl.BlockSpec((tt, heads), lambda b, h, t: (b * nt + t, 0))
    return pl.pallas_call(
        functools.partial(_gdn_kernel, chunk=chunk, hps=hps),
        out_shape=jax.ShapeDtypeStruct((n, heads * d), BF16),
        grid=(batch, hb, nt),
        in_specs=[slab(0), slab(hb), slab(2 * hb), slab(3 * hb),
                  cw(0), cw(hb), cw(2 * hb),
                  gspec, gspec, gspec,
                  pl.BlockSpec((hps, 1, tt), lambda b, h, t: (b * hb + h, 0, t)),
                  pl.BlockSpec((1, d), lambda b, h, t: (0, 0))],
        out_specs=pl.BlockSpec((tt, hps * d), lambda b, h, t: (b * nt + t, h)),
        scratch_shapes=[pltpu.VMEM((hps, d, d), F32), pltpu.VMEM((3 * hps, tt + _CONV_PAD, d), F32)],
        compiler_params=_cparams(("parallel", "parallel", "arbitrary"), 40),
        name="gdn_mixer",
    )(proj, proj, proj, proj, conv_w, conv_w, conv_w, gc, beta, gl, gct, norm_w.reshape(1, d).astype(F32))


def _nsa_cmp_kernel(kc_ref, vc_ref, pek_ref, w1k_ref, w2k_ref, pev_ref, w1v_ref, w2v_ref,
                    ko_ref, vo_ref, xf, sh, *, stride, block):
    seq, d = kc_ref.shape
    nh = seq // stride
    reps = block // stride
    assert nh % 8 == 0

    def compress(src_ref, pe_ref, w1_ref, w2_ref, dst_ref):
        xf[...] = src_ref[...].astype(F32)
        pe = pe_ref[...]
        hid = None
        for r in range(reps):
            acc = None
            for l in range(stride):
                li = r * stride + l
                rows = xf[pl.ds(l, nh, stride=stride), :] + pe[li:li + 1, :]
                term = _dot(rows.astype(BF16), w1_ref[li].astype(BF16))
                acc = term if acc is None else acc + term
            if r == 0:
                hid = acc
            else:
                sh[0:nh, :] = acc
                sh[nh:nh + 8, :] = jnp.zeros((8, d), F32)
                hid = hid + sh[pl.ds(r, nh), :]
        hid = _silu(hid)
        dst_ref[0] = _dot(hid.astype(BF16), w2_ref[...].astype(BF16)).astype(dst_ref.dtype)

    compress(kc_ref, pek_ref, w1k_ref, w2k_ref, ko_ref)
    compress(vc_ref, pev_ref, w1v_ref, w2v_ref, vo_ref)


def _nsa_compress(proj, kc_col, vc_col, pe_k, w1_k, w2_k, pe_v, w1_v, w2_v, batch, seq, groups):
    d = HEAD_DIM
    nh = seq // CMP_STRIDE
    full2 = lambda b, g: (0, 0)
    full3 = lambda b, g: (0, 0, 0)
    out_sds = jax.ShapeDtypeStruct((batch * groups, nh, d), BF16)
    out_spec = pl.BlockSpec((1, nh, d), lambda b, g: (b * groups + g, 0, 0))
    return pl.pallas_call(
        functools.partial(_nsa_cmp_kernel, stride=CMP_STRIDE, block=CMP_BLOCK),
        out_shape=(out_sds, out_sds),
        grid=(batch, groups),
        in_specs=[pl.BlockSpec((seq, d), lambda b, g: (b, kc_col + g)),
                  pl.BlockSpec((seq, d), lambda b, g: (b, vc_col + g)),
                  pl.BlockSpec((CMP_BLOCK, d), full2), pl.BlockSpec((CMP_BLOCK, d, d), full3), pl.BlockSpec((d, d), full2),
                  pl.BlockSpec((CMP_BLOCK, d), full2), pl.BlockSpec((CMP_BLOCK, d, d), full3), pl.BlockSpec((d, d), full2)],
        out_specs=(out_spec, out_spec),
        scratch_shapes=[pltpu.VMEM((seq, d), F32), pltpu.VMEM((nh + 8, d), F32)],
        compiler_params=_cparams(("parallel", "parallel"), 32),
        name="nsa_compress",
    )(proj, proj, pe_k, w1_k, w2_k, pe_v, w1_v, w2_v)


def _nsa_kernel(q_ref, kc_ref, vc_ref, ks_ref, vs_ref, kw_ref, vw_ref, gate_ref, selmap_ref, expand_ref,
                o_ref, score_t, *, hpg, tk, n_sel, n_top):
    tq = q_ref.shape[0]
    d = HEAD_DIM
    n_cmp_pad = kc_ref.shape[1]
    t0 = pl.program_id(2) * tq
    rows = hpg * tq

    qb = q_ref[...]
    q4 = jnp.concatenate([qb[:, hh * d:(hh + 1) * d] for hh in range(hpg)], axis=0)
    q4 = (q4.astype(F32) * (d ** -0.5 * math.log2(math.e))).astype(BF16)
    trow = t0 + lax.broadcasted_iota(jnp.int32, (tq, 1), 0)

    ncol = lax.broadcasted_iota(jnp.int32, (1, n_cmp_pad), 1)
    vis_c = ((ncol * CMP_STRIDE + (CMP_BLOCK - 1)) <= trow)[None]
    s3 = jnp.where(vis_c, _dot_nt(q4, kc_ref[0]).reshape(hpg, tq, n_cmp_pad), NEG_INF)
    e = jnp.where(vis_c, jnp.exp2(s3 - jnp.max(s3, axis=-1, keepdims=True)), 0.0)
    den = jnp.sum(e, axis=-1, keepdims=True)
    p_c = e / jnp.where(den > 0.0, den, 1.0)
    o_cmp = _dot(p_c.reshape(rows, n_cmp_pad).astype(BF16), vc_ref[0])

    imp_t = lax.dot_general(selmap_ref[...], jnp.sum(p_c, axis=0), (((1,), (1,)), ((), ())),
                            preferred_element_type=F32, precision=lax.Precision.HIGHEST)
    jrow = lax.broadcasted_iota(jnp.int32, (n_sel, 1), 0)
    tcol = t0 + lax.broadcasted_iota(jnp.int32, (1, tq), 1)
    cur = tcol // SEL_BLOCK
    forced = (jrow == 0) | (jrow == cur) | (jrow == cur - 1)
    sc = jnp.where(forced, FORCE_SCORE, jnp.where(jrow * SEL_BLOCK <= tcol, imp_t, NEG_INF))
    bits = pltpu.bitcast(sc, jnp.int32)
    key = bits ^ ((bits >> 31) & jnp.int32(0x7FFFFFFF))
    score_t[...] = key

    def rank_step(i, cnt):
        ki = score_t[pl.ds(i, 1), :] + jnp.where(jrow > i, 1, 0)
        return cnt + jnp.where(ki > key, 1.0, 0.0)

    n_rank = jnp.minimum((t0 + tq - 1) // SEL_BLOCK + 1, n_sel)
    cnt = lax.fori_loop(0, n_rank, rank_step, jnp.zeros((n_sel, tq), F32))
    sel_bias_t = jnp.where(cnt < n_top, 0.0, NEG_INF).astype(BF16)
    eye = jnp.where(lax.broadcasted_iota(jnp.int32, (n_sel, n_sel), 0)
                    == lax.broadcasted_iota(jnp.int32, (n_sel, n_sel), 1), 1.0, 0.0).astype(BF16)
    sel_bias = _dot_tn(sel_bias_t, eye).astype(BF16)

    def sel_step(kt, carry, diagonal):
        m, l, acc = carry
        k0 = pl.multiple_of(kt * tk, tk)
        b = _dot(sel_bias, expand_ref[:, pl.ds(k0, tk)])
        if diagonal:
            kpos = k0 + lax.broadcasted_iota(jnp.int32, (1, tk), 1)
            b = jnp.where(kpos <= trow, b, NEG_INF)
        s3 = _dot_nt(q4, ks_ref[pl.ds(k0, tk), :]).reshape(hpg, tq, tk) + b[None]
        m_new = jnp.maximum(m, jnp.max(s3, axis=-1, keepdims=True))
        alpha = jnp.exp2(m - m_new)
        e = jnp.exp2(s3 - m_new)
        l = alpha * l + jnp.sum(e, axis=-1, keepdims=True)
        pv = _dot(e.reshape(rows, tk).astype(BF16), vs_ref[pl.ds(k0, tk), :])
        acc = alpha * acc + pv.reshape(hpg, tq, d)
        return m_new, l, acc

    n_full = t0 // tk
    carry = (jnp.full((hpg, tq, 1), NEG_INF, F32), jnp.zeros((hpg, tq, 1), F32), jnp.zeros((hpg, tq, d), F32))
    carry = lax.fori_loop(0, n_full, functools.partial(sel_step, diagonal=False), carry)
    _, l_s, acc_s = sel_step(n_full, carry, diagonal=True)
    o_sel = acc_s / l_s

    span = WINDOW + tq
    ws = pl.multiple_of(jnp.maximum(t0 - WINDOW, 0), tq)
    kpos = ws + lax.broadcasted_iota(jnp.int32, (1, span), 1)
    rel = trow - kpos
    b_w = jnp.where((rel >= 0) & (rel < WINDOW), 0.0, NEG_INF)
    s3 = _dot_nt(q4, kw_ref[pl.ds(ws, span), :]).reshape(hpg, tq, span) + b_w[None]
    e = jnp.exp2(s3 - jnp.max(s3, axis=-1, keepdims=True))
    o_win = _dot(e.reshape(rows, span).astype(BF16), vw_ref[pl.ds(ws, span), :])
    o_win3 = o_win.reshape(hpg, tq, d) / jnp.sum(e, axis=-1, keepdims=True)

    gates = _sigmoid(gate_ref[0])
    o_cmp3 = o_cmp.reshape(hpg, tq, d)
    for hh in range(hpg):
        g0 = gates[:, 3 * hh:3 * hh + 1]
        g1 = gates[:, 3 * hh + 1:3 * hh + 2]
        g2 = gates[:, 3 * hh + 2:3 * hh + 3]
        o_ref[:, hh * d:(hh + 1) * d] = (g0 * o_cmp3[hh] + g1 * o_sel[hh] + g2 * o_win3[hh]).astype(o_ref.dtype)


def _nsa_mixer(proj, q_col, ks_col, vs_col, kw_col, vw_col, k_cmp, v_cmp, gates, batch, seq, groups, hpg,
               tq=128, tk=512):
    d = HEAD_DIM
    n = batch * seq
    nq = seq // tq
    n_sel = seq // SEL_BLOCK
    n_top = min(SEL_TOP_N, n_sel)
    n_cmp = (seq - CMP_BLOCK) // CMP_STRIDE + 1
    n_cmp_pad = k_cmp.shape[1]
    assert n_sel % 8 == 0 and seq >= WINDOW + tq and WINDOW % tq == 0 and tk % tq == 0 and seq % tk == 0

    cmp_start = np.arange(n_cmp_pad) * CMP_STRIDE
    sel_start = np.arange(n_sel) * SEL_BLOCK
    overlap = (np.minimum(cmp_start[None, :] + CMP_BLOCK, sel_start[:, None] + SEL_BLOCK)
               - np.maximum(cmp_start[None, :], sel_start[:, None]))
    sel_map_t = np.clip(overlap, 0, None) / CMP_STRIDE
    sel_map_t[:, n_cmp:] = 0.0
    expand = (np.arange(seq)[None, :] // SEL_BLOCK == np.arange(n_sel)[:, None]).astype(np.float32)

    def kv(col):
        return pl.BlockSpec((seq, d), lambda b, g, t: (b, col + g))

    cmp_spec = pl.BlockSpec((1, n_cmp_pad, d), lambda b, g, t: (b * groups + g, 0, 0))
    return pl.pallas_call(
        functools.partial(_nsa_kernel, hpg=hpg, tk=tk, n_sel=n_sel, n_top=n_top),
        out_shape=jax.ShapeDtypeStruct((n, groups * hpg * d), BF16),
        grid=(batch, groups, nq),
        in_specs=[pl.BlockSpec((tq, hpg * d), lambda b, g, t: (b * nq + t, q_col // hpg + g)),
                  cmp_spec, cmp_spec, kv(ks_col), kv(vs_col), kv(kw_col), kv(vw_col),
                  pl.BlockSpec((1, tq, 3 * hpg), lambda b, g, t: (b * groups + g, t, 0)),
                  pl.BlockSpec((n_sel, n_cmp_pad), lambda b, g, t: (0, 0)),
                  pl.BlockSpec((n_sel, seq), lambda b, g, t: (0, 0))],
        out_specs=pl.BlockSpec((tq, hpg * d), lambda b, g, t: (b * nq + t, g)),
        scratch_shapes=[pltpu.VMEM((n_sel, tq), jnp.int32)],
        compiler_params=_cparams(("parallel", "parallel", "arbitrary"), 48),
        name="nsa_attention",
    )(proj, k_cmp, v_cmp, proj, proj, proj, proj, gates, jnp.asarray(sel_map_t, F32), jnp.asarray(expand, BF16))


def _router_kernel(h_ref, g_ref, wr_ref, br_ref, xn_ref, ids_ref, wts_ref):
    x = h_ref[...]
    xn = x * lax.rsqrt(jnp.mean(x * x, axis=-1, keepdims=True) + RMS_EPS) * g_ref[...]
    xn_ref[...] = xn
    logits = _dot(xn, wr_ref[...], precision=lax.Precision.HIGHEST) + br_ref[...]
    lane = lax.broadcasted_iota(jnp.int32, logits.shape, 1)
    big = jnp.int32(2 * LANES)
    is_g = lane < N_GROUPS
    gl = jnp.where(is_g, logits, NEG_INF)
    gm = jnp.max(gl, axis=-1, keepdims=True)
    p_g = 1.0 / jnp.sum(jnp.where(is_g, jnp.exp(gl - gm), 0.0), axis=-1, keepdims=True)
    grp = jnp.min(jnp.where(gl == gm, lane, big), axis=-1, keepdims=True)
    eidx = lane - N_GROUPS
    in_grp = (eidx >= 0) & (eidx // EXPERTS_PER_GROUP == grp) & (eidx < N_GROUPS * EXPERTS_PER_GROUP)
    el = jnp.where(in_grp, logits, NEG_INF)
    em = jnp.max(el, axis=-1, keepdims=True)
    ee = jnp.where(in_grp, jnp.exp(el - em), 0.0)
    pe = ee / jnp.sum(ee, axis=-1, keepdims=True)
    p1 = jnp.max(jnp.where(in_grp, pe, -1.0), axis=-1, keepdims=True)
    i1 = jnp.min(jnp.where(in_grp & (pe == p1), lane, big), axis=-1, keepdims=True)
    rest = in_grp & (lane != i1)
    p2 = jnp.max(jnp.where(rest, pe, -1.0), axis=-1, keepdims=True)
    i2 = jnp.min(jnp.where(rest & (pe == p2), lane, big), axis=-1, keepdims=True)
    denom = p1 + p2
    ids_ref[...] = jnp.where(lane == 0, i1 - N_GROUPS, i2 - N_GROUPS)
    wts_ref[...] = jnp.where(lane == 0, p_g * p1 / denom, p_g * p2 / denom)


def _router(h, g_ffn, w_group, b_group, w_expert, b_expert, tm=256):
    n, d = h.shape
    n_r = w_group.shape[1] + w_expert.shape[1]
    assert n_r <= LANES and TOP_K == 2
    wr = jnp.concatenate([w_group, w_expert, jnp.zeros((d, LANES - n_r), F32)], axis=1).astype(F32)
    br = jnp.concatenate([b_group, b_expert, jnp.zeros((LANES - n_r,), F32)]).reshape(1, LANES).astype(F32)
    row = lambda i: (i, 0)
    fixed = lambda i: (0, 0)
    xn, ids, wts = pl.pallas_call(
        _router_kernel,
        out_shape=(jax.ShapeDtypeStruct((n, d), F32), jax.ShapeDtypeStruct((n, LANES), jnp.int32),
                   jax.ShapeDtypeStruct((n, LANES), F32)),
        grid=(n // tm,),
        in_specs=[pl.BlockSpec((tm, d), row), pl.BlockSpec((1, d), fixed),
                  pl.BlockSpec((d, LANES), fixed), pl.BlockSpec((1, LANES), fixed)],
        out_specs=(pl.BlockSpec((tm, d), row), pl.BlockSpec((tm, LANES), row), pl.BlockSpec((tm, LANES), row)),
        compiler_params=_cparams(("parallel",), 40),
        name="moe_router",
    )(h, g_ffn.reshape(1, d).astype(F32), wr, br)
    return xn, ids[:, :TOP_K], wts[:, :TOP_K]


def _row_gather(idx_ref, base, src_hbm, dst, sem, n_rows, wait):
    for r in range(n_rows):
        cp = pltpu.make_async_copy(src_hbm.at[pl.ds(idx_ref[base + r], 1), :], dst.at[pl.ds(r, 1), :], sem)
        if wait:
            cp.wait()
        else:
            cp.start()


def _expert_kernel(be_ref, tok_ref, nact_ref, x_hbm, wgu_ref, wd_ref, wrow_ref, y_ref, xbuf, sem, *, blk, d_ff):
    i = pl.program_id(0)
    n_act = nact_ref[0]
    slot = i % 2

    @pl.when(i == 0)
    def _():
        _row_gather(tok_ref, 0, x_hbm, xbuf.at[0], sem.at[0], blk, wait=False)

    def run_block(prefetch_next):
        _row_gather(tok_ref, i * blk, x_hbm, xbuf.at[slot], sem.at[slot], blk, wait=True)
        if prefetch_next:
            _row_gather(tok_ref, (i + 1) * blk, x_hbm, xbuf.at[1 - slot], sem.at[1 - slot], blk, wait=False)
        xb = xbuf[slot].astype(BF16)
        gu = _dot(xb, wgu_ref[0])
        act = _silu(gu[:, :d_ff]) * gu[:, d_ff:]
        y_ref[...] = _dot(act.astype(BF16), wd_ref[0]) * wrow_ref[...]

    pl.when(i + 1 < n_act)(functools.partial(run_block, True))
    pl.when(i + 1 == n_act)(functools.partial(run_block, False))

    @pl.when(i >= n_act)
    def _():
        y_ref[...] = jnp.zeros_like(y_ref)


def _experts(xn, w_gate_up, w_down, blk_expert, tok_buf, n_active, w_buf, blk):
    n, d = xn.shape
    n_exp, _, ff2 = w_gate_up.shape
    d_ff = ff2 // 2
    n_blk = blk_expert.shape[0]
    return pl.pallas_call(
        functools.partial(_expert_kernel, blk=blk, d_ff=d_ff),
        out_shape=jax.ShapeDtypeStruct((n_blk * blk, d), F32),
        grid_spec=pltpu.PrefetchScalarGridSpec(
            num_scalar_prefetch=3,
            grid=(n_blk,),
            in_specs=[pl.BlockSpec(memory_space=pl.ANY),
                      pl.BlockSpec((1, d, ff2), lambda i, be, tok, na: (be[i], 0, 0)),
                      pl.BlockSpec((1, d_ff, d), lambda i, be, tok, na: (be[i], 0, 0)),
                      pl.BlockSpec((blk, 1), lambda i, be, tok, na: (i, 0))],
            out_specs=pl.BlockSpec((blk, d), lambda i, be, tok, na: (i, 0)),
            scratch_shapes=[pltpu.VMEM((2, blk, d), F32), pltpu.SemaphoreType.DMA((2,))]),
        compiler_params=_cparams(("arbitrary",), 56),
        name="moe_experts",
    )(blk_expert, tok_buf, n_active, xn, w_gate_up, w_down, w_buf.reshape(n_blk * blk, 1))


def _combine_kernel(pos_ref, h_ref, g_ref, y_hbm, o_ref, ybuf, sem, *, tc):
    i = pl.program_id(0)
    n = pl.num_programs(0)
    slot = i % 2
    rows = TOP_K * tc

    @pl.when(i == 0)
    def _():
        _row_gather(pos_ref, 0, y_hbm, ybuf.at[0], sem.at[0], rows, wait=False)

    def run_tile(prefetch_next):
        _row_gather(pos_ref, i * rows, y_hbm, ybuf.at[slot], sem.at[slot], rows, wait=True)
        if prefetch_next:
            _row_gather(pos_ref, (i + 1) * rows, y_hbm, ybuf.at[1 - slot], sem.at[1 - slot], rows, wait=False)
        hh = h_ref[...]
        for kk in range(TOP_K):
            hh = hh + ybuf[slot, kk * tc:(kk + 1) * tc, :]
        o_ref[...] = hh * lax.rsqrt(jnp.mean(hh * hh, axis=-1, keepdims=True) + RMS_EPS) * g_ref[...]

    pl.when(i + 1 < n)(functools.partial(run_tile, True))
    pl.when(i + 1 == n)(functools.partial(run_tile, False))


def _combine(h, y_buf, pos, g_final, tc=128):
    n, d = h.shape
    return pl.pallas_call(
        functools.partial(_combine_kernel, tc=tc),
        out_shape=jax.ShapeDtypeStruct((n, d), F32),
        grid_spec=pltpu.PrefetchScalarGridSpec(
            num_scalar_prefetch=1,
            grid=(n // tc,),
            in_specs=[pl.BlockSpec((tc, d), lambda i, pos: (i, 0)),
                      pl.BlockSpec((1, d), lambda i, pos: (0, 0)),
                      pl.BlockSpec(memory_space=pl.ANY)],
            out_specs=pl.BlockSpec((tc, d), lambda i, pos: (i, 0)),
            scratch_shapes=[pltpu.VMEM((2, TOP_K * tc, d), F32), pltpu.SemaphoreType.DMA((2,))]),
        compiler_params=_cparams(("arbitrary",), 40),
        name="moe_combine_final_norm",
    )(pos, h, g_final.reshape(1, d).astype(F32), y_buf)


def _dispatch_plan(expert, weight, n_experts, blk):
    n_tok, top_k = expert.shape
    n_slot = n_tok * top_k
    e_flat = expert.reshape(n_slot)
    order = jnp.argsort(e_flat)
    e_sorted = e_flat[order]
    counts = jnp.bincount(e_flat, length=n_experts)
    padded = (counts + blk - 1) // blk * blk
    pad_end = jnp.cumsum(padded)
    dest = ((pad_end - padded)[e_sorted] + jnp.arange(n_slot) - (jnp.cumsum(counts) - counts)[e_sorted]).astype(jnp.int32)
    n_blk = -(-n_slot // blk) + n_experts
    tok_buf = jnp.zeros((n_blk * blk,), jnp.int32).at[dest].set((order // top_k).astype(jnp.int32))
    w_buf = jnp.zeros((n_blk * blk,), F32).at[dest].set(weight.reshape(n_slot)[order])
    blk_expert = jnp.minimum(jnp.searchsorted(pad_end, jnp.arange(n_blk) * blk, side='right'),
                             n_experts - 1).astype(jnp.int32)
    n_active = (pad_end[-1] // blk).astype(jnp.int32).reshape(1)
    pos = jnp.zeros((n_slot,), jnp.int32).at[order].set(dest).reshape(n_tok, top_k)
    return tok_buf, w_buf, blk_expert, n_active, pos


def _layer(x, g_mix, w_in, conv_w, a_log, dt_bias, norm_w, pe_k, w1_k, w2_k, pe_v, w1_v, w2_v,
           w_a, w_b, w_out, g_ffn, w_group, b_group, w_expert, b_expert, w_gate_up, w_down, g_out):
    batch, seq, dm = x.shape
    n = batch * seq
    d = HEAD_DIM
    gdn_w = w_a.shape[0]
    nsa_w = w_b.shape[0]
    heads = gdn_w // d
    nsa_heads = nsa_w // d
    groups = NSA_GROUPS
    hpg = nsa_heads // groups
    kvw = groups * d

    sizes = (3 * gdn_w, gdn_w, heads, heads, nsa_w, 6 * kvw, 3 * nsa_heads, 2 * dm)
    assert sum(sizes) == w_in.shape[1]
    offs = np.concatenate([[0], np.cumsum(sizes)])
    seg = lambda i: w_in[:, offs[i]:offs[i + 1]]
    w_big = jnp.concatenate([seg(0), seg(1), seg(4), seg(5), seg(7)], axis=1).astype(BF16)
    w_small = jnp.concatenate([seg(2), seg(3), seg(6)], axis=1).astype(BF16)
    z_col = 3 * heads
    nq_col = z_col + heads
    kv_col = nq_col + nsa_heads
    gate_col0 = (kv_col + 6 * groups) * d

    x2 = x.reshape(n, dm)
    xn = _rmsnorm(x2, g_mix, BF16)
    proj = _matmul(xn, w_big, BF16, tm=1024, tn=512, name="in_proj")
    small = _matmul(xn, w_small, F32, tm=1024, tn=w_small.shape[1], name="in_proj_small")
    a_raw, b_raw, nsa_gate = small[:, :heads], small[:, heads:2 * heads], small[:, 2 * heads:]

    gc, beta, gl = _gdn_gates(a_raw, b_raw, a_log, dt_bias, GDN_CHUNK)
    gct = gc.reshape(batch, seq, heads).transpose(0, 2, 1).reshape(batch * heads, 1, seq)
    o_gdn = _gdn_mixer(proj, conv_w, gc, beta, gl, gct, norm_w, batch, seq, heads, GDN_CHUNK)

    k_cmp, v_cmp = _nsa_compress(proj, kv_col, kv_col + groups, pe_k, w1_k, w2_k, pe_v, w1_v, w2_v,
                                 batch, seq, groups)
    gates = nsa_gate.reshape(batch, seq, groups, 3 * hpg).transpose(0, 2, 1, 3).reshape(batch * groups, seq, 3 * hpg)
    o_nsa = _nsa_mixer(proj, nq_col, kv_col + 2 * groups, kv_col + 3 * groups, kv_col + 4 * groups,
                       kv_col + 5 * groups, k_cmp, v_cmp, gates, batch, seq, groups, hpg)

    merged = _merge(o_gdn, o_nsa, w_a.astype(BF16), w_b.astype(BF16), proj, gate_col0)
    h = _resid_matmul(merged, w_out.astype(BF16), x2)

    n_experts = w_gate_up.shape[0]
    xn2, expert, weight = _router(h, g_ffn, w_group, b_group, w_expert, b_expert)
    tok_buf, w_buf, blk_expert, n_active, pos = _dispatch_plan(expert, weight, n_experts, MOE_BLOCK)
    y_buf = _experts(xn2, w_gate_up.astype(BF16), w_down.astype(BF16), blk_expert, tok_buf, n_active, w_buf, MOE_BLOCK)
    tc = 128
    pos_tiles = pos.reshape(n // tc, tc, TOP_K).transpose(0, 2, 1).reshape(n * TOP_K)
    out = _combine(h, y_buf, pos_tiles, g_out, tc)
    return out.reshape(batch, seq, dm)


def kernel(x, g_mix, w_in, gdn_conv_w, gdn_a_log, gdn_dt_bias, gdn_norm_w, cmp_pe_k, cmp_w1_k, cmp_w2_k,
           cmp_pe_v, cmp_w1_v, cmp_w2_v, w_branch_gdn, w_branch_nsa, w_out, g_ffn, w_group, b_group,
           w_expert, b_expert, w_gate_up, w_down, g_final):
    depth = g_mix.shape[0]
    assert depth == 1, "the fused final norm assumes a single layer"
    return _layer(x, g_mix[0], w_in[0], gdn_conv_w[0], gdn_a_log[0], gdn_dt_bias[0], gdn_norm_w[0],
                  cmp_pe_k[0], cmp_w1_k[0], cmp_w2_k[0], cmp_pe_v[0], cmp_w1_v[0], cmp_w2_v[0],
                  w_branch_gdn[0], w_branch_nsa[0], w_out[0], g_ffn[0], w_group[0], b_group[0],
                  w_expert[0], b_expert[0], w_gate_up[0], w_down[0], g_final)
```

```python
import functools
import math

import numpy as np
import jax
import jax.numpy as jnp
from jax import lax
from jax.experimental import pallas as pl
from jax.experimental.pallas import tpu as pltpu

F32 = jnp.float32
BF16 = jnp.bfloat16

RMS_EPS = 1e-6
NEG_INF = -1e30
FORCE_SCORE = 1e6

HEAD_DIM = 128
GDN_CHUNK = 64
NSA_GROUPS = 4
CMP_BLOCK = 32
CMP_STRIDE = 16
SEL_BLOCK = 64
SEL_TOP_N = 16
WINDOW = 512
N_GROUPS = 8
EXPERTS_PER_GROUP = 8
TOP_K = 2
MOE_BLOCK = 128
LANES = 128

V7X_VMEM_BYTES = 64 * 1024 * 1024


def _cparams(semantics, vmem_mb):
    assert vmem_mb * 1024 * 1024 < V7X_VMEM_BYTES
    return pltpu.CompilerParams(dimension_semantics=semantics, vmem_limit_bytes=vmem_mb * 1024 * 1024)


def _dot(a, b, **kw):
    return jnp.dot(a, b, preferred_element_type=F32, **kw)


def _dot_nt(a, b):
    return lax.dot_general(a, b, (((1,), (1,)), ((), ())), preferred_element_type=F32)


def _dot_tn(a, b):
    return lax.dot_general(a, b, (((0,), (0,)), ((), ())), preferred_element_type=F32)


def _sigmoid(x):
    return 1.0 / (1.0 + jnp.exp(-x))


def _silu(x):
    return x * _sigmoid(x)


def _rmsnorm_kernel(x_ref, g_ref, o_ref):
    x = x_ref[...].astype(F32)
    ms = jnp.mean(x * x, axis=-1, keepdims=True)
    o_ref[...] = (x * lax.rsqrt(ms + RMS_EPS) * g_ref[...]).astype(o_ref.dtype)


def _rmsnorm(x, gain, out_dtype, tm=512):
    n, d = x.shape
    return pl.pallas_call(
        _rmsnorm_kernel,
        out_shape=jax.ShapeDtypeStruct((n, d), out_dtype),
        grid=(n // tm,),
        in_specs=[pl.BlockSpec((tm, d), lambda i: (i, 0)), pl.BlockSpec((1, d), lambda i: (0, 0))],
        out_specs=pl.BlockSpec((tm, d), lambda i: (i, 0)),
        compiler_params=_cparams(("parallel",), 40),
        name="rmsnorm",
    )(x, gain.reshape(1, d).astype(F32))


def _mm_kernel(x_ref, w_ref, o_ref):
    o_ref[...] = _dot(x_ref[...], w_ref[...]).astype(o_ref.dtype)


def _matmul(x, w, out_dtype, tm, tn, name):
    m, k = x.shape
    n = w.shape[1]
    return pl.pallas_call(
        _mm_kernel,
        out_shape=jax.ShapeDtypeStruct((m, n), out_dtype),
        grid=(m // tm, n // tn),
        in_specs=[pl.BlockSpec((tm, k), lambda i, j: (i, 0)), pl.BlockSpec((k, tn), lambda i, j: (0, j))],
        out_specs=pl.BlockSpec((tm, tn), lambda i, j: (i, j)),
        compiler_params=_cparams(("parallel", "arbitrary"), 48),
        name=name,
    )(x, w)


def _merge_kernel(a_ref, b_ref, wa_ref, wb_ref, ga_ref, gb_ref, o_ref):
    ya = _dot(a_ref[...], wa_ref[...])
    yb = _dot(b_ref[...], wb_ref[...])
    o_ref[...] = (_sigmoid(ga_ref[...].astype(F32)) * ya + _sigmoid(gb_ref[...].astype(F32)) * yb).astype(o_ref.dtype)


def _merge(o_gdn, o_nsa, w_a, w_b, proj, gate_col0, tm=1024, tn=512):
    n, ka = o_gdn.shape
    kb = o_nsa.shape[1]
    d = w_a.shape[1]
    ja, jb = gate_col0 // tn, (gate_col0 + d) // tn
    return pl.pallas_call(
        _merge_kernel,
        out_shape=jax.ShapeDtypeStruct((n, d), BF16),
        grid=(n // tm, d // tn),
        in_specs=[pl.BlockSpec((tm, ka), lambda i, j: (i, 0)),
                  pl.BlockSpec((tm, kb), lambda i, j: (i, 0)),
                  pl.BlockSpec((ka, tn), lambda i, j: (0, j)),
                  pl.BlockSpec((kb, tn), lambda i, j: (0, j)),
                  pl.BlockSpec((tm, tn), lambda i, j: (i, ja + j)),
                  pl.BlockSpec((tm, tn), lambda i, j: (i, jb + j))],
        out_specs=pl.BlockSpec((tm, tn), lambda i, j: (i, j)),
        compiler_params=_cparams(("parallel", "arbitrary"), 48),
        name="branch_merge",
    )(o_gdn, o_nsa, w_a, w_b, proj, proj)


def _resid_mm_kernel(x_ref, w_ref, r_ref, o_ref):
    o_ref[...] = r_ref[...] + _dot(x_ref[...], w_ref[...])


def _resid_matmul(x, w, resid, tm=1024, tn=512):
    m, k = x.shape
    n = w.shape[1]
    return pl.pallas_call(
        _resid_mm_kernel,
        out_shape=jax.ShapeDtypeStruct((m, n), F32),
        grid=(m // tm, n // tn),
        in_specs=[pl.BlockSpec((tm, k), lambda i, j: (i, 0)),
                  pl.BlockSpec((k, tn), lambda i, j: (0, j)),
                  pl.BlockSpec((tm, tn), lambda i, j: (i, j))],
        out_specs=pl.BlockSpec((tm, tn), lambda i, j: (i, j)),
        compiler_params=_cparams(("parallel", "arbitrary"), 48),
        name="out_proj_residual",
    )(x, w, resid)


def _gdn_gate_kernel(a_ref, b_ref, alog_ref, dtb_ref, gc_ref, beta_ref, gl_ref, *, chunk):
    tm = a_ref.shape[0]
    x = a_ref[...] + dtb_ref[...]
    softplus = jnp.maximum(x, 0.0) + jnp.log(1.0 + jnp.exp(-jnp.abs(x)))
    g = -jnp.exp(alog_ref[...]) * softplus
    row = lax.broadcasted_iota(jnp.int32, (tm, tm), 0)
    col = lax.broadcasted_iota(jnp.int32, (tm, tm), 1)
    same = row // chunk == col // chunk
    tri = jnp.where((col <= row) & same, 1.0, 0.0).astype(F32)
    gc_ref[...] = _dot(tri, g, precision=lax.Precision.HIGHEST)
    gl_ref[...] = _dot(jnp.where(same, 1.0, 0.0).astype(F32), g, precision=lax.Precision.HIGHEST)
    beta_ref[...] = _sigmoid(b_ref[...])


def _gdn_gates(a_raw, b_raw, a_log, dt_bias, chunk, tm=512):
    n, h = a_raw.shape
    spec = pl.BlockSpec((tm, h), lambda i: (i, 0))
    vec = pl.BlockSpec((1, h), lambda i: (0, 0))
    sds = jax.ShapeDtypeStruct((n, h), F32)
    return pl.pallas_call(
        functools.partial(_gdn_gate_kernel, chunk=chunk),
        out_shape=(sds, sds, sds),
        grid=(n // tm,),
        in_specs=[spec, spec, vec, vec],
        out_specs=(spec, spec, spec),
        compiler_params=_cparams(("parallel",), 32),
        name="gdn_gates",
    )(a_raw, b_raw, a_log.reshape(1, h).astype(F32), dt_bias.reshape(1, h).astype(F32))


_CONV_PAD = 8


def _gdn_kernel(q_ref, k_ref, v_ref, z_ref, cwq_ref, cwk_ref, cwv_ref, gc_ref, beta_ref, gl_ref, gct_ref, nw_ref,
                o_ref, state, xbuf, *, chunk, hps):
    hblk = pl.program_id(1)
    t = pl.program_id(2)
    tt = q_ref.shape[0]
    d = HEAD_DIM
    width = cwq_ref.shape[0]
    c = chunk
    n_ch = tt // c
    n_lvl = int(math.log2(c))
    assert 2 ** n_lvl == c

    @pl.when(t == 0)
    def _():
        state[...] = jnp.zeros_like(state)
        xbuf[:, 0:_CONV_PAD, :] = jnp.zeros((3 * hps, _CONV_PAD, d), F32)

    def conv_silu(s, x_bf16, w):
        xbuf[s, _CONV_PAD:_CONV_PAD + tt, :] = x_bf16.astype(F32)
        y = None
        for j in range(width):
            term = xbuf[s, pl.ds(_CONV_PAD - (width - 1) + j, tt), :] * w[j:j + 1, :]
            y = term if y is None else y + term
        xbuf[s, 0:_CONV_PAD, :] = xbuf[s, tt:tt + _CONV_PAD, :]
        return _silu(y)

    def l2n(x):
        return x * lax.rsqrt(jnp.sum(x * x, axis=-1, keepdims=True) + RMS_EPS)

    ri = lax.broadcasted_iota(jnp.int32, (tt, tt), 0)
    ci = lax.broadcasted_iota(jnp.int32, (tt, tt), 1)
    incl = (ri >= ci) & (ri // c == ci // c)
    strict = ri > ci
    lane = lax.broadcasted_iota(jnp.int32, gc_ref.shape, 1)
    nw = nw_ref[...]

    hd = [dict() for _ in range(hps)]
    for hp, e in enumerate(hd):
        h = hblk * hps + hp
        hs = slice(hp * d, (hp + 1) * d)
        q = l2n(conv_silu(3 * hp, q_ref[:, hs], cwq_ref[:, hs])) * (d ** -0.5)
        k = l2n(conv_silu(3 * hp + 1, k_ref[:, hs], cwk_ref[:, hs]))
        v = conv_silu(3 * hp + 2, v_ref[:, hs], cwv_ref[:, hs])
        pick = lambda ref: jnp.sum(jnp.where(lane == h, ref[...], 0.0), axis=1, keepdims=True)
        gcc, bc, glc = pick(gc_ref), pick(beta_ref), pick(gl_ref)
        egc = jnp.exp(gcc)
        kb = k.astype(BF16)
        qk = _dot_nt(jnp.concatenate([q.astype(BF16), kb], axis=0), kb)
        dm = jnp.exp(jnp.where(incl, gcc - gct_ref[hp], NEG_INF))
        e["npow"] = jnp.where(strict, -(qk[tt:] * dm * bc), 0.0)
        e["x"] = jnp.concatenate([v * bc, k * (bc * egc)], axis=1)
        e["attn"] = (qk[:tt] * dm).astype(BF16)
        e["qe"] = q * egc
        e["kout"] = (k * jnp.exp(glc - gcc)).astype(BF16)
        e["dch"] = jnp.exp(glc)

    for lvl in range(n_lvl):
        for e in hd:
            nb = e["npow"].astype(BF16)
            if lvl + 1 < n_lvl:
                r = _dot(nb, jnp.concatenate([nb, e["x"].astype(BF16)], axis=1))
                e["npow"], e["x"] = r[:, :tt], e["x"] + r[:, tt:]
            else:
                e["x"] = e["x"] + _dot(nb, e["x"].astype(BF16))

    for hp, e in enumerate(hd):
        e["xb"] = e["x"].astype(BF16)
        au = _dot(e["attn"], e["xb"])
        e["o0"] = au[:, :d]
        e["qe"] = (e["qe"] - au[:, d:]).astype(BF16)
        e["s"] = state[hp]

    for ch in range(n_ch):
        sl = slice(ch * c, (ch + 1) * c)
        for hp, e in enumerate(hd):
            hs = slice(hp * d, (hp + 1) * d)
            ktx = _dot_tn(e["kout"][sl], e["xb"][sl])
            sb = e["s"].astype(BF16)
            o = _dot(e["qe"][sl], sb) + e["o0"][sl]
            e["s"] = e["s"] * e["dch"][ch * c:ch * c + 1, :] - _dot(ktx[:, d:].astype(BF16), sb) + ktx[:, :d]
            on = o * lax.rsqrt(jnp.mean(o * o, axis=-1, keepdims=True) + RMS_EPS) * nw
            o_ref[sl, hs] = (on * _silu(z_ref[sl, hs].astype(F32))).astype(o_ref.dtype)

    for hp, e in enumerate(hd):
        state[hp] = e["s"]


def _gdn_mixer(proj, conv_w, gc, beta, gl, gct, norm_w, batch, seq, heads, chunk, tt=256, hps=4):
    d = HEAD_DIM
    n = batch * seq
    nt = seq // tt
    hb = heads // hps
    width = conv_w.shape[0]
    assert width - 1 <= _CONV_PAD and tt % chunk == 0 and seq % tt == 0 and heads % hps == 0

    def slab(off):
        return pl.BlockSpec((tt, hps * d), lambda b, h, t: (b * nt + t, off + h))

    def cw(off):
        return pl.BlockSpec((width, hps * d), lambda b, h, t: (0, off + h))

    gspec = pl.BlockSpec((tt, heads), lambda b, h, t: (b * nt + t, 0))
    return pl.pallas_call(
        functools.partial(_gdn_kernel, chunk=chunk, hps=hps),
        out_shape=jax.ShapeDtypeStruct((n, heads * d), BF16),
        grid=(batch, hb, nt),
        in_specs=[slab(0), slab(hb), slab(2 * hb), slab(3 * hb),
                  cw(0), cw(hb), cw(2 * hb),
                  gspec, gspec, gspec,
                  pl.BlockSpec((hps, 1, tt), lambda b, h, t: (b * hb + h, 0, t)),
                  pl.BlockSpec((1, d), lambda b, h, t: (0, 0))],
        out_specs=pl.BlockSpec((tt, hps * d), lambda b, h, t: (b * nt + t, h)),
        scratch_shapes=[pltpu.VMEM((hps, d, d), F32), pltpu.VMEM((3 * hps, tt + _CONV_PAD, d), F32)],
        compiler_params=_cparams(("parallel", "parallel", "arbitrary"), 40),
        name="gdn_mixer",
    )(proj, proj, proj, proj, conv_w, conv_w, conv_w, gc, beta, gl, gct, norm_w.reshape(1, d).astype(F32))


def _nsa_cmp_kernel(kc_ref, vc_ref, pek_ref, w1k_ref, w2k_ref, pev_ref, w1v_ref, w2v_ref,
                    ko_ref, vo_ref, xf, sh, *, stride, block):
    seq, d = kc_ref.shape
    nh = seq // stride
    reps = block // stride
    assert nh % 8 == 0

    def compress(src_ref, pe_ref, w1_ref, w2_ref, dst_ref):
        xf[...] = src_ref[...].astype(F32)
        pe = pe_ref[...]
        hid = None
        for r in range(reps):
            acc = None
            for l in range(stride):
                li = r * stride + l
                rows = xf[pl.ds(l, nh, stride=stride), :] + pe[li:li + 1, :]
                term = _dot(rows.astype(BF16), w1_ref[li].astype(BF16))
                acc = term if acc is None else acc + term
            if r == 0:
                hid = acc
            else:
                sh[0:nh, :] = acc
                sh[nh:nh + 8, :] = jnp.zeros((8, d), F32)
                hid = hid + sh[pl.ds(r, nh), :]
        hid = _silu(hid)
        dst_ref[0] = _dot(hid.astype(BF16), w2_ref[...].astype(BF16)).astype(dst_ref.dtype)

    compress(kc_ref, pek_ref, w1k_ref, w2k_ref, ko_ref)
    compress(vc_ref, pev_ref, w1v_ref, w2v_ref, vo_ref)


def _nsa_compress(proj, kc_col, vc_col, pe_k, w1_k, w2_k, pe_v, w1_v, w2_v, batch, seq, groups):
    d = HEAD_DIM
    nh = seq // CMP_STRIDE
    full2 = lambda b, g: (0, 0)
    full3 = lambda b, g: (0, 0, 0)
    out_sds = jax.ShapeDtypeStruct((batch * groups, nh, d), BF16)
    out_spec = pl.BlockSpec((1, nh, d), lambda b, g: (b * groups + g, 0, 0))
    return pl.pallas_call(
        functools.partial(_nsa_cmp_kernel, stride=CMP_STRIDE, block=CMP_BLOCK),
        out_shape=(out_sds, out_sds),
        grid=(batch, groups),
        in_specs=[pl.BlockSpec((seq, d), lambda b, g: (b, kc_col + g)),
                  pl.BlockSpec((seq, d), lambda b, g: (b, vc_col + g)),
                  pl.BlockSpec((CMP_BLOCK, d), full2), pl.BlockSpec((CMP_BLOCK, d, d), full3), pl.BlockSpec((d, d), full2),
                  pl.BlockSpec((CMP_BLOCK, d), full2), pl.BlockSpec((CMP_BLOCK, d, d), full3), pl.BlockSpec((d, d), full2)],
        out_specs=(out_spec, out_spec),
        scratch_shapes=[pltpu.VMEM((seq, d), F32), pltpu.VMEM((nh + 8, d), F32)],
        compiler_params=_cparams(("parallel", "parallel"), 32),
        name="nsa_compress",
    )(proj, proj, pe_k, w1_k, w2_k, pe_v, w1_v, w2_v)


def _nsa_kernel(q_ref, kc_ref, vc_ref, ks_ref, vs_ref, kw_ref, vw_ref, gate_ref, selmap_ref, expand_ref,
                o_ref, score_t, *, hpg, gps, tk, n_sel, n_top):
    tq = q_ref.shape[0]
    d = HEAD_DIM
    gw = hpg * d
    n_cmp_pad = kc_ref.shape[1]
    t0 = pl.program_id(2) * tq
    rows = hpg * tq
    grp = list(range(gps))
    kv = lambda ref, gi, r0, n: ref[pl.ds(r0, n), gi * d:(gi + 1) * d]

    q4 = []
    for gi in grp:
        qg = jnp.concatenate([q_ref[:, gi * gw + hh * d:gi * gw + (hh + 1) * d] for hh in range(hpg)], axis=0)
        q4.append((qg.astype(F32) * (d ** -0.5 * math.log2(math.e))).astype(BF16))
    trow = t0 + lax.broadcasted_iota(jnp.int32, (tq, 1), 0)

    ncol = lax.broadcasted_iota(jnp.int32, (1, n_cmp_pad), 1)
    vis_c = ((ncol * CMP_STRIDE + (CMP_BLOCK - 1)) <= trow)[None]
    s_c = [_dot_nt(q4[gi], kc_ref[gi]).reshape(hpg, tq, n_cmp_pad) for gi in grp]
    p_c = []
    for gi in grp:
        s3 = jnp.where(vis_c, s_c[gi], NEG_INF)
        e = jnp.where(vis_c, jnp.exp2(s3 - jnp.max(s3, axis=-1, keepdims=True)), 0.0)
        den = jnp.sum(e, axis=-1, keepdims=True)
        p_c.append(e / jnp.where(den > 0.0, den, 1.0))
    o_cmp = [_dot(p_c[gi].reshape(rows, n_cmp_pad).astype(BF16), vc_ref[gi]).reshape(hpg, tq, d) for gi in grp]

    jrow = lax.broadcasted_iota(jnp.int32, (n_sel, 1), 0)
    tcol = t0 + lax.broadcasted_iota(jnp.int32, (1, tq), 1)
    cur = tcol // SEL_BLOCK
    forced = (jrow == 0) | (jrow == cur) | (jrow == cur - 1)
    causal_blk = jrow * SEL_BLOCK <= tcol
    keys = []
    for gi in grp:
        imp_t = lax.dot_general(selmap_ref[...], jnp.sum(p_c[gi], axis=0), (((1,), (1,)), ((), ())),
                                preferred_element_type=F32, precision=lax.Precision.HIGHEST)
        sc = jnp.where(forced, FORCE_SCORE, jnp.where(causal_blk, imp_t, NEG_INF))
        bits = pltpu.bitcast(sc, jnp.int32)
        keys.append(bits ^ ((bits >> 31) & jnp.int32(0x7FFFFFFF)))
        score_t[gi] = keys[gi]

    def rank_step(i, cnts):
        tie = jnp.where(jrow > i, 1, 0)
        return tuple(cnts[gi] + jnp.where(score_t[gi, pl.ds(i, 1), :] + tie > keys[gi], 1.0, 0.0) for gi in grp)

    n_rank = jnp.minimum((t0 + tq - 1) // SEL_BLOCK + 1, n_sel)
    cnts = lax.fori_loop(0, n_rank, rank_step, tuple(jnp.zeros((n_sel, tq), F32) for _ in grp))
    eye = jnp.where(lax.broadcasted_iota(jnp.int32, (n_sel, n_sel), 0)
                    == lax.broadcasted_iota(jnp.int32, (n_sel, n_sel), 1), 1.0, 0.0).astype(BF16)
    sel_bias = [_dot_tn(jnp.where(cnts[gi] < n_top, 0.0, NEG_INF).astype(BF16), eye).astype(BF16) for gi in grp]

    def sel_step(kt, carry, diagonal):
        k0 = pl.multiple_of(kt * tk, tk)
        ex = expand_ref[:, pl.ds(k0, tk)]
        s3 = []
        for gi in grp:
            b = _dot(sel_bias[gi], ex)
            if diagonal:
                kpos = k0 + lax.broadcasted_iota(jnp.int32, (1, tk), 1)
                b = jnp.where(kpos <= trow, b, NEG_INF)
            s3.append(_dot_nt(q4[gi], kv(ks_ref, gi, k0, tk)).reshape(hpg, tq, tk) + b[None])
        out = []
        for gi in grp:
            m, l, acc = carry[gi]
            m_new = jnp.maximum(m, jnp.max(s3[gi], axis=-1, keepdims=True))
            alpha = jnp.exp2(m - m_new)
            e = jnp.exp2(s3[gi] - m_new)
            l = alpha * l + jnp.sum(e, axis=-1, keepdims=True)
            pv = _dot(e.reshape(rows, tk).astype(BF16), kv(vs_ref, gi, k0, tk))
            out.append((m_new, l, alpha * acc + pv.reshape(hpg, tq, d)))
        return tuple(out)

    n_full = t0 // tk
    init = (jnp.full((hpg, tq, 1), NEG_INF, F32), jnp.zeros((hpg, tq, 1), F32), jnp.zeros((hpg, tq, d), F32))
    carry = lax.fori_loop(0, n_full, functools.partial(sel_step, diagonal=False), tuple(init for _ in grp))
    carry = sel_step(n_full, carry, diagonal=True)
    o_sel = [carry[gi][2] / carry[gi][1] for gi in grp]

    span = WINDOW + tq
    ws = pl.multiple_of(jnp.maximum(t0 - WINDOW, 0), tq)
    kpos = ws + lax.broadcasted_iota(jnp.int32, (1, span), 1)
    rel = trow - kpos
    b_w = jnp.where((rel >= 0) & (rel < WINDOW), 0.0, NEG_INF)[None]
    s_w = [_dot_nt(q4[gi], kv(kw_ref, gi, ws, span)).reshape(hpg, tq, span) + b_w for gi in grp]
    o_win = []
    for gi in grp:
        e = jnp.exp2(s_w[gi] - jnp.max(s_w[gi], axis=-1, keepdims=True))
        pv = _dot(e.reshape(rows, span).astype(BF16), kv(vw_ref, gi, ws, span))
        o_win.append(pv.reshape(hpg, tq, d) / jnp.sum(e, axis=-1, keepdims=True))

    for gi in grp:
        gates = _sigmoid(gate_ref[gi])
        for hh in range(hpg):
            g0 = gates[:, 3 * hh:3 * hh + 1]
            g1 = gates[:, 3 * hh + 1:3 * hh + 2]
            g2 = gates[:, 3 * hh + 2:3 * hh + 3]
            o_ref[:, gi * gw + hh * d:gi * gw + (hh + 1) * d] = (
                g0 * o_cmp[gi][hh] + g1 * o_sel[gi][hh] + g2 * o_win[gi][hh]).astype(o_ref.dtype)


def _nsa_mixer(proj, q_col, ks_col, vs_col, kw_col, vw_col, k_cmp, v_cmp, gates, batch, seq, groups, hpg,
               tq=128, tk=512, gps=2):
    d = HEAD_DIM
    n = batch * seq
    nq = seq // tq
    n_sel = seq // SEL_BLOCK
    n_top = min(SEL_TOP_N, n_sel)
    n_cmp = (seq - CMP_BLOCK) // CMP_STRIDE + 1
    n_cmp_pad = k_cmp.shape[1]
    assert n_sel % 8 == 0 and seq >= WINDOW + tq and WINDOW % tq == 0 and tk % tq == 0 and seq % tk == 0

    cmp_start = np.arange(n_cmp_pad) * CMP_STRIDE
    sel_start = np.arange(n_sel) * SEL_BLOCK
    overlap = (np.minimum(cmp_start[None, :] + CMP_BLOCK, sel_start[:, None] + SEL_BLOCK)
               - np.maximum(cmp_start[None, :], sel_start[:, None]))
    sel_map_t = np.clip(overlap, 0, None) / CMP_STRIDE
    sel_map_t[:, n_cmp:] = 0.0
    expand = (np.arange(seq)[None, :] // SEL_BLOCK == np.arange(n_sel)[:, None]).astype(np.float32)

    gb = groups // gps
    qw = gps * hpg * d
    assert groups % gps == 0 and all(c % gps == 0 for c in (ks_col, vs_col, kw_col, vw_col)) and q_col % (gps * hpg) == 0

    def kv(col):
        return pl.BlockSpec((seq, gps * d), lambda b, g, t: (b, col // gps + g))

    cmp_spec = pl.BlockSpec((gps, n_cmp_pad, d), lambda b, g, t: (b * gb + g, 0, 0))
    return pl.pallas_call(
        functools.partial(_nsa_kernel, hpg=hpg, gps=gps, tk=tk, n_sel=n_sel, n_top=n_top),
        out_shape=jax.ShapeDtypeStruct((n, groups * hpg * d), BF16),
        grid=(batch, gb, nq),
        in_specs=[pl.BlockSpec((tq, qw), lambda b, g, t: (b * nq + t, q_col // (gps * hpg) + g)),
                  cmp_spec, cmp_spec, kv(ks_col), kv(vs_col), kv(kw_col), kv(vw_col),
                  pl.BlockSpec((gps, tq, 3 * hpg), lambda b, g, t: (b * gb + g, t, 0)),
                  pl.BlockSpec((n_sel, n_cmp_pad), lambda b, g, t: (0, 0)),
                  pl.BlockSpec((n_sel, seq), lambda b, g, t: (0, 0))],
        out_specs=pl.BlockSpec((tq, qw), lambda b, g, t: (b * nq + t, g)),
        scratch_shapes=[pltpu.VMEM((gps, n_sel, tq), jnp.int32)],
        compiler_params=_cparams(("parallel", "parallel", "arbitrary"), 48),
        name="nsa_attention",
    )(proj, k_cmp, v_cmp, proj, proj, proj, proj, gates, jnp.asarray(sel_map_t, F32), jnp.asarray(expand, BF16))


def _router_kernel(h_ref, g_ref, wr_ref, br_ref, xn_ref, ids_ref, wts_ref, cnt_ref):
    x = h_ref[...]
    xn = x * lax.rsqrt(jnp.mean(x * x, axis=-1, keepdims=True) + RMS_EPS) * g_ref[...]
    xn_ref[...] = xn
    logits = _dot(xn, wr_ref[...], precision=lax.Precision.HIGHEST) + br_ref[...]
    lane = lax.broadcasted_iota(jnp.int32, logits.shape, 1)
    big = jnp.int32(2 * LANES)
    is_g = lane < N_GROUPS
    gl = jnp.where(is_g, logits, NEG_INF)
    gm = jnp.max(gl, axis=-1, keepdims=True)
    p_g = 1.0 / jnp.sum(jnp.where(is_g, jnp.exp(gl - gm), 0.0), axis=-1, keepdims=True)
    grp = jnp.min(jnp.where(gl == gm, lane, big), axis=-1, keepdims=True)
    eidx = lane - N_GROUPS
    in_grp = (eidx >= 0) & (eidx // EXPERTS_PER_GROUP == grp) & (eidx < N_GROUPS * EXPERTS_PER_GROUP)
    el = jnp.where(in_grp, logits, NEG_INF)
    em = jnp.max(el, axis=-1, keepdims=True)
    ee = jnp.where(in_grp, jnp.exp(el - em), 0.0)
    pe = ee / jnp.sum(ee, axis=-1, keepdims=True)
    p1 = jnp.max(jnp.where(in_grp, pe, -1.0), axis=-1, keepdims=True)
    i1 = jnp.min(jnp.where(in_grp & (pe == p1), lane, big), axis=-1, keepdims=True)
    rest = in_grp & (lane != i1)
    p2 = jnp.max(jnp.where(rest, pe, -1.0), axis=-1, keepdims=True)
    i2 = jnp.min(jnp.where(rest & (pe == p2), lane, big), axis=-1, keepdims=True)
    denom = p1 + p2
    wts_ref[...] = jnp.where(lane == 0, p_g * p1 / denom, p_g * p2 / denom)

    @pl.when(pl.program_id(0) == 0)
    def _():
        cnt_ref[...] = jnp.zeros_like(cnt_ref)

    tm = logits.shape[0]
    hot = jnp.where((lane == i1) | (lane == i2), 1.0, 0.0)
    below = jnp.where(lax.broadcasted_iota(jnp.int32, (tm, tm), 0) > lax.broadcasted_iota(jnp.int32, (tm, tm), 1),
                      1.0, 0.0).astype(BF16)
    before = _dot(below, hot.astype(BF16)) + cnt_ref[...]
    r1 = jnp.sum(jnp.where(lane == i1, before, 0.0), axis=-1, keepdims=True).astype(jnp.int32)
    r2 = jnp.sum(jnp.where(lane == i2, before, 0.0), axis=-1, keepdims=True).astype(jnp.int32)
    cnt_ref[...] += jnp.sum(hot, axis=0, keepdims=True)
    ids_ref[...] = jnp.where(lane == 0, i1 - N_GROUPS, jnp.where(lane == 1, i2 - N_GROUPS, jnp.where(lane == 2, r1, r2)))


def _router(h, g_ffn, w_group, b_group, w_expert, b_expert, tm=256):
    n, d = h.shape
    n_g, n_e = w_group.shape[1], w_expert.shape[1]
    n_r = n_g + n_e
    assert n_r <= LANES and TOP_K == 2 and n_g == N_GROUPS
    wr = jnp.concatenate([w_group, w_expert, jnp.zeros((d, LANES - n_r), F32)], axis=1).astype(F32)
    br = jnp.concatenate([b_group, b_expert, jnp.zeros((LANES - n_r,), F32)]).reshape(1, LANES).astype(F32)
    row = lambda i: (i, 0)
    fixed = lambda i: (0, 0)
    xn, ids, wts, cnt = pl.pallas_call(
        _router_kernel,
        out_shape=(jax.ShapeDtypeStruct((n, d), F32), jax.ShapeDtypeStruct((n, LANES), jnp.int32),
                   jax.ShapeDtypeStruct((n, LANES), F32), jax.ShapeDtypeStruct((1, LANES), F32)),
        grid=(n // tm,),
        in_specs=[pl.BlockSpec((tm, d), row), pl.BlockSpec((1, d), fixed),
                  pl.BlockSpec((d, LANES), fixed), pl.BlockSpec((1, LANES), fixed)],
        out_specs=(pl.BlockSpec((tm, d), row), pl.BlockSpec((tm, LANES), row), pl.BlockSpec((tm, LANES), row),
                   pl.BlockSpec((1, LANES), fixed)),
        compiler_params=_cparams(("arbitrary",), 40),
        name="moe_router",
    )(h, g_ffn.reshape(1, d).astype(F32), wr, br)
    return xn, ids, wts, cnt[0, n_g:n_r]


def _row_gather(idx_ref, base, src_hbm, dst, sem, n_rows, wait):
    for r in range(n_rows):
        cp = pltpu.make_async_copy(src_hbm.at[pl.ds(idx_ref[base + r], 1), :], dst.at[pl.ds(r, 1), :], sem)
        if wait:
            cp.wait()
        else:
            cp.start()


def _slot_scatter_kernel(dest_ref, tok_ref, *, top_k, unroll):
    n_out = tok_ref.shape[0]
    n_slot = dest_ref.shape[0]

    def clear(j, carry):
        for u in range(unroll):
            tok_ref[j * unroll + u] = 0
        return carry

    def place(j, carry):
        for u in range(unroll):
            s = j * unroll + u
            tok_ref[dest_ref[s]] = s // top_k
        return carry

    lax.fori_loop(0, n_out // unroll, clear, 0)
    lax.fori_loop(0, n_slot // unroll, place, 0)


def _slot_scatter(dest_flat, n_out, top_k, unroll=8):
    assert n_out % unroll == 0 and dest_flat.shape[0] % unroll == 0
    return pl.pallas_call(
        functools.partial(_slot_scatter_kernel, top_k=top_k, unroll=unroll),
        out_shape=jax.ShapeDtypeStruct((n_out,), jnp.int32),
        in_specs=[pl.BlockSpec(memory_space=pltpu.SMEM)],
        out_specs=pl.BlockSpec(memory_space=pltpu.SMEM),
        name="moe_slot_scatter",
    )(dest_flat)


def _expert_kernel(be_ref, tok_ref, nact_ref, x_hbm, wgu_ref, wd_ref, y_ref, xbuf, sem, *, blk, d_ff):
    i = pl.program_id(0)
    n_act = nact_ref[0]
    slot = i % 2

    @pl.when(i == 0)
    def _():
        _row_gather(tok_ref, 0, x_hbm, xbuf.at[0], sem.at[0], blk, wait=False)

    def run_block(prefetch_next):
        _row_gather(tok_ref, i * blk, x_hbm, xbuf.at[slot], sem.at[slot], blk, wait=True)
        if prefetch_next:
            _row_gather(tok_ref, (i + 1) * blk, x_hbm, xbuf.at[1 - slot], sem.at[1 - slot], blk, wait=False)
        xb = xbuf[slot].astype(BF16)
        gu = _dot(xb, wgu_ref[0].astype(BF16))
        act = _silu(gu[:, :d_ff]) * gu[:, d_ff:]
        y_ref[...] = _dot(act.astype(BF16), wd_ref[0].astype(BF16))

    pl.when(i + 1 < n_act)(functools.partial(run_block, True))
    pl.when(i + 1 == n_act)(functools.partial(run_block, False))

    @pl.when(i >= n_act)
    def _():
        y_ref[...] = jnp.zeros_like(y_ref)


def _experts(xn, w_gate_up, w_down, blk_expert, tok_buf, n_active, blk):
    n, d = xn.shape
    n_exp, _, ff2 = w_gate_up.shape
    d_ff = ff2 // 2
    n_blk = blk_expert.shape[0]
    return pl.pallas_call(
        functools.partial(_expert_kernel, blk=blk, d_ff=d_ff),
        out_shape=jax.ShapeDtypeStruct((n_blk * blk, d), F32),
        grid_spec=pltpu.PrefetchScalarGridSpec(
            num_scalar_prefetch=3,
            grid=(n_blk,),
            in_specs=[pl.BlockSpec(memory_space=pl.ANY),
                      pl.BlockSpec((1, d, ff2), lambda i, be, tok, na: (be[i], 0, 0), pipeline_mode=pl.Buffered(1)),
                      pl.BlockSpec((1, d_ff, d), lambda i, be, tok, na: (be[i], 0, 0), pipeline_mode=pl.Buffered(1))],
            out_specs=pl.BlockSpec((blk, d), lambda i, be, tok, na: (i, 0)),
            scratch_shapes=[pltpu.VMEM((2, blk, d), F32), pltpu.SemaphoreType.DMA((2,))]),
        compiler_params=_cparams(("arbitrary",), 56),
        name="moe_experts",
    )(blk_expert, tok_buf, n_active, xn, w_gate_up, w_down)


def _combine_kernel(pos_ref, h_ref, g_ref, w_ref, y_hbm, o_ref, ybuf, sem, *, tc):
    i = pl.program_id(0)
    n = pl.num_programs(0)
    slot = i % 2
    rows = TOP_K * tc

    @pl.when(i == 0)
    def _():
        _row_gather(pos_ref, 0, y_hbm, ybuf.at[0], sem.at[0], rows, wait=False)

    def run_tile(prefetch_next):
        _row_gather(pos_ref, i * rows, y_hbm, ybuf.at[slot], sem.at[slot], rows, wait=True)
        if prefetch_next:
            _row_gather(pos_ref, (i + 1) * rows, y_hbm, ybuf.at[1 - slot], sem.at[1 - slot], rows, wait=False)
        hh = h_ref[...]
        w = w_ref[...]
        for kk in range(TOP_K):
            hh = hh + w[:, kk:kk + 1] * ybuf[slot, kk * tc:(kk + 1) * tc, :]
        o_ref[...] = hh * lax.rsqrt(jnp.mean(hh * hh, axis=-1, keepdims=True) + RMS_EPS) * g_ref[...]

    pl.when(i + 1 < n)(functools.partial(run_tile, True))
    pl.when(i + 1 == n)(functools.partial(run_tile, False))


def _combine(h, y_buf, pos, wts, g_final, tc=128):
    n, d = h.shape
    return pl.pallas_call(
        functools.partial(_combine_kernel, tc=tc),
        out_shape=jax.ShapeDtypeStruct((n, d), F32),
        grid_spec=pltpu.PrefetchScalarGridSpec(
            num_scalar_prefetch=1,
            grid=(n // tc,),
            in_specs=[pl.BlockSpec((tc, d), lambda i, pos: (i, 0)),
                      pl.BlockSpec((1, d), lambda i, pos: (0, 0)),
                      pl.BlockSpec((tc, LANES), lambda i, pos: (i, 0)),
                      pl.BlockSpec(memory_space=pl.ANY)],
            out_specs=pl.BlockSpec((tc, d), lambda i, pos: (i, 0)),
            scratch_shapes=[pltpu.VMEM((2, TOP_K * tc, d), F32), pltpu.SemaphoreType.DMA((2,))]),
        compiler_params=_cparams(("arbitrary",), 40),
        name="moe_combine_final_norm",
    )(pos, h, g_final.reshape(1, d).astype(F32), wts, y_buf)


def _dispatch_plan(ids, counts, blk):
    n_tok = ids.shape[0]
    n_experts = counts.shape[0]
    n_slot = n_tok * TOP_K
    counts = counts.astype(jnp.int32)
    padded = (counts + blk - 1) // blk * blk
    pad_end = jnp.cumsum(padded)
    pad_start = pad_end - padded
    dest = pad_start[ids[:, :TOP_K]] + ids[:, TOP_K:2 * TOP_K]
    n_blk = -(-n_slot // blk) + n_experts
    blk_start = jnp.arange(n_blk, dtype=jnp.int32) * blk
    blk_expert = jnp.minimum(jnp.sum(pad_end[None, :] <= blk_start[:, None], axis=1), n_experts - 1).astype(jnp.int32)
    n_active = (pad_end[-1] // blk).astype(jnp.int32).reshape(1)
    return dest.astype(jnp.int32), blk_expert, n_active, n_blk


def _layer(x, g_mix, w_in, conv_w, a_log, dt_bias, norm_w, pe_k, w1_k, w2_k, pe_v, w1_v, w2_v,
           w_a, w_b, w_out, g_ffn, w_group, b_group, w_expert, b_expert, w_gate_up, w_down, g_out):
    batch, seq, dm = x.shape
    n = batch * seq
    d = HEAD_DIM
    gdn_w = w_a.shape[0]
    nsa_w = w_b.shape[0]
    heads = gdn_w // d
    nsa_heads = nsa_w // d
    groups = NSA_GROUPS
    hpg = nsa_heads // groups
    kvw = groups * d

    sizes = (3 * gdn_w, gdn_w, heads, heads, nsa_w, 6 * kvw, 3 * nsa_heads, 2 * dm)
    assert sum(sizes) == w_in.shape[1]
    offs = np.concatenate([[0], np.cumsum(sizes)])
    seg = lambda i: w_in[:, offs[i]:offs[i + 1]]
    w_big = jnp.concatenate([seg(0), seg(1), seg(4), seg(5), seg(7)], axis=1).astype(BF16)
    w_small = jnp.concatenate([seg(2), seg(3), seg(6)], axis=1).astype(BF16)
    z_col = 3 * heads
    nq_col = z_col + heads
    kv_col = nq_col + nsa_heads
    gate_col0 = (kv_col + 6 * groups) * d

    x2 = x.reshape(n, dm)
    xn = _rmsnorm(x2, g_mix, BF16)
    proj = _matmul(xn, w_big, BF16, tm=1024, tn=512, name="in_proj")
    small = _matmul(xn, w_small, F32, tm=1024, tn=w_small.shape[1], name="in_proj_small")
    a_raw, b_raw, nsa_gate = small[:, :heads], small[:, heads:2 * heads], small[:, 2 * heads:]

    gc, beta, gl = _gdn_gates(a_raw, b_raw, a_log, dt_bias, GDN_CHUNK)
    gct = gc.reshape(batch, seq, heads).transpose(0, 2, 1).reshape(batch * heads, 1, seq)
    o_gdn = _gdn_mixer(proj, conv_w, gc, beta, gl, gct, norm_w, batch, seq, heads, GDN_CHUNK)

    k_cmp, v_cmp = _nsa_compress(proj, kv_col, kv_col + groups, pe_k, w1_k, w2_k, pe_v, w1_v, w2_v,
                                 batch, seq, groups)
    gates = nsa_gate.reshape(batch, seq, groups, 3 * hpg).transpose(0, 2, 1, 3).reshape(batch * groups, seq, 3 * hpg)
    o_nsa = _nsa_mixer(proj, nq_col, kv_col + 2 * groups, kv_col + 3 * groups, kv_col + 4 * groups,
                       kv_col + 5 * groups, k_cmp, v_cmp, gates, batch, seq, groups, hpg)

    merged = _merge(o_gdn, o_nsa, w_a.astype(BF16), w_b.astype(BF16), proj, gate_col0)
    h = _resid_matmul(merged, w_out.astype(BF16), x2)

    xn2, ids, wts, counts = _router(h, g_ffn, w_group, b_group, w_expert, b_expert)
    dest, blk_expert, n_active, n_blk = _dispatch_plan(ids, counts, MOE_BLOCK)
    tok_buf = _slot_scatter(dest.reshape(n * TOP_K), n_blk * MOE_BLOCK, TOP_K)
    y_buf = _experts(xn2, w_gate_up, w_down, blk_expert, tok_buf, n_active, MOE_BLOCK)
    tc = 128
    pos_tiles = dest.reshape(n // tc, tc, TOP_K).transpose(0, 2, 1).reshape(n * TOP_K)
    out = _combine(h, y_buf, pos_tiles, wts, g_out, tc)
    return out.reshape(batch, seq, dm)


def kernel(x, g_mix, w_in, gdn_conv_w, gdn_a_log, gdn_dt_bias, gdn_norm_w, cmp_pe_k, cmp_w1_k, cmp_w2_k,
           cmp_pe_v, cmp_w1_v, cmp_w2_v, w_branch_gdn, w_branch_nsa, w_out, g_ffn, w_group, b_group,
           w_expert, b_expert, w_gate_up, w_down, g_final):
    depth = g_mix.shape[0]
    assert depth == 1, "the fused final norm assumes a single layer"
    return _layer(x, g_mix[0], w_in[0], gdn_conv_w[0], gdn_a_log[0], gdn_dt_bias[0], gdn_norm_w[0],
                  cmp_pe_k[0], cmp_w1_k[0], cmp_w2_k[0], cmp_pe_v[0], cmp_w1_v[0], cmp_w2_v[0],
                  w_branch_gdn[0], w_branch_nsa[0], w_out[0], g_ffn[0], w_group[0], b_group[0],
                  w_expert[0], b_expert[0], w_gate_up[0], w_down[0], g_final)
```

```python
import functools
import math

import numpy as np
import jax
import jax.numpy as jnp
from jax import lax
from jax.experimental import pallas as pl
from jax.experimental.pallas import tpu as pltpu

F32 = jnp.float32
BF16 = jnp.bfloat16

RMS_EPS = 1e-6
NEG_INF = -1e30
FORCE_SCORE = 1e6

HEAD_DIM = 128
GDN_CHUNK = 64
NSA_GROUPS = 4
CMP_BLOCK = 32
CMP_STRIDE = 16
SEL_BLOCK = 64
SEL_TOP_N = 16
WINDOW = 512
N_GROUPS = 8
EXPERTS_PER_GROUP = 8
TOP_K = 2
MOE_BLOCK = 128
LANES = 128

V7X_VMEM_BYTES = 64 * 1024 * 1024


def _cparams(semantics, vmem_mb):
    assert vmem_mb * 1024 * 1024 < V7X_VMEM_BYTES
    return pltpu.CompilerParams(dimension_semantics=semantics, vmem_limit_bytes=vmem_mb * 1024 * 1024)


def _dot(a, b, **kw):
    return jnp.dot(a, b, preferred_element_type=F32, **kw)


def _dot_nt(a, b):
    return lax.dot_general(a, b, (((1,), (1,)), ((), ())), preferred_element_type=F32)


def _dot_tn(a, b):
    return lax.dot_general(a, b, (((0,), (0,)), ((), ())), preferred_element_type=F32)


def _sigmoid(x):
    return 1.0 / (1.0 + jnp.exp(-x))


def _silu(x):
    return x * _sigmoid(x)


def _rmsnorm_kernel(x_ref, g_ref, o_ref):
    x = x_ref[...].astype(F32)
    ms = jnp.mean(x * x, axis=-1, keepdims=True)
    o_ref[...] = (x * lax.rsqrt(ms + RMS_EPS) * g_ref[...]).astype(o_ref.dtype)


def _rmsnorm(x, gain, out_dtype, tm=512):
    n, d = x.shape
    return pl.pallas_call(
        _rmsnorm_kernel,
        out_shape=jax.ShapeDtypeStruct((n, d), out_dtype),
        grid=(n // tm,),
        in_specs=[pl.BlockSpec((tm, d), lambda i: (i, 0)), pl.BlockSpec((1, d), lambda i: (0, 0))],
        out_specs=pl.BlockSpec((tm, d), lambda i: (i, 0)),
        compiler_params=_cparams(("parallel",), 40),
        name="rmsnorm",
    )(x, gain.reshape(1, d).astype(F32))


def _mm_kernel(x_ref, w_ref, o_ref):
    o_ref[...] = _dot(x_ref[...], w_ref[...]).astype(o_ref.dtype)


def _matmul(x, w, out_dtype, tm, tn, name):
    m, k = x.shape
    n = w.shape[1]
    return pl.pallas_call(
        _mm_kernel,
        out_shape=jax.ShapeDtypeStruct((m, n), out_dtype),
        grid=(m // tm, n // tn),
        in_specs=[pl.BlockSpec((tm, k), lambda i, j: (i, 0)), pl.BlockSpec((k, tn), lambda i, j: (0, j))],
        out_specs=pl.BlockSpec((tm, tn), lambda i, j: (i, j)),
        compiler_params=_cparams(("parallel", "arbitrary"), 48),
        name=name,
    )(x, w)


def _merge_kernel(a_ref, b_ref, wa_ref, wb_ref, ga_ref, gb_ref, o_ref):
    ya = _dot(a_ref[...], wa_ref[...])
    yb = _dot(b_ref[...], wb_ref[...])
    o_ref[...] = (_sigmoid(ga_ref[...].astype(F32)) * ya + _sigmoid(gb_ref[...].astype(F32)) * yb).astype(o_ref.dtype)


def _merge(o_gdn, o_nsa, w_a, w_b, proj, gate_col0, tm=1024, tn=512):
    n, ka = o_gdn.shape
    kb = o_nsa.shape[1]
    d = w_a.shape[1]
    ja, jb = gate_col0 // tn, (gate_col0 + d) // tn
    return pl.pallas_call(
        _merge_kernel,
        out_shape=jax.ShapeDtypeStruct((n, d), BF16),
        grid=(n // tm, d // tn),
        in_specs=[pl.BlockSpec((tm, ka), lambda i, j: (i, 0)),
                  pl.BlockSpec((tm, kb), lambda i, j: (i, 0)),
                  pl.BlockSpec((ka, tn), lambda i, j: (0, j)),
                  pl.BlockSpec((kb, tn), lambda i, j: (0, j)),
                  pl.BlockSpec((tm, tn), lambda i, j: (i, ja + j)),
                  pl.BlockSpec((tm, tn), lambda i, j: (i, jb + j))],
        out_specs=pl.BlockSpec((tm, tn), lambda i, j: (i, j)),
        compiler_params=_cparams(("parallel", "arbitrary"), 48),
        name="branch_merge",
    )(o_gdn, o_nsa, w_a, w_b, proj, proj)


def _resid_mm_kernel(x_ref, w_ref, r_ref, o_ref):
    o_ref[...] = r_ref[...] + _dot(x_ref[...], w_ref[...])


def _resid_matmul(x, w, resid, tm=1024, tn=512):
    m, k = x.shape
    n = w.shape[1]
    return pl.pallas_call(
        _resid_mm_kernel,
        out_shape=jax.ShapeDtypeStruct((m, n), F32),
        grid=(m // tm, n // tn),
        in_specs=[pl.BlockSpec((tm, k), lambda i, j: (i, 0)),
                  pl.BlockSpec((k, tn), lambda i, j: (0, j)),
                  pl.BlockSpec((tm, tn), lambda i, j: (i, j))],
        out_specs=pl.BlockSpec((tm, tn), lambda i, j: (i, j)),
        compiler_params=_cparams(("parallel", "arbitrary"), 48),
        name="out_proj_residual",
    )(x, w, resid)


def _gdn_gate_kernel(a_ref, b_ref, alog_ref, dtb_ref, gc_ref, beta_ref, gl_ref, *, chunk):
    tm = a_ref.shape[0]
    x = a_ref[...] + dtb_ref[...]
    softplus = jnp.maximum(x, 0.0) + jnp.log(1.0 + jnp.exp(-jnp.abs(x)))
    g = -jnp.exp(alog_ref[...]) * softplus
    row = lax.broadcasted_iota(jnp.int32, (tm, tm), 0)
    col = lax.broadcasted_iota(jnp.int32, (tm, tm), 1)
    same = row // chunk == col // chunk
    tri = jnp.where((col <= row) & same, 1.0, 0.0).astype(F32)
    gc_ref[...] = _dot(tri, g, precision=lax.Precision.HIGHEST)
    gl_ref[...] = _dot(jnp.where(same, 1.0, 0.0).astype(F32), g, precision=lax.Precision.HIGHEST)
    beta_ref[...] = _sigmoid(b_ref[...])


def _gdn_gates(a_raw, b_raw, a_log, dt_bias, chunk, tm=512):
    n, h = a_raw.shape
    spec = pl.BlockSpec((tm, h), lambda i: (i, 0))
    vec = pl.BlockSpec((1, h), lambda i: (0, 0))
    sds = jax.ShapeDtypeStruct((n, h), F32)
    return pl.pallas_call(
        functools.partial(_gdn_gate_kernel, chunk=chunk),
        out_shape=(sds, sds, sds),
        grid=(n // tm,),
        in_specs=[spec, spec, vec, vec],
        out_specs=(spec, spec, spec),
        compiler_params=_cparams(("parallel",), 32),
        name="gdn_gates",
    )(a_raw, b_raw, a_log.reshape(1, h).astype(F32), dt_bias.reshape(1, h).astype(F32))


_CONV_PAD = 8


def _gdn_kernel(q_ref, k_ref, v_ref, z_ref, cwq_ref, cwk_ref, cwv_ref, gc_ref, beta_ref, gl_ref, gct_ref, nw_ref,
                o_ref, state, xbuf, *, chunk, hps):
    hblk = pl.program_id(1)
    t = pl.program_id(2)
    tt = q_ref.shape[0]
    d = HEAD_DIM
    width = cwq_ref.shape[0]
    c = chunk
    n_ch = tt // c
    n_lvl = int(math.log2(c))
    assert 2 ** n_lvl == c

    @pl.when(t == 0)
    def _():
        state[...] = jnp.zeros_like(state)
        xbuf[:, 0:_CONV_PAD, :] = jnp.zeros((3 * hps, _CONV_PAD, d), F32)

    def conv_silu(s, x_bf16, w):
        xbuf[s, _CONV_PAD:_CONV_PAD + tt, :] = x_bf16.astype(F32)
        y = None
        for j in range(width):
            term = xbuf[s, pl.ds(_CONV_PAD - (width - 1) + j, tt), :] * w[j:j + 1, :]
            y = term if y is None else y + term
        xbuf[s, 0:_CONV_PAD, :] = xbuf[s, tt:tt + _CONV_PAD, :]
        return _silu(y)

    def l2n(x):
        return x * lax.rsqrt(jnp.sum(x * x, axis=-1, keepdims=True) + RMS_EPS)

    ri = lax.broadcasted_iota(jnp.int32, (tt, tt), 0)
    ci = lax.broadcasted_iota(jnp.int32, (tt, tt), 1)
    incl = (ri >= ci) & (ri // c == ci // c)
    strict = ri > ci
    lane = lax.broadcasted_iota(jnp.int32, gc_ref.shape, 1)
    nw = nw_ref[...]

    hd = [dict() for _ in range(hps)]
    for hp, e in enumerate(hd):
        h = hblk * hps + hp
        hs = slice(hp * d, (hp + 1) * d)
        q = l2n(conv_silu(3 * hp, q_ref[:, hs], cwq_ref[:, hs])) * (d ** -0.5)
        k = l2n(conv_silu(3 * hp + 1, k_ref[:, hs], cwk_ref[:, hs]))
        v = conv_silu(3 * hp + 2, v_ref[:, hs], cwv_ref[:, hs])
        pick = lambda ref: jnp.sum(jnp.where(lane == h, ref[...], 0.0), axis=1, keepdims=True)
        gcc, bc, glc = pick(gc_ref), pick(beta_ref), pick(gl_ref)
        egc = jnp.exp(gcc)
        kb = k.astype(BF16)
        qk = _dot_nt(jnp.concatenate([q.astype(BF16), kb], axis=0), kb)
        dm = jnp.exp(jnp.where(incl, gcc - gct_ref[hp], NEG_INF))
        e["npow"] = jnp.where(strict, -(qk[tt:] * dm * bc), 0.0)
        e["x"] = jnp.concatenate([v * bc, k * (bc * egc)], axis=1)
        e["attn"] = (qk[:tt] * dm).astype(BF16)
        e["qe"] = q * egc
        e["kout"] = (k * jnp.exp(glc - gcc)).astype(BF16)
        e["dch"] = jnp.exp(glc)

    for lvl in range(n_lvl):
        for e in hd:
            nb = e["npow"].astype(BF16)
            if lvl + 1 < n_lvl:
                r = _dot(nb, jnp.concatenate([nb, e["x"].astype(BF16)], axis=1))
                e["npow"], e["x"] = r[:, :tt], e["x"] + r[:, tt:]
            else:
                e["x"] = e["x"] + _dot(nb, e["x"].astype(BF16))

    for hp, e in enumerate(hd):
        e["xb"] = e["x"].astype(BF16)
        au = _dot(e["attn"], e["xb"])
        e["o0"] = au[:, :d]
        e["qe"] = (e["qe"] - au[:, d:]).astype(BF16)
        e["s"] = state[hp]

    for ch in range(n_ch):
        sl = slice(ch * c, (ch + 1) * c)
        for hp, e in enumerate(hd):
            hs = slice(hp * d, (hp + 1) * d)
            ktx = _dot_tn(e["kout"][sl], e["xb"][sl])
            sb = e["s"].astype(BF16)
            o = _dot(e["qe"][sl], sb) + e["o0"][sl]
            e["s"] = e["s"] * e["dch"][ch * c:ch * c + 1, :] - _dot(ktx[:, d:].astype(BF16), sb) + ktx[:, :d]
            on = o * lax.rsqrt(jnp.mean(o * o, axis=-1, keepdims=True) + RMS_EPS) * nw
            o_ref[sl, hs] = (on * _silu(z_ref[sl, hs].astype(F32))).astype(o_ref.dtype)

    for hp, e in enumerate(hd):
        state[hp] = e["s"]


def _gdn_mixer(proj, conv_w, gc, beta, gl, gct, norm_w, batch, seq, heads, chunk, tt=256, hps=4):
    d = HEAD_DIM
    n = batch * seq
    nt = seq // tt
    hb = heads // hps
    width = conv_w.shape[0]
    assert width - 1 <= _CONV_PAD and tt % chunk == 0 and seq % tt == 0 and heads % hps == 0

    def slab(off):
        return pl.BlockSpec((tt, hps * d), lambda b, h, t: (b * nt + t, off + h))

    def cw(off):
        return pl.BlockSpec((width, hps * d), lambda b, h, t: (0, off + h))

    gspec = pl.BlockSpec((tt, heads), lambda b, h, t: (b * nt + t, 0))
    return pl.pallas_call(
        functools.partial(_gdn_kernel, chunk=chunk, hps=hps),
        out_shape=jax.ShapeDtypeStruct((n, heads * d), BF16),
        grid=(batch, hb, nt),
        in_specs=[slab(0), slab(hb), slab(2 * hb), slab(3 * hb),
                  cw(0), cw(hb), cw(2 * hb),
                  gspec, gspec, gspec,
                  pl.BlockSpec((hps, 1, tt), lambda b, h, t: (b * hb + h, 0, t)),
                  pl.BlockSpec((1, d), lambda b, h, t: (0, 0))],
        out_specs=pl.BlockSpec((tt, hps * d), lambda b, h, t: (b * nt + t, h)),
        scratch_shapes=[pltpu.VMEM((hps, d, d), F32), pltpu.VMEM((3 * hps, tt + _CONV_PAD, d), F32)],
        compiler_params=_cparams(("parallel", "parallel", "arbitrary"), 40),
        name="gdn_mixer",
    )(proj, proj, proj, proj, conv_w, conv_w, conv_w, gc, beta, gl, gct, norm_w.reshape(1, d).astype(F32))


def _nsa_cmp_kernel(kc_ref, vc_ref, pek_ref, w1k_ref, w2k_ref, pev_ref, w1v_ref, w2v_ref,
                    ko_ref, vo_ref, xf, sh, *, stride, block):
    seq, d = kc_ref.shape
    nh = seq // stride
    reps = block // stride
    assert nh % 8 == 0

    def compress(src_ref, pe_ref, w1_ref, w2_ref, dst_ref):
        xf[...] = src_ref[...].astype(F32)
        pe = pe_ref[...]
        hid = None
        for r in range(reps):
            acc = None
            for l in range(stride):
                li = r * stride + l
                rows = xf[pl.ds(l, nh, stride=stride), :] + pe[li:li + 1, :]
                term = _dot(rows.astype(BF16), w1_ref[li].astype(BF16))
                acc = term if acc is None else acc + term
            if r == 0:
                hid = acc
            else:
                sh[0:nh, :] = acc
                sh[nh:nh + 8, :] = jnp.zeros((8, d), F32)
                hid = hid + sh[pl.ds(r, nh), :]
        hid = _silu(hid)
        dst_ref[0] = _dot(hid.astype(BF16), w2_ref[...].astype(BF16)).astype(dst_ref.dtype)

    compress(kc_ref, pek_ref, w1k_ref, w2k_ref, ko_ref)
    compress(vc_ref, pev_ref, w1v_ref, w2v_ref, vo_ref)


def _nsa_compress(proj, kc_col, vc_col, pe_k, w1_k, w2_k, pe_v, w1_v, w2_v, batch, seq, groups):
    d = HEAD_DIM
    nh = seq // CMP_STRIDE
    full2 = lambda b, g: (0, 0)
    full3 = lambda b, g: (0, 0, 0)
    out_sds = jax.ShapeDtypeStruct((batch * groups, nh, d), BF16)
    out_spec = pl.BlockSpec((1, nh, d), lambda b, g: (b * groups + g, 0, 0))
    return pl.pallas_call(
        functools.partial(_nsa_cmp_kernel, stride=CMP_STRIDE, block=CMP_BLOCK),
        out_shape=(out_sds, out_sds),
        grid=(batch, groups),
        in_specs=[pl.BlockSpec((seq, d), lambda b, g: (b, kc_col + g)),
                  pl.BlockSpec((seq, d), lambda b, g: (b, vc_col + g)),
                  pl.BlockSpec((CMP_BLOCK, d), full2), pl.BlockSpec((CMP_BLOCK, d, d), full3), pl.BlockSpec((d, d), full2),
                  pl.BlockSpec((CMP_BLOCK, d), full2), pl.BlockSpec((CMP_BLOCK, d, d), full3), pl.BlockSpec((d, d), full2)],
        out_specs=(out_spec, out_spec),
        scratch_shapes=[pltpu.VMEM((seq, d), F32), pltpu.VMEM((nh + 8, d), F32)],
        compiler_params=_cparams(("parallel", "parallel"), 32),
        name="nsa_compress",
    )(proj, proj, pe_k, w1_k, w2_k, pe_v, w1_v, w2_v)


def _nsa_kernel(q_ref, kc_ref, vc_ref, ks_ref, vs_ref, kw_ref, vw_ref, gate_ref, selmap_ref, expand_ref,
                o_ref, score_t, *, hpg, gps, tk, n_sel, n_top):
    tq = q_ref.shape[0]
    d = HEAD_DIM
    gw = hpg * d
    n_cmp_pad = kc_ref.shape[1]
    t0 = pl.program_id(2) * tq
    rows = hpg * tq
    grp = list(range(gps))
    kv = lambda ref, gi, r0, n: ref[pl.ds(r0, n), gi * d:(gi + 1) * d]

    q4 = []
    for gi in grp:
        qg = jnp.concatenate([q_ref[:, gi * gw + hh * d:gi * gw + (hh + 1) * d] for hh in range(hpg)], axis=0)
        q4.append((qg.astype(F32) * (d ** -0.5 * math.log2(math.e))).astype(BF16))
    trow = t0 + lax.broadcasted_iota(jnp.int32, (tq, 1), 0)

    ncol = lax.broadcasted_iota(jnp.int32, (1, n_cmp_pad), 1)
    vis_c = ((ncol * CMP_STRIDE + (CMP_BLOCK - 1)) <= trow)[None]
    s_c = [_dot_nt(q4[gi], kc_ref[gi]).reshape(hpg, tq, n_cmp_pad) for gi in grp]
    p_c = []
    for gi in grp:
        s3 = jnp.where(vis_c, s_c[gi], NEG_INF)
        e = jnp.where(vis_c, jnp.exp2(s3 - jnp.max(s3, axis=-1, keepdims=True)), 0.0)
        den = jnp.sum(e, axis=-1, keepdims=True)
        p_c.append(e / jnp.where(den > 0.0, den, 1.0))
    o_cmp = [_dot(p_c[gi].reshape(rows, n_cmp_pad).astype(BF16), vc_ref[gi]).reshape(hpg, tq, d) for gi in grp]

    jrow = lax.broadcasted_iota(jnp.int32, (n_sel, 1), 0)
    tcol = t0 + lax.broadcasted_iota(jnp.int32, (1, tq), 1)
    cur = tcol // SEL_BLOCK
    forced = (jrow == 0) | (jrow == cur) | (jrow == cur - 1)
    causal_blk = jrow * SEL_BLOCK <= tcol
    keys = []
    for gi in grp:
        imp_t = lax.dot_general(selmap_ref[...], jnp.sum(p_c[gi], axis=0), (((1,), (1,)), ((), ())),
                                preferred_element_type=F32, precision=lax.Precision.HIGHEST)
        sc = jnp.where(forced, FORCE_SCORE, jnp.where(causal_blk, imp_t, NEG_INF))
        bits = pltpu.bitcast(sc, jnp.int32)
        keys.append(bits ^ ((bits >> 31) & jnp.int32(0x7FFFFFFF)))
        score_t[gi] = keys[gi]

    def rank_step(i, cnts):
        tie = jnp.where(jrow > i, 1, 0)
        return tuple(cnts[gi] + jnp.where(score_t[gi, pl.ds(i, 1), :] + tie > keys[gi], 1.0, 0.0) for gi in grp)

    n_rank = jnp.minimum((t0 + tq - 1) // SEL_BLOCK + 1, n_sel)
    cnts = lax.fori_loop(0, n_rank, rank_step, tuple(jnp.zeros((n_sel, tq), F32) for _ in grp))
    eye = jnp.where(lax.broadcasted_iota(jnp.int32, (n_sel, n_sel), 0)
                    == lax.broadcasted_iota(jnp.int32, (n_sel, n_sel), 1), 1.0, 0.0).astype(BF16)
    sel_bias = [_dot_tn(jnp.where(cnts[gi] < n_top, 0.0, NEG_INF).astype(BF16), eye).astype(BF16) for gi in grp]

    def sel_step(kt, carry, diagonal):
        k0 = pl.multiple_of(kt * tk, tk)
        ex = expand_ref[:, pl.ds(k0, tk)]
        s3 = []
        for gi in grp:
            b = _dot(sel_bias[gi], ex)
            if diagonal:
                kpos = k0 + lax.broadcasted_iota(jnp.int32, (1, tk), 1)
                b = jnp.where(kpos <= trow, b, NEG_INF)
            s3.append(_dot_nt(q4[gi], kv(ks_ref, gi, k0, tk)).reshape(hpg, tq, tk) + b[None])
        out = []
        for gi in grp:
            m, l, acc = carry[gi]
            m_new = jnp.maximum(m, jnp.max(s3[gi], axis=-1, keepdims=True))
            alpha = jnp.exp2(m - m_new)
            e = jnp.exp2(s3[gi] - m_new)
            l = alpha * l + jnp.sum(e, axis=-1, keepdims=True)
            pv = _dot(e.reshape(rows, tk).astype(BF16), kv(vs_ref, gi, k0, tk))
            out.append((m_new, l, alpha * acc + pv.reshape(hpg, tq, d)))
        return tuple(out)

    n_full = t0 // tk
    init = (jnp.full((hpg, tq, 1), NEG_INF, F32), jnp.zeros((hpg, tq, 1), F32), jnp.zeros((hpg, tq, d), F32))
    carry = lax.fori_loop(0, n_full, functools.partial(sel_step, diagonal=False), tuple(init for _ in grp))
    carry = sel_step(n_full, carry, diagonal=True)
    o_sel = [carry[gi][2] / carry[gi][1] for gi in grp]

    span = WINDOW + tq
    ws = pl.multiple_of(jnp.maximum(t0 - WINDOW, 0), tq)
    kpos = ws + lax.broadcasted_iota(jnp.int32, (1, span), 1)
    rel = trow - kpos
    b_w = jnp.where((rel >= 0) & (rel < WINDOW), 0.0, NEG_INF)[None]
    s_w = [_dot_nt(q4[gi], kv(kw_ref, gi, ws, span)).reshape(hpg, tq, span) + b_w for gi in grp]
    o_win = []
    for gi in grp:
        e = jnp.exp2(s_w[gi] - jnp.max(s_w[gi], axis=-1, keepdims=True))
        pv = _dot(e.reshape(rows, span).astype(BF16), kv(vw_ref, gi, ws, span))
        o_win.append(pv.reshape(hpg, tq, d) / jnp.sum(e, axis=-1, keepdims=True))

    for gi in grp:
        gates = _sigmoid(gate_ref[gi])
        for hh in range(hpg):
            g0 = gates[:, 3 * hh:3 * hh + 1]
            g1 = gates[:, 3 * hh + 1:3 * hh + 2]
            g2 = gates[:, 3 * hh + 2:3 * hh + 3]
            o_ref[:, gi * gw + hh * d:gi * gw + (hh + 1) * d] = (
                g0 * o_cmp[gi][hh] + g1 * o_sel[gi][hh] + g2 * o_win[gi][hh]).astype(o_ref.dtype)


def _nsa_mixer(proj, q_col, ks_col, vs_col, kw_col, vw_col, k_cmp, v_cmp, gates, batch, seq, groups, hpg,
               tq=128, tk=512, gps=2):
    d = HEAD_DIM
    n = batch * seq
    nq = seq // tq
    n_sel = seq // SEL_BLOCK
    n_top = min(SEL_TOP_N, n_sel)
    n_cmp = (seq - CMP_BLOCK) // CMP_STRIDE + 1
    n_cmp_pad = k_cmp.shape[1]
    assert n_sel % 8 == 0 and seq >= WINDOW + tq and WINDOW % tq == 0 and tk % tq == 0 and seq % tk == 0

    cmp_start = np.arange(n_cmp_pad) * CMP_STRIDE
    sel_start = np.arange(n_sel) * SEL_BLOCK
    overlap = (np.minimum(cmp_start[None, :] + CMP_BLOCK, sel_start[:, None] + SEL_BLOCK)
               - np.maximum(cmp_start[None, :], sel_start[:, None]))
    sel_map_t = np.clip(overlap, 0, None) / CMP_STRIDE
    sel_map_t[:, n_cmp:] = 0.0
    expand = (np.arange(seq)[None, :] // SEL_BLOCK == np.arange(n_sel)[:, None]).astype(np.float32)

    gb = groups // gps
    qw = gps * hpg * d
    assert groups % gps == 0 and all(c % gps == 0 for c in (ks_col, vs_col, kw_col, vw_col)) and q_col % (gps * hpg) == 0

    def kv(col):
        return pl.BlockSpec((seq, gps * d), lambda b, g, t: (b, col // gps + g))

    cmp_spec = pl.BlockSpec((gps, n_cmp_pad, d), lambda b, g, t: (b * gb + g, 0, 0))
    return pl.pallas_call(
        functools.partial(_nsa_kernel, hpg=hpg, gps=gps, tk=tk, n_sel=n_sel, n_top=n_top),
        out_shape=jax.ShapeDtypeStruct((n, groups * hpg * d), BF16),
        grid=(batch, gb, nq),
        in_specs=[pl.BlockSpec((tq, qw), lambda b, g, t: (b * nq + t, q_col // (gps * hpg) + g)),
                  cmp_spec, cmp_spec, kv(ks_col), kv(vs_col), kv(kw_col), kv(vw_col),
                  pl.BlockSpec((gps, tq, 3 * hpg), lambda b, g, t: (b * gb + g, t, 0)),
                  pl.BlockSpec((n_sel, n_cmp_pad), lambda b, g, t: (0, 0)),
                  pl.BlockSpec((n_sel, seq), lambda b, g, t: (0, 0))],
        out_specs=pl.BlockSpec((tq, qw), lambda b, g, t: (b * nq + t, g)),
        scratch_shapes=[pltpu.VMEM((gps, n_sel, tq), jnp.int32)],
        compiler_params=_cparams(("parallel", "parallel", "arbitrary"), 48),
        name="nsa_attention",
    )(proj, k_cmp, v_cmp, proj, proj, proj, proj, gates, jnp.asarray(sel_map_t, F32), jnp.asarray(expand, BF16))


def _router_kernel(h_ref, g_ref, wr_ref, br_ref, xn_ref, ids_ref, wts_ref, cnt_ref):
    x = h_ref[...]
    xn = x * lax.rsqrt(jnp.mean(x * x, axis=-1, keepdims=True) + RMS_EPS) * g_ref[...]
    xn_ref[...] = xn
    logits = _dot(xn, wr_ref[...], precision=lax.Precision.HIGHEST) + br_ref[...]
    lane = lax.broadcasted_iota(jnp.int32, logits.shape, 1)
    big = jnp.int32(2 * LANES)
    is_g = lane < N_GROUPS
    gl = jnp.where(is_g, logits, NEG_INF)
    gm = jnp.max(gl, axis=-1, keepdims=True)
    p_g = 1.0 / jnp.sum(jnp.where(is_g, jnp.exp(gl - gm), 0.0), axis=-1, keepdims=True)
    grp = jnp.min(jnp.where(gl == gm, lane, big), axis=-1, keepdims=True)
    eidx = lane - N_GROUPS
    in_grp = (eidx >= 0) & (eidx // EXPERTS_PER_GROUP == grp) & (eidx < N_GROUPS * EXPERTS_PER_GROUP)
    el = jnp.where(in_grp, logits, NEG_INF)
    em = jnp.max(el, axis=-1, keepdims=True)
    ee = jnp.where(in_grp, jnp.exp(el - em), 0.0)
    pe = ee / jnp.sum(ee, axis=-1, keepdims=True)
    p1 = jnp.max(jnp.where(in_grp, pe, -1.0), axis=-1, keepdims=True)
    i1 = jnp.min(jnp.where(in_grp & (pe == p1), lane, big), axis=-1, keepdims=True)
    rest = in_grp & (lane != i1)
    p2 = jnp.max(jnp.where(rest, pe, -1.0), axis=-1, keepdims=True)
    i2 = jnp.min(jnp.where(rest & (pe == p2), lane, big), axis=-1, keepdims=True)
    denom = p1 + p2
    wts_ref[...] = jnp.where(lane == 0, p_g * p1 / denom, p_g * p2 / denom)

    @pl.when(pl.program_id(0) == 0)
    def _():
        cnt_ref[...] = jnp.zeros_like(cnt_ref)

    tm = logits.shape[0]
    hot = jnp.where((lane == i1) | (lane == i2), 1.0, 0.0)
    below = jnp.where(lax.broadcasted_iota(jnp.int32, (tm, tm), 0) > lax.broadcasted_iota(jnp.int32, (tm, tm), 1),
                      1.0, 0.0).astype(BF16)
    before = _dot(below, hot.astype(BF16)) + cnt_ref[...]
    r1 = jnp.sum(jnp.where(lane == i1, before, 0.0), axis=-1, keepdims=True).astype(jnp.int32)
    r2 = jnp.sum(jnp.where(lane == i2, before, 0.0), axis=-1, keepdims=True).astype(jnp.int32)
    cnt_ref[...] += jnp.sum(hot, axis=0, keepdims=True)
    ids_ref[...] = jnp.where(lane == 0, i1 - N_GROUPS, jnp.where(lane == 1, i2 - N_GROUPS, jnp.where(lane == 2, r1, r2)))


def _router(h, g_ffn, w_group, b_group, w_expert, b_expert, tm=256):
    n, d = h.shape
    n_g, n_e = w_group.shape[1], w_expert.shape[1]
    n_r = n_g + n_e
    assert n_r <= LANES and TOP_K == 2 and n_g == N_GROUPS
    wr = jnp.concatenate([w_group, w_expert, jnp.zeros((d, LANES - n_r), F32)], axis=1).astype(F32)
    br = jnp.concatenate([b_group, b_expert, jnp.zeros((LANES - n_r,), F32)]).reshape(1, LANES).astype(F32)
    row = lambda i: (i, 0)
    fixed = lambda i: (0, 0)
    xn, ids, wts, cnt = pl.pallas_call(
        _router_kernel,
        out_shape=(jax.ShapeDtypeStruct((n, d), F32), jax.ShapeDtypeStruct((n, LANES), jnp.int32),
                   jax.ShapeDtypeStruct((n, LANES), F32), jax.ShapeDtypeStruct((1, LANES), F32)),
        grid=(n // tm,),
        in_specs=[pl.BlockSpec((tm, d), row), pl.BlockSpec((1, d), fixed),
                  pl.BlockSpec((d, LANES), fixed), pl.BlockSpec((1, LANES), fixed)],
        out_specs=(pl.BlockSpec((tm, d), row), pl.BlockSpec((tm, LANES), row), pl.BlockSpec((tm, LANES), row),
                   pl.BlockSpec((1, LANES), fixed)),
        compiler_params=_cparams(("arbitrary",), 40),
        name="moe_router",
    )(h, g_ffn.reshape(1, d).astype(F32), wr, br)
    return xn, ids, wts, cnt[0, n_g:n_r]


def _row_gather(idx_ref, base, src_hbm, dst, sem, n_rows, wait):
    for r in range(n_rows):
        cp = pltpu.make_async_copy(src_hbm.at[pl.ds(idx_ref[base + r], 1), :], dst.at[pl.ds(r, 1), :], sem)
        if wait:
            cp.wait()
        else:
            cp.start()


def _slot_scatter_kernel(dest_ref, tok_ref, *, top_k, unroll):
    n_out = tok_ref.shape[0]
    n_slot = dest_ref.shape[0]

    def clear(j, carry):
        for u in range(unroll):
            tok_ref[j * unroll + u] = 0
        return carry

    def place(j, carry):
        rows = [dest_ref[j * unroll + u] for u in range(unroll)]
        for u in range(unroll):
            tok_ref[rows[u]] = (j * unroll + u) // top_k
        return carry

    lax.fori_loop(0, n_out // unroll, clear, 0)
    lax.fori_loop(0, n_slot // unroll, place, 0)


def _slot_scatter(dest_flat, n_out, top_k, unroll=8):
    assert n_out % unroll == 0 and dest_flat.shape[0] % unroll == 0
    return pl.pallas_call(
        functools.partial(_slot_scatter_kernel, top_k=top_k, unroll=unroll),
        out_shape=jax.ShapeDtypeStruct((n_out,), jnp.int32),
        in_specs=[pl.BlockSpec(memory_space=pltpu.SMEM)],
        out_specs=pl.BlockSpec(memory_space=pltpu.SMEM),
        name="moe_slot_scatter",
    )(dest_flat)


_FF_SPLIT = 2


def _expert_act_kernel(be_ref, tok_ref, nact_ref, x_hbm, wg_ref, wu_ref, a_ref, xbuf, sem, *, blk):
    p = pl.program_id(0)
    i = pl.program_id(1)
    n_act = nact_ref[0]
    slot = (p * n_act + i) % 2

    @pl.when((p == 0) & (i == 0))
    def _():
        _row_gather(tok_ref, 0, x_hbm, xbuf.at[0], sem.at[0], blk, wait=False)

    def run_block(next_block):
        _row_gather(tok_ref, i * blk, x_hbm, xbuf.at[slot], sem.at[slot], blk, wait=True)
        if next_block is not None:
            _row_gather(tok_ref, next_block * blk, x_hbm, xbuf.at[1 - slot], sem.at[1 - slot], blk, wait=False)
        xb = xbuf[slot].astype(BF16)
        g = _dot(xb, wg_ref[0].astype(BF16))
        u = _dot(xb, wu_ref[0].astype(BF16))
        a_ref[...] = (_silu(g) * u).astype(a_ref.dtype)

    last = i + 1 == n_act
    pl.when(i + 1 < n_act)(lambda: run_block(i + 1))
    pl.when(last & (p + 1 < _FF_SPLIT))(lambda: run_block(0))
    pl.when(last & (p + 1 == _FF_SPLIT))(lambda: run_block(None))

    @pl.when(i >= n_act)
    def _():
        a_ref[...] = jnp.zeros_like(a_ref)


def _expert_down_kernel(be_ref, nact_ref, a_ref, wd_ref, y_ref):
    i = pl.program_id(0)

    @pl.when(i < nact_ref[0])
    def _():
        y_ref[...] = _dot(a_ref[...], wd_ref[0].astype(BF16))

    @pl.when(i >= nact_ref[0])
    def _():
        y_ref[...] = jnp.zeros_like(y_ref)


def _experts(xn, w_gate_up, w_down, blk_expert, tok_buf, n_active, blk):
    n, d = xn.shape
    n_exp, _, ff2 = w_gate_up.shape
    d_ff = ff2 // 2
    fs = d_ff // _FF_SPLIT
    assert fs % LANES == 0
    n_blk = blk_expert.shape[0]
    act = pl.pallas_call(
        functools.partial(_expert_act_kernel, blk=blk),
        out_shape=jax.ShapeDtypeStruct((n_blk * blk, d_ff), BF16),
        grid_spec=pltpu.PrefetchScalarGridSpec(
            num_scalar_prefetch=3,
            grid=(_FF_SPLIT, n_blk),
            in_specs=[pl.BlockSpec(memory_space=pl.ANY),
                      pl.BlockSpec((1, d, fs), lambda p, i, be, tok, na: (be[i], 0, p)),
                      pl.BlockSpec((1, d, fs), lambda p, i, be, tok, na: (be[i], 0, _FF_SPLIT + p))],
            out_specs=pl.BlockSpec((blk, fs), lambda p, i, be, tok, na: (i, p)),
            scratch_shapes=[pltpu.VMEM((2, blk, d), F32), pltpu.SemaphoreType.DMA((2,))]),
        compiler_params=_cparams(("arbitrary", "arbitrary"), 48),
        name="moe_expert_act",
    )(blk_expert, tok_buf, n_active, xn, w_gate_up, w_gate_up)
    return pl.pallas_call(
        _expert_down_kernel,
        out_shape=jax.ShapeDtypeStruct((n_blk * blk, d), F32),
        grid_spec=pltpu.PrefetchScalarGridSpec(
            num_scalar_prefetch=2,
            grid=(n_blk,),
            in_specs=[pl.BlockSpec((blk, d_ff), lambda i, be, na: (i, 0)),
                      pl.BlockSpec((1, d_ff, d), lambda i, be, na: (be[i], 0, 0))],
            out_specs=pl.BlockSpec((blk, d), lambda i, be, na: (i, 0))),
        compiler_params=_cparams(("arbitrary",), 48),
        name="moe_expert_down",
    )(blk_expert, n_active, act, w_down)


def _combine_kernel(pos_ref, h_ref, g_ref, w_ref, y_hbm, o_ref, ybuf, sem, *, tc):
    i = pl.program_id(0)
    n = pl.num_programs(0)
    slot = i % 2
    rows = TOP_K * tc

    @pl.when(i == 0)
    def _():
        _row_gather(pos_ref, 0, y_hbm, ybuf.at[0], sem.at[0], rows, wait=False)

    def run_tile(prefetch_next):
        _row_gather(pos_ref, i * rows, y_hbm, ybuf.at[slot], sem.at[slot], rows, wait=True)
        if prefetch_next:
            _row_gather(pos_ref, (i + 1) * rows, y_hbm, ybuf.at[1 - slot], sem.at[1 - slot], rows, wait=False)
        hh = h_ref[...]
        w = w_ref[...]
        for kk in range(TOP_K):
            hh = hh + w[:, kk:kk + 1] * ybuf[slot, kk * tc:(kk + 1) * tc, :]
        o_ref[...] = hh * lax.rsqrt(jnp.mean(hh * hh, axis=-1, keepdims=True) + RMS_EPS) * g_ref[...]

    pl.when(i + 1 < n)(functools.partial(run_tile, True))
    pl.when(i + 1 == n)(functools.partial(run_tile, False))


def _combine(h, y_buf, pos, wts, g_final, tc=128):
    n, d = h.shape
    return pl.pallas_call(
        functools.partial(_combine_kernel, tc=tc),
        out_shape=jax.ShapeDtypeStruct((n, d), F32),
        grid_spec=pltpu.PrefetchScalarGridSpec(
            num_scalar_prefetch=1,
            grid=(n // tc,),
            in_specs=[pl.BlockSpec((tc, d), lambda i, pos: (i, 0)),
                      pl.BlockSpec((1, d), lambda i, pos: (0, 0)),
                      pl.BlockSpec((tc, LANES), lambda i, pos: (i, 0)),
                      pl.BlockSpec(memory_space=pl.ANY)],
            out_specs=pl.BlockSpec((tc, d), lambda i, pos: (i, 0)),
            scratch_shapes=[pltpu.VMEM((2, TOP_K * tc, d), F32), pltpu.SemaphoreType.DMA((2,))]),
        compiler_params=_cparams(("arbitrary",), 40),
        name="moe_combine_final_norm",
    )(pos, h, g_final.reshape(1, d).astype(F32), wts, y_buf)


def _dispatch_plan(ids, counts, blk):
    n_tok = ids.shape[0]
    n_experts = counts.shape[0]
    n_slot = n_tok * TOP_K
    counts = counts.astype(jnp.int32)
    padded = (counts + blk - 1) // blk * blk
    pad_end = jnp.cumsum(padded)
    pad_start = pad_end - padded
    dest = pad_start[ids[:, :TOP_K]] + ids[:, TOP_K:2 * TOP_K]
    n_blk = -(-n_slot // blk) + n_experts
    blk_start = jnp.arange(n_blk, dtype=jnp.int32) * blk
    blk_expert = jnp.minimum(jnp.sum(pad_end[None, :] <= blk_start[:, None], axis=1), n_experts - 1).astype(jnp.int32)
    n_active = (pad_end[-1] // blk).astype(jnp.int32).reshape(1)
    return dest.astype(jnp.int32), blk_expert, n_active, n_blk


def _layer(x, g_mix, w_in, conv_w, a_log, dt_bias, norm_w, pe_k, w1_k, w2_k, pe_v, w1_v, w2_v,
           w_a, w_b, w_out, g_ffn, w_group, b_group, w_expert, b_expert, w_gate_up, w_down, g_out):
    batch, seq, dm = x.shape
    n = batch * seq
    d = HEAD_DIM
    gdn_w = w_a.shape[0]
    nsa_w = w_b.shape[0]
    heads = gdn_w // d
    nsa_heads = nsa_w // d
    groups = NSA_GROUPS
    hpg = nsa_heads // groups
    kvw = groups * d

    sizes = (3 * gdn_w, gdn_w, heads, heads, nsa_w, 6 * kvw, 3 * nsa_heads, 2 * dm)
    assert sum(sizes) == w_in.shape[1]
    offs = np.concatenate([[0], np.cumsum(sizes)])
    seg = lambda i: w_in[:, offs[i]:offs[i + 1]]
    w_big = jnp.concatenate([seg(0), seg(1), seg(4), seg(5), seg(7)], axis=1).astype(BF16)
    w_small = jnp.concatenate([seg(2), seg(3), seg(6)], axis=1).astype(BF16)
    z_col = 3 * heads
    nq_col = z_col + heads
    kv_col = nq_col + nsa_heads
    gate_col0 = (kv_col + 6 * groups) * d

    x2 = x.reshape(n, dm)
    xn = _rmsnorm(x2, g_mix, BF16)
    proj = _matmul(xn, w_big, BF16, tm=1024, tn=1024, name="in_proj")
    small = _matmul(xn, w_small, F32, tm=1024, tn=w_small.shape[1], name="in_proj_small")
    a_raw, b_raw, nsa_gate = small[:, :heads], small[:, heads:2 * heads], small[:, 2 * heads:]

    gc, beta, gl = _gdn_gates(a_raw, b_raw, a_log, dt_bias, GDN_CHUNK)
    gct = gc.reshape(batch, seq, heads).transpose(0, 2, 1).reshape(batch * heads, 1, seq)
    o_gdn = _gdn_mixer(proj, conv_w, gc, beta, gl, gct, norm_w, batch, seq, heads, GDN_CHUNK)

    k_cmp, v_cmp = _nsa_compress(proj, kv_col, kv_col + groups, pe_k, w1_k, w2_k, pe_v, w1_v, w2_v,
                                 batch, seq, groups)
    gates = nsa_gate.reshape(batch, seq, groups, 3 * hpg).transpose(0, 2, 1, 3).reshape(batch * groups, seq, 3 * hpg)
    o_nsa = _nsa_mixer(proj, nq_col, kv_col + 2 * groups, kv_col + 3 * groups, kv_col + 4 * groups,
                       kv_col + 5 * groups, k_cmp, v_cmp, gates, batch, seq, groups, hpg)

    merged = _merge(o_gdn, o_nsa, w_a.astype(BF16), w_b.astype(BF16), proj, gate_col0)
    h = _resid_matmul(merged, w_out.astype(BF16), x2)

    xn2, ids, wts, counts = _router(h, g_ffn, w_group, b_group, w_expert, b_expert)
    dest, blk_expert, n_active, n_blk = _dispatch_plan(ids, counts, MOE_BLOCK)
    tok_buf = _slot_scatter(dest.reshape(n * TOP_K), n_blk * MOE_BLOCK, TOP_K)
    y_buf = _experts(xn2, w_gate_up, w_down, blk_expert, tok_buf, n_active, MOE_BLOCK)
    tc = 128
    pos_tiles = dest.reshape(n // tc, tc, TOP_K).transpose(0, 2, 1).reshape(n * TOP_K)
    out = _combine(h, y_buf, pos_tiles, wts, g_out, tc)
    return out.reshape(batch, seq, dm)


def kernel(x, g_mix, w_in, gdn_conv_w, gdn_a_log, gdn_dt_bias, gdn_norm_w, cmp_pe_k, cmp_w1_k, cmp_w2_k,
           cmp_pe_v, cmp_w1_v, cmp_w2_v, w_branch_gdn, w_branch_nsa, w_out, g_ffn, w_group, b_group,
           w_expert, b_expert, w_gate_up, w_down, g_final):
    depth = g_mix.shape[0]
    assert depth == 1, "the fused final norm assumes a single layer"
    return _layer(x, g_mix[0], w_in[0], gdn_conv_w[0], gdn_a_log[0], gdn_dt_bias[0], gdn_norm_w[0],
                  cmp_pe_k[0], cmp_w1_k[0], cmp_w2_k[0], cmp_pe_v[0], cmp_w1_v[0], cmp_w2_v[0],
                  w_branch_gdn[0], w_branch_nsa[0], w_out[0], g_ffn[0], w_group[0], b_group[0],
                  w_expert[0], b_expert[0], w_gate_up[0], w_down[0], g_final)
```

```python
import functools
import math

import numpy as np
import jax
import jax.numpy as jnp
from jax import lax
from jax.experimental import pallas as pl
from jax.experimental.pallas import tpu as pltpu

F32 = jnp.float32
BF16 = jnp.bfloat16

RMS_EPS = 1e-6
NEG_INF = -1e30
FORCE_SCORE = 1e6

HEAD_DIM = 128
GDN_CHUNK = 64
NSA_GROUPS = 4
CMP_BLOCK = 32
CMP_STRIDE = 16
SEL_BLOCK = 64
SEL_TOP_N = 16
WINDOW = 512
N_GROUPS = 8
EXPERTS_PER_GROUP = 8
TOP_K = 2
MOE_BLOCK = 128
LANES = 128

V7X_VMEM_BYTES = 64 * 1024 * 1024


def _cparams(semantics, vmem_mb):
    assert vmem_mb * 1024 * 1024 < V7X_VMEM_BYTES
    return pltpu.CompilerParams(dimension_semantics=semantics, vmem_limit_bytes=vmem_mb * 1024 * 1024)


def _dot(a, b, **kw):
    return jnp.dot(a, b, preferred_element_type=F32, **kw)


def _dot_nt(a, b):
    return lax.dot_general(a, b, (((1,), (1,)), ((), ())), preferred_element_type=F32)


def _dot_tn(a, b):
    return lax.dot_general(a, b, (((0,), (0,)), ((), ())), preferred_element_type=F32)


def _sigmoid(x):
    return 1.0 / (1.0 + jnp.exp(-x))


def _silu(x):
    return x * _sigmoid(x)


def _rmsnorm_kernel(x_ref, g_ref, o_ref):
    x = x_ref[...].astype(F32)
    ms = jnp.mean(x * x, axis=-1, keepdims=True)
    o_ref[...] = (x * lax.rsqrt(ms + RMS_EPS) * g_ref[...]).astype(o_ref.dtype)


def _rmsnorm(x, gain, out_dtype, tm=512):
    n, d = x.shape
    return pl.pallas_call(
        _rmsnorm_kernel,
        out_shape=jax.ShapeDtypeStruct((n, d), out_dtype),
        grid=(n // tm,),
        in_specs=[pl.BlockSpec((tm, d), lambda i: (i, 0)), pl.BlockSpec((1, d), lambda i: (0, 0))],
        out_specs=pl.BlockSpec((tm, d), lambda i: (i, 0)),
        compiler_params=_cparams(("parallel",), 40),
        name="rmsnorm",
    )(x, gain.reshape(1, d).astype(F32))


def _mm_kernel(x_ref, w_ref, o_ref):
    o_ref[...] = _dot(x_ref[...], w_ref[...]).astype(o_ref.dtype)


def _matmul(x, w, out_dtype, tm, tn, name):
    m, k = x.shape
    n = w.shape[1]
    return pl.pallas_call(
        _mm_kernel,
        out_shape=jax.ShapeDtypeStruct((m, n), out_dtype),
        grid=(m // tm, n // tn),
        in_specs=[pl.BlockSpec((tm, k), lambda i, j: (i, 0)), pl.BlockSpec((k, tn), lambda i, j: (0, j))],
        out_specs=pl.BlockSpec((tm, tn), lambda i, j: (i, j)),
        compiler_params=_cparams(("parallel", "arbitrary"), 48),
        name=name,
    )(x, w)


def _mm_cast_kernel(x_ref, w_ref, *refs, n_riders):
    o_ref = refs[n_riders]
    o_ref[...] = _dot(x_ref[...], w_ref[...]).astype(o_ref.dtype)
    for r in range(n_riders):
        refs[n_riders + 1 + r][...] = refs[r][...].astype(BF16)


def _rider_rows(n_rows, max_blocks):
    rb = 16
    while n_rows % rb or n_rows // rb > max_blocks:
        rb += 16
    return rb


def _matmul_with_casts(x, w, out_dtype, tm, tn, name, riders):
    m, k = x.shape
    n = w.shape[1]
    ni, nj = m // tm, n // tn
    in_specs = [pl.BlockSpec((tm, k), lambda i, j: (i, 0)), pl.BlockSpec((k, tn), lambda i, j: (0, j))]
    out_specs = [pl.BlockSpec((tm, tn), lambda i, j: (i, j))]
    out_shape = [jax.ShapeDtypeStruct((m, n), out_dtype)]
    rider_specs = []
    for r in riders:
        rows, cols = r.shape
        rb = _rider_rows(rows, ni * nj)
        last = rows // rb - 1
        spec = pl.BlockSpec((rb, cols), lambda i, j, last=last: (jnp.minimum(i * nj + j, last), 0))
        rider_specs.append(spec)
        out_shape.append(jax.ShapeDtypeStruct((rows, cols), BF16))
    outs = pl.pallas_call(
        functools.partial(_mm_cast_kernel, n_riders=len(riders)),
        out_shape=tuple(out_shape),
        grid=(ni, nj),
        in_specs=in_specs[:2] + rider_specs,
        out_specs=tuple(out_specs + rider_specs),
        compiler_params=_cparams(("arbitrary", "arbitrary"), 48),
        name=name,
    )(x, w, *riders)
    return outs


def _merge_kernel(a_ref, b_ref, wa_ref, wb_ref, ga_ref, gb_ref, o_ref):
    ya = _dot(a_ref[...], wa_ref[...])
    yb = _dot(b_ref[...], wb_ref[...])
    o_ref[...] = (_sigmoid(ga_ref[...].astype(F32)) * ya + _sigmoid(gb_ref[...].astype(F32)) * yb).astype(o_ref.dtype)


def _merge(o_gdn, o_nsa, w_a, w_b, proj, gate_col0, tm=1024, tn=512):
    n, ka = o_gdn.shape
    kb = o_nsa.shape[1]
    d = w_a.shape[1]
    ja, jb = gate_col0 // tn, (gate_col0 + d) // tn
    return pl.pallas_call(
        _merge_kernel,
        out_shape=jax.ShapeDtypeStruct((n, d), BF16),
        grid=(n // tm, d // tn),
        in_specs=[pl.BlockSpec((tm, ka), lambda i, j: (i, 0)),
                  pl.BlockSpec((tm, kb), lambda i, j: (i, 0)),
                  pl.BlockSpec((ka, tn), lambda i, j: (0, j)),
                  pl.BlockSpec((kb, tn), lambda i, j: (0, j)),
                  pl.BlockSpec((tm, tn), lambda i, j: (i, ja + j)),
                  pl.BlockSpec((tm, tn), lambda i, j: (i, jb + j))],
        out_specs=pl.BlockSpec((tm, tn), lambda i, j: (i, j)),
        compiler_params=_cparams(("parallel", "arbitrary"), 48),
        name="branch_merge",
    )(o_gdn, o_nsa, w_a, w_b, proj, proj)


def _resid_mm_kernel(x_ref, w_ref, r_ref, o_ref):
    o_ref[...] = r_ref[...] + _dot(x_ref[...], w_ref[...])


def _resid_matmul(x, w, resid, tm=1024, tn=512):
    m, k = x.shape
    n = w.shape[1]
    return pl.pallas_call(
        _resid_mm_kernel,
        out_shape=jax.ShapeDtypeStruct((m, n), F32),
        grid=(m // tm, n // tn),
        in_specs=[pl.BlockSpec((tm, k), lambda i, j: (i, 0)),
                  pl.BlockSpec((k, tn), lambda i, j: (0, j)),
                  pl.BlockSpec((tm, tn), lambda i, j: (i, j))],
        out_specs=pl.BlockSpec((tm, tn), lambda i, j: (i, j)),
        compiler_params=_cparams(("parallel", "arbitrary"), 48),
        name="out_proj_residual",
    )(x, w, resid)


def _gdn_gate_kernel(a_ref, b_ref, alog_ref, dtb_ref, gc_ref, beta_ref, gl_ref, *, chunk):
    tm = a_ref.shape[0]
    x = a_ref[...] + dtb_ref[...]
    softplus = jnp.maximum(x, 0.0) + jnp.log(1.0 + jnp.exp(-jnp.abs(x)))
    g = -jnp.exp(alog_ref[...]) * softplus
    row = lax.broadcasted_iota(jnp.int32, (tm, tm), 0)
    col = lax.broadcasted_iota(jnp.int32, (tm, tm), 1)
    same = row // chunk == col // chunk
    tri = jnp.where((col <= row) & same, 1.0, 0.0).astype(F32)
    gc_ref[...] = _dot(tri, g, precision=lax.Precision.HIGHEST)
    gl_ref[...] = _dot(jnp.where(same, 1.0, 0.0).astype(F32), g, precision=lax.Precision.HIGHEST)
    beta_ref[...] = _sigmoid(b_ref[...])


def _gdn_gates(a_raw, b_raw, a_log, dt_bias, chunk, tm=512):
    n, h = a_raw.shape
    spec = pl.BlockSpec((tm, h), lambda i: (i, 0))
    vec = pl.BlockSpec((1, h), lambda i: (0, 0))
    sds = jax.ShapeDtypeStruct((n, h), F32)
    return pl.pallas_call(
        functools.partial(_gdn_gate_kernel, chunk=chunk),
        out_shape=(sds, sds, sds),
        grid=(n // tm,),
        in_specs=[spec, spec, vec, vec],
        out_specs=(spec, spec, spec),
        compiler_params=_cparams(("parallel",), 32),
        name="gdn_gates",
    )(a_raw, b_raw, a_log.reshape(1, h).astype(F32), dt_bias.reshape(1, h).astype(F32))


_CONV_PAD = 8


def _gdn_kernel(q_ref, k_ref, v_ref, z_ref, cwq_ref, cwk_ref, cwv_ref, gc_ref, beta_ref, gl_ref, gct_ref, nw_ref,
                o_ref, state, xbuf, *, chunk, hps):
    hblk = pl.program_id(1)
    t = pl.program_id(2)
    tt = q_ref.shape[0]
    d = HEAD_DIM
    width = cwq_ref.shape[0]
    c = chunk
    n_ch = tt // c
    n_lvl = int(math.log2(c))
    assert 2 ** n_lvl == c

    @pl.when(t == 0)
    def _():
        state[...] = jnp.zeros_like(state)
        xbuf[:, 0:_CONV_PAD, :] = jnp.zeros((3 * hps, _CONV_PAD, d), F32)

    def conv_silu(s, x_bf16, w):
        xbuf[s, _CONV_PAD:_CONV_PAD + tt, :] = x_bf16.astype(F32)
        y = None
        for j in range(width):
            term = xbuf[s, pl.ds(_CONV_PAD - (width - 1) + j, tt), :] * w[j:j + 1, :]
            y = term if y is None else y + term
        xbuf[s, 0:_CONV_PAD, :] = xbuf[s, tt:tt + _CONV_PAD, :]
        return _silu(y)

    def l2n(x):
        return x * lax.rsqrt(jnp.sum(x * x, axis=-1, keepdims=True) + RMS_EPS)

    ri = lax.broadcasted_iota(jnp.int32, (tt, tt), 0)
    ci = lax.broadcasted_iota(jnp.int32, (tt, tt), 1)
    incl = (ri >= ci) & (ri // c == ci // c)
    strict = ri > ci
    lane = lax.broadcasted_iota(jnp.int32, gc_ref.shape, 1)
    nw = nw_ref[...]

    hd = [dict() for _ in range(hps)]
    for hp, e in enumerate(hd):
        h = hblk * hps + hp
        hs = slice(hp * d, (hp + 1) * d)
        q = l2n(conv_silu(3 * hp, q_ref[:, hs], cwq_ref[:, hs])) * (d ** -0.5)
        k = l2n(conv_silu(3 * hp + 1, k_ref[:, hs], cwk_ref[:, hs]))
        v = conv_silu(3 * hp + 2, v_ref[:, hs], cwv_ref[:, hs])
        pick = lambda ref: jnp.sum(jnp.where(lane == h, ref[...], 0.0), axis=1, keepdims=True)
        gcc, bc, glc = pick(gc_ref), pick(beta_ref), pick(gl_ref)
        egc = jnp.exp(gcc)
        kb = k.astype(BF16)
        qk = _dot_nt(jnp.concatenate([q.astype(BF16), kb], axis=0), kb)
        dm = jnp.exp(jnp.where(incl, gcc - gct_ref[hp], NEG_INF))
        e["npow"] = jnp.where(strict, -(qk[tt:] * dm * bc), 0.0)
        e["x"] = jnp.concatenate([v * bc, k * (bc * egc)], axis=1)
        e["attn"] = (qk[:tt] * dm).astype(BF16)
        e["qe"] = q * egc
        e["kout"] = (k * jnp.exp(glc - gcc)).astype(BF16)
        e["dch"] = jnp.exp(glc)

    for lvl in range(n_lvl):
        for e in hd:
            nb = e["npow"].astype(BF16)
            if lvl + 1 < n_lvl:
                r = _dot(nb, jnp.concatenate([nb, e["x"].astype(BF16)], axis=1))
                e["npow"], e["x"] = r[:, :tt], e["x"] + r[:, tt:]
            else:
                e["x"] = e["x"] + _dot(nb, e["x"].astype(BF16))

    for hp, e in enumerate(hd):
        e["xb"] = e["x"].astype(BF16)
        au = _dot(e["attn"], e["xb"])
        e["o0"] = au[:, :d]
        e["qe"] = (e["qe"] - au[:, d:]).astype(BF16)
        e["s"] = state[hp]

    for ch in range(n_ch):
        sl = slice(ch * c, (ch + 1) * c)
        for hp, e in enumerate(hd):
            hs = slice(hp * d, (hp + 1) * d)
            ktx = _dot_tn(e["kout"][sl], e["xb"][sl])
            sb = e["s"].astype(BF16)
            o = _dot(e["qe"][sl], sb) + e["o0"][sl]
            e["s"] = e["s"] * e["dch"][ch * c:ch * c + 1, :] - _dot(ktx[:, d:].astype(BF16), sb) + ktx[:, :d]
            on = o * lax.rsqrt(jnp.mean(o * o, axis=-1, keepdims=True) + RMS_EPS) * nw
            o_ref[sl, hs] = (on * _silu(z_ref[sl, hs].astype(F32))).astype(o_ref.dtype)

    for hp, e in enumerate(hd):
        state[hp] = e["s"]


def _gdn_mixer(proj, conv_w, gc, beta, gl, gct, norm_w, batch, seq, heads, chunk, tt=256, hps=4):
    d = HEAD_DIM
    n = batch * seq
    nt = seq // tt
    hb = heads // hps
    width = conv_w.shape[0]
    assert width - 1 <= _CONV_PAD and tt % chunk == 0 and seq % tt == 0 and heads % hps == 0

    def slab(off):
        return pl.BlockSpec((tt, hps * d), lambda b, h, t: (b * nt + t, off + h))

    def cw(off):
        return pl.BlockSpec((width, hps * d), lambda b, h, t: (0, off + h))

    gspec = pl.BlockSpec((tt, heads), lambda b, h, t: (b * nt + t, 0))
    return pl.pallas_call(
        functools.partial(_gdn_kernel, chunk=chunk, hps=hps),
        out_shape=jax.ShapeDtypeStruct((n, heads * d), BF16),
        grid=(batch, hb, nt),
        in_specs=[slab(0), slab(hb), slab(2 * hb), slab(3 * hb),
                  cw(0), cw(hb), cw(2 * hb),
                  gspec, gspec, gspec,
                  pl.BlockSpec((hps, 1, tt), lambda b, h, t: (b * hb + h, 0, t)),
                  pl.BlockSpec((1, d), lambda b, h, t: (0, 0))],
        out_specs=pl.BlockSpec((tt, hps * d), lambda b, h, t: (b * nt + t, h)),
        scratch_shapes=[pltpu.VMEM((hps, d, d), F32), pltpu.VMEM((3 * hps, tt + _CONV_PAD, d), F32)],
        compiler_params=_cparams(("parallel", "parallel", "arbitrary"), 40),
        name="gdn_mixer",
    )(proj, proj, proj, proj, conv_w, conv_w, conv_w, gc, beta, gl, gct, norm_w.reshape(1, d).astype(F32))


def _nsa_cmp_kernel(kc_ref, vc_ref, pek_ref, w1k_ref, w2k_ref, pev_ref, w1v_ref, w2v_ref,
                    ko_ref, vo_ref, xf, sh, *, stride, block):
    seq, d = kc_ref.shape
    nh = seq // stride
    reps = block // stride
    assert nh % 8 == 0

    def compress(src_ref, pe_ref, w1_ref, w2_ref, dst_ref):
        xf[...] = src_ref[...].astype(F32)
        pe = pe_ref[...]
        hid = None
        for r in range(reps):
            acc = None
            for l in range(stride):
                li = r * stride + l
                rows = xf[pl.ds(l, nh, stride=stride), :] + pe[li:li + 1, :]
                term = _dot(rows.astype(BF16), w1_ref[li].astype(BF16))
                acc = term if acc is None else acc + term
            if r == 0:
                hid = acc
            else:
                sh[0:nh, :] = acc
                sh[nh:nh + 8, :] = jnp.zeros((8, d), F32)
                hid = hid + sh[pl.ds(r, nh), :]
        hid = _silu(hid)
        dst_ref[0] = _dot(hid.astype(BF16), w2_ref[...].astype(BF16)).astype(dst_ref.dtype)

    compress(kc_ref, pek_ref, w1k_ref, w2k_ref, ko_ref)
    compress(vc_ref, pev_ref, w1v_ref, w2v_ref, vo_ref)


def _nsa_compress(proj, kc_col, vc_col, pe_k, w1_k, w2_k, pe_v, w1_v, w2_v, batch, seq, groups):
    d = HEAD_DIM
    nh = seq // CMP_STRIDE
    full2 = lambda b, g: (0, 0)
    full3 = lambda b, g: (0, 0, 0)
    out_sds = jax.ShapeDtypeStruct((batch * groups, nh, d), BF16)
    out_spec = pl.BlockSpec((1, nh, d), lambda b, g: (b * groups + g, 0, 0))
    return pl.pallas_call(
        functools.partial(_nsa_cmp_kernel, stride=CMP_STRIDE, block=CMP_BLOCK),
        out_shape=(out_sds, out_sds),
        grid=(batch, groups),
        in_specs=[pl.BlockSpec((seq, d), lambda b, g: (b, kc_col + g)),
                  pl.BlockSpec((seq, d), lambda b, g: (b, vc_col + g)),
                  pl.BlockSpec((CMP_BLOCK, d), full2), pl.BlockSpec((CMP_BLOCK, d, d), full3), pl.BlockSpec((d, d), full2),
                  pl.BlockSpec((CMP_BLOCK, d), full2), pl.BlockSpec((CMP_BLOCK, d, d), full3), pl.BlockSpec((d, d), full2)],
        out_specs=(out_spec, out_spec),
        scratch_shapes=[pltpu.VMEM((seq, d), F32), pltpu.VMEM((nh + 8, d), F32)],
        compiler_params=_cparams(("parallel", "parallel"), 32),
        name="nsa_compress",
    )(proj, proj, pe_k, w1_k, w2_k, pe_v, w1_v, w2_v)


def _nsa_kernel(q_ref, kc_ref, vc_ref, ks_ref, vs_ref, kw_ref, vw_ref, gate_ref, selmap_ref, expand_ref,
                o_ref, score_t, *, hpg, gps, tk, n_sel, n_top):
    tq = q_ref.shape[0]
    d = HEAD_DIM
    gw = hpg * d
    n_cmp_pad = kc_ref.shape[1]
    t0 = pl.program_id(2) * tq
    rows = hpg * tq
    grp = list(range(gps))
    kv = lambda ref, gi, r0, n: ref[pl.ds(r0, n), gi * d:(gi + 1) * d]

    q4 = []
    for gi in grp:
        qg = jnp.concatenate([q_ref[:, gi * gw + hh * d:gi * gw + (hh + 1) * d] for hh in range(hpg)], axis=0)
        q4.append((qg.astype(F32) * (d ** -0.5 * math.log2(math.e))).astype(BF16))
    trow = t0 + lax.broadcasted_iota(jnp.int32, (tq, 1), 0)

    ncol = lax.broadcasted_iota(jnp.int32, (1, n_cmp_pad), 1)
    vis_c = ((ncol * CMP_STRIDE + (CMP_BLOCK - 1)) <= trow)[None]
    s_c = [_dot_nt(q4[gi], kc_ref[gi]).reshape(hpg, tq, n_cmp_pad) for gi in grp]
    p_c = []
    for gi in grp:
        s3 = jnp.where(vis_c, s_c[gi], NEG_INF)
        e = jnp.where(vis_c, jnp.exp2(s3 - jnp.max(s3, axis=-1, keepdims=True)), 0.0)
        den = jnp.sum(e, axis=-1, keepdims=True)
        p_c.append(e / jnp.where(den > 0.0, den, 1.0))
    o_cmp = [_dot(p_c[gi].reshape(rows, n_cmp_pad).astype(BF16), vc_ref[gi]).reshape(hpg, tq, d) for gi in grp]

    jrow = lax.broadcasted_iota(jnp.int32, (n_sel, 1), 0)
    tcol = t0 + lax.broadcasted_iota(jnp.int32, (1, tq), 1)
    cur = tcol // SEL_BLOCK
    forced = (jrow == 0) | (jrow == cur) | (jrow == cur - 1)
    causal_blk = jrow * SEL_BLOCK <= tcol
    keys = []
    for gi in grp:
        imp_t = lax.dot_general(selmap_ref[...], jnp.sum(p_c[gi], axis=0), (((1,), (1,)), ((), ())),
                                preferred_element_type=F32, precision=lax.Precision.HIGHEST)
        sc = jnp.where(forced, FORCE_SCORE, jnp.where(causal_blk, imp_t, NEG_INF))
        bits = pltpu.bitcast(sc, jnp.int32)
        keys.append(bits ^ ((bits >> 31) & jnp.int32(0x7FFFFFFF)))
        score_t[gi] = keys[gi]

    def rank_step(i, cnts):
        tie = jnp.where(jrow > i, 1, 0)
        return tuple(cnts[gi] + jnp.where(score_t[gi, pl.ds(i, 1), :] + tie > keys[gi], 1.0, 0.0) for gi in grp)

    n_rank = jnp.minimum((t0 + tq - 1) // SEL_BLOCK + 1, n_sel)
    cnts = lax.fori_loop(0, n_rank, rank_step, tuple(jnp.zeros((n_sel, tq), F32) for _ in grp))
    eye = jnp.where(lax.broadcasted_iota(jnp.int32, (n_sel, n_sel), 0)
                    == lax.broadcasted_iota(jnp.int32, (n_sel, n_sel), 1), 1.0, 0.0).astype(BF16)
    sel_bias = [_dot_tn(jnp.where(cnts[gi] < n_top, 0.0, NEG_INF).astype(BF16), eye).astype(BF16) for gi in grp]

    def sel_step(kt, carry, diagonal):
        k0 = pl.multiple_of(kt * tk, tk)
        ex = expand_ref[:, pl.ds(k0, tk)]
        s3 = []
        for gi in grp:
            b = _dot(sel_bias[gi], ex)
            if diagonal:
                kpos = k0 + lax.broadcasted_iota(jnp.int32, (1, tk), 1)
                b = jnp.where(kpos <= trow, b, NEG_INF)
            s3.append(_dot_nt(q4[gi], kv(ks_ref, gi, k0, tk)).reshape(hpg, tq, tk) + b[None])
        out = []
        for gi in grp:
            m, acc = carry[gi]
            m_new = jnp.maximum(m, jnp.max(s3[gi], axis=-1, keepdims=True))
            alpha = jnp.exp2(m - m_new)
            e = jnp.exp2((s3[gi] - m_new).astype(BF16))
            pv = _dot(e.reshape(rows, tk), with_ones(kv(vs_ref, gi, k0, tk)))
            out.append((m_new, alpha * acc + pv.reshape(hpg, tq, 2 * d)))
        return tuple(out)

    def with_ones(v_tile):
        return jnp.concatenate([v_tile, jnp.ones(v_tile.shape, BF16)], axis=1)

    n_full = t0 // tk
    init = (jnp.full((hpg, tq, 1), NEG_INF, F32), jnp.zeros((hpg, tq, 2 * d), F32))
    carry = lax.fori_loop(0, n_full, functools.partial(sel_step, diagonal=False), tuple(init for _ in grp))
    carry = sel_step(n_full, carry, diagonal=True)
    o_sel = [carry[gi][1][:, :, :d] / carry[gi][1][:, :, d:] for gi in grp]

    span = WINDOW + tq
    ws = pl.multiple_of(jnp.maximum(t0 - WINDOW, 0), tq)
    kpos = ws + lax.broadcasted_iota(jnp.int32, (1, span), 1)
    rel = trow - kpos
    b_w = jnp.where((rel >= 0) & (rel < WINDOW), 0.0, NEG_INF)[None]
    s_w = [_dot_nt(q4[gi], kv(kw_ref, gi, ws, span)).reshape(hpg, tq, span) + b_w for gi in grp]
    o_win = []
    for gi in grp:
        e = jnp.exp2((s_w[gi] - jnp.max(s_w[gi], axis=-1, keepdims=True)).astype(BF16))
        pv = _dot(e.reshape(rows, span), with_ones(kv(vw_ref, gi, ws, span))).reshape(hpg, tq, 2 * d)
        o_win.append(pv[:, :, :d] / pv[:, :, d:])

    for gi in grp:
        gates = _sigmoid(gate_ref[gi])
        for hh in range(hpg):
            g0 = gates[:, 3 * hh:3 * hh + 1]
            g1 = gates[:, 3 * hh + 1:3 * hh + 2]
            g2 = gates[:, 3 * hh + 2:3 * hh + 3]
            o_ref[:, gi * gw + hh * d:gi * gw + (hh + 1) * d] = (
                g0 * o_cmp[gi][hh] + g1 * o_sel[gi][hh] + g2 * o_win[gi][hh]).astype(o_ref.dtype)


def _nsa_mixer(proj, q_col, ks_col, vs_col, kw_col, vw_col, k_cmp, v_cmp, gates, batch, seq, groups, hpg,
               tq=128, tk=1024, gps=2):
    d = HEAD_DIM
    n = batch * seq
    nq = seq // tq
    n_sel = seq // SEL_BLOCK
    n_top = min(SEL_TOP_N, n_sel)
    n_cmp = (seq - CMP_BLOCK) // CMP_STRIDE + 1
    n_cmp_pad = k_cmp.shape[1]
    assert n_sel % 8 == 0 and seq >= WINDOW + tq and WINDOW % tq == 0 and tk % tq == 0 and seq % tk == 0

    cmp_start = np.arange(n_cmp_pad) * CMP_STRIDE
    sel_start = np.arange(n_sel) * SEL_BLOCK
    overlap = (np.minimum(cmp_start[None, :] + CMP_BLOCK, sel_start[:, None] + SEL_BLOCK)
               - np.maximum(cmp_start[None, :], sel_start[:, None]))
    sel_map_t = np.clip(overlap, 0, None) / CMP_STRIDE
    sel_map_t[:, n_cmp:] = 0.0
    expand = (np.arange(seq)[None, :] // SEL_BLOCK == np.arange(n_sel)[:, None]).astype(np.float32)

    gb = groups // gps
    qw = gps * hpg * d
    assert groups % gps == 0 and all(c % gps == 0 for c in (ks_col, vs_col, kw_col, vw_col)) and q_col % (gps * hpg) == 0

    def kv(col):
        return pl.BlockSpec((seq, gps * d), lambda b, g, t: (b, col // gps + g))

    cmp_spec = pl.BlockSpec((gps, n_cmp_pad, d), lambda b, g, t: (b * gb + g, 0, 0))
    return pl.pallas_call(
        functools.partial(_nsa_kernel, hpg=hpg, gps=gps, tk=tk, n_sel=n_sel, n_top=n_top),
        out_shape=jax.ShapeDtypeStruct((n, groups * hpg * d), BF16),
        grid=(batch, gb, nq),
        in_specs=[pl.BlockSpec((tq, qw), lambda b, g, t: (b * nq + t, q_col // (gps * hpg) + g)),
                  cmp_spec, cmp_spec, kv(ks_col), kv(vs_col), kv(kw_col), kv(vw_col),
                  pl.BlockSpec((gps, tq, 3 * hpg), lambda b, g, t: (b * gb + g, t, 0)),
                  pl.BlockSpec((n_sel, n_cmp_pad), lambda b, g, t: (0, 0)),
                  pl.BlockSpec((n_sel, seq), lambda b, g, t: (0, 0))],
        out_specs=pl.BlockSpec((tq, qw), lambda b, g, t: (b * nq + t, g)),
        scratch_shapes=[pltpu.VMEM((gps, n_sel, tq), jnp.int32)],
        compiler_params=_cparams(("parallel", "parallel", "arbitrary"), 48),
        name="nsa_attention",
    )(proj, k_cmp, v_cmp, proj, proj, proj, proj, gates, jnp.asarray(sel_map_t, F32), jnp.asarray(expand, BF16))


def _router_kernel(h_ref, g_ref, wr_ref, br_ref, xn_ref, ids_ref, wts_ref, cnt_ref):
    x = h_ref[...]
    xn = x * lax.rsqrt(jnp.mean(x * x, axis=-1, keepdims=True) + RMS_EPS) * g_ref[...]
    xn_ref[...] = xn
    logits = _dot(xn, wr_ref[...], precision=lax.Precision.HIGHEST) + br_ref[...]
    lane = lax.broadcasted_iota(jnp.int32, logits.shape, 1)
    big = jnp.int32(2 * LANES)
    is_g = lane < N_GROUPS
    gl = jnp.where(is_g, logits, NEG_INF)
    gm = jnp.max(gl, axis=-1, keepdims=True)
    p_g = 1.0 / jnp.sum(jnp.where(is_g, jnp.exp(gl - gm), 0.0), axis=-1, keepdims=True)
    grp = jnp.min(jnp.where(gl == gm, lane, big), axis=-1, keepdims=True)
    eidx = lane - N_GROUPS
    in_grp = (eidx >= 0) & (eidx // EXPERTS_PER_GROUP == grp) & (eidx < N_GROUPS * EXPERTS_PER_GROUP)
    el = jnp.where(in_grp, logits, NEG_INF)
    em = jnp.max(el, axis=-1, keepdims=True)
    ee = jnp.where(in_grp, jnp.exp(el - em), 0.0)
    pe = ee / jnp.sum(ee, axis=-1, keepdims=True)
    p1 = jnp.max(jnp.where(in_grp, pe, -1.0), axis=-1, keepdims=True)
    i1 = jnp.min(jnp.where(in_grp & (pe == p1), lane, big), axis=-1, keepdims=True)
    rest = in_grp & (lane != i1)
    p2 = jnp.max(jnp.where(rest, pe, -1.0), axis=-1, keepdims=True)
    i2 = jnp.min(jnp.where(rest & (pe == p2), lane, big), axis=-1, keepdims=True)
    denom = p1 + p2
    wts_ref[...] = jnp.where(lane == 0, p_g * p1 / denom, p_g * p2 / denom)

    @pl.when(pl.program_id(0) == 0)
    def _():
        cnt_ref[...] = jnp.zeros_like(cnt_ref)

    tm = logits.shape[0]
    hot = jnp.where((lane == i1) | (lane == i2), 1.0, 0.0)
    below = jnp.where(lax.broadcasted_iota(jnp.int32, (tm, tm), 0) > lax.broadcasted_iota(jnp.int32, (tm, tm), 1),
                      1.0, 0.0).astype(BF16)
    before = _dot(below, hot.astype(BF16)) + cnt_ref[...]
    r1 = jnp.sum(jnp.where(lane == i1, before, 0.0), axis=-1, keepdims=True).astype(jnp.int32)
    r2 = jnp.sum(jnp.where(lane == i2, before, 0.0), axis=-1, keepdims=True).astype(jnp.int32)
    cnt_ref[...] += jnp.sum(hot, axis=0, keepdims=True)
    ids_ref[...] = jnp.where(lane == 0, i1 - N_GROUPS, jnp.where(lane == 1, i2 - N_GROUPS, jnp.where(lane == 2, r1, r2)))


def _router(h, g_ffn, w_group, b_group, w_expert, b_expert, tm=256):
    n, d = h.shape
    n_g, n_e = w_group.shape[1], w_expert.shape[1]
    n_r = n_g + n_e
    assert n_r <= LANES and TOP_K == 2 and n_g == N_GROUPS
    wr = jnp.concatenate([w_group, w_expert, jnp.zeros((d, LANES - n_r), F32)], axis=1).astype(F32)
    br = jnp.concatenate([b_group, b_expert, jnp.zeros((LANES - n_r,), F32)]).reshape(1, LANES).astype(F32)
    row = lambda i: (i, 0)
    fixed = lambda i: (0, 0)
    xn, ids, wts, cnt = pl.pallas_call(
        _router_kernel,
        out_shape=(jax.ShapeDtypeStruct((n, d), F32), jax.ShapeDtypeStruct((n, LANES), jnp.int32),
                   jax.ShapeDtypeStruct((n, LANES), F32), jax.ShapeDtypeStruct((1, LANES), F32)),
        grid=(n // tm,),
        in_specs=[pl.BlockSpec((tm, d), row), pl.BlockSpec((1, d), fixed),
                  pl.BlockSpec((d, LANES), fixed), pl.BlockSpec((1, LANES), fixed)],
        out_specs=(pl.BlockSpec((tm, d), row), pl.BlockSpec((tm, LANES), row), pl.BlockSpec((tm, LANES), row),
                   pl.BlockSpec((1, LANES), fixed)),
        compiler_params=_cparams(("arbitrary",), 40),
        name="moe_router",
    )(h, g_ffn.reshape(1, d).astype(F32), wr, br)
    return xn, ids, wts, cnt[0, n_g:n_r]


def _row_gather(idx_ref, base, src_hbm, dst, sem, n_rows, wait):
    for r in range(n_rows):
        cp = pltpu.make_async_copy(src_hbm.at[pl.ds(idx_ref[base + r], 1), :], dst.at[pl.ds(r, 1), :], sem)
        if wait:
            cp.wait()
        else:
            cp.start()


def _slot_scatter_kernel(dest_ref, tok_ref, *, top_k, unroll):
    n_out = tok_ref.shape[0]
    n_slot = dest_ref.shape[0]

    def clear(j, carry):
        for u in range(unroll):
            tok_ref[j * unroll + u] = 0
        return carry

    def place(j, carry):
        rows = [dest_ref[j * unroll + u] for u in range(unroll)]
        for u in range(unroll):
            tok_ref[rows[u]] = (j * unroll + u) // top_k
        return carry

    lax.fori_loop(0, n_out // unroll, clear, 0)
    lax.fori_loop(0, n_slot // unroll, place, 0)


def _slot_scatter(dest_flat, n_out, top_k, unroll=8):
    assert n_out % unroll == 0 and dest_flat.shape[0] % unroll == 0
    return pl.pallas_call(
        functools.partial(_slot_scatter_kernel, top_k=top_k, unroll=unroll),
        out_shape=jax.ShapeDtypeStruct((n_out,), jnp.int32),
        in_specs=[pl.BlockSpec(memory_space=pltpu.SMEM)],
        out_specs=pl.BlockSpec(memory_space=pltpu.SMEM),
        name="moe_slot_scatter",
    )(dest_flat)


def _expert_kernel(be_ref, tok_ref, nact_ref, x_hbm, wgu_ref, wd_ref, y_ref, xbuf, sem, *, blk, d_ff):
    i = pl.program_id(0)
    n_act = nact_ref[0]
    slot = i % 2

    @pl.when(i == 0)
    def _():
        _row_gather(tok_ref, 0, x_hbm, xbuf.at[0], sem.at[0], blk, wait=False)

    def run_block(prefetch_next):
        _row_gather(tok_ref, i * blk, x_hbm, xbuf.at[slot], sem.at[slot], blk, wait=True)
        if prefetch_next:
            _row_gather(tok_ref, (i + 1) * blk, x_hbm, xbuf.at[1 - slot], sem.at[1 - slot], blk, wait=False)
        xb = xbuf[slot].astype(BF16)
        gu = _dot(xb, wgu_ref[0])
        act = _silu(gu[:, :d_ff]) * gu[:, d_ff:]
        y_ref[...] = _dot(act.astype(BF16), wd_ref[0])

    pl.when(i + 1 < n_act)(functools.partial(run_block, True))
    pl.when(i + 1 == n_act)(functools.partial(run_block, False))

    @pl.when(i >= n_act)
    def _():
        y_ref[...] = jnp.zeros_like(y_ref)


def _experts(xn, w_gate_up, w_down, blk_expert, tok_buf, n_active, blk):
    n, d = xn.shape
    n_exp, _, ff2 = w_gate_up.shape
    d_ff = ff2 // 2
    n_blk = blk_expert.shape[0]
    return pl.pallas_call(
        functools.partial(_expert_kernel, blk=blk, d_ff=d_ff),
        out_shape=jax.ShapeDtypeStruct((n_blk * blk, d), F32),
        grid_spec=pltpu.PrefetchScalarGridSpec(
            num_scalar_prefetch=3,
            grid=(n_blk,),
            in_specs=[pl.BlockSpec(memory_space=pl.ANY),
                      pl.BlockSpec((1, d, ff2), lambda i, be, tok, na: (be[i], 0, 0)),
                      pl.BlockSpec((1, d_ff, d), lambda i, be, tok, na: (be[i], 0, 0))],
            out_specs=pl.BlockSpec((blk, d), lambda i, be, tok, na: (i, 0)),
            scratch_shapes=[pltpu.VMEM((2, blk, d), F32), pltpu.SemaphoreType.DMA((2,))]),
        compiler_params=_cparams(("arbitrary",), 56),
        name="moe_experts",
    )(blk_expert, tok_buf, n_active, xn, w_gate_up, w_down)


def _combine_kernel(pos_ref, h_ref, g_ref, w_ref, y_hbm, o_ref, ybuf, sem, *, tc):
    i = pl.program_id(0)
    n = pl.num_programs(0)
    slot = i % 2
    rows = TOP_K * tc

    @pl.when(i == 0)
    def _():
        _row_gather(pos_ref, 0, y_hbm, ybuf.at[0], sem.at[0], rows, wait=False)

    def run_tile(prefetch_next):
        _row_gather(pos_ref, i * rows, y_hbm, ybuf.at[slot], sem.at[slot], rows, wait=True)
        if prefetch_next:
            _row_gather(pos_ref, (i + 1) * rows, y_hbm, ybuf.at[1 - slot], sem.at[1 - slot], rows, wait=False)
        hh = h_ref[...]
        w = w_ref[...]
        for kk in range(TOP_K):
            hh = hh + w[:, kk:kk + 1] * ybuf[slot, kk * tc:(kk + 1) * tc, :]
        o_ref[...] = hh * lax.rsqrt(jnp.mean(hh * hh, axis=-1, keepdims=True) + RMS_EPS) * g_ref[...]

    pl.when(i + 1 < n)(functools.partial(run_tile, True))
    pl.when(i + 1 == n)(functools.partial(run_tile, False))


def _combine(h, y_buf, pos, wts, g_final, tc=128):
    n, d = h.shape
    return pl.pallas_call(
        functools.partial(_combine_kernel, tc=tc),
        out_shape=jax.ShapeDtypeStruct((n, d), F32),
        grid_spec=pltpu.PrefetchScalarGridSpec(
            num_scalar_prefetch=1,
            grid=(n // tc,),
            in_specs=[pl.BlockSpec((tc, d), lambda i, pos: (i, 0)),
                      pl.BlockSpec((1, d), lambda i, pos: (0, 0)),
                      pl.BlockSpec((tc, LANES), lambda i, pos: (i, 0)),
                      pl.BlockSpec(memory_space=pl.ANY)],
            out_specs=pl.BlockSpec((tc, d), lambda i, pos: (i, 0)),
            scratch_shapes=[pltpu.VMEM((2, TOP_K * tc, d), F32), pltpu.SemaphoreType.DMA((2,))]),
        compiler_params=_cparams(("arbitrary",), 40),
        name="moe_combine_final_norm",
    )(pos, h, g_final.reshape(1, d).astype(F32), wts, y_buf)


def _dispatch_plan(ids, counts, blk):
    n_tok = ids.shape[0]
    n_experts = counts.shape[0]
    n_slot = n_tok * TOP_K
    counts = counts.astype(jnp.int32)
    padded = (counts + blk - 1) // blk * blk
    pad_end = jnp.cumsum(padded)
    pad_start = pad_end - padded
    dest = pad_start[ids[:, :TOP_K]] + ids[:, TOP_K:2 * TOP_K]
    n_blk = -(-n_slot // blk) + n_experts
    blk_start = jnp.arange(n_blk, dtype=jnp.int32) * blk
    blk_expert = jnp.minimum(jnp.sum(pad_end[None, :] <= blk_start[:, None], axis=1), n_experts - 1).astype(jnp.int32)
    n_active = (pad_end[-1] // blk).astype(jnp.int32).reshape(1)
    return dest.astype(jnp.int32), blk_expert, n_active, n_blk


def _layer(x, g_mix, w_in, conv_w, a_log, dt_bias, norm_w, pe_k, w1_k, w2_k, pe_v, w1_v, w2_v,
           w_a, w_b, w_out, g_ffn, w_group, b_group, w_expert, b_expert, w_gate_up, w_down, g_out):
    batch, seq, dm = x.shape
    n = batch * seq
    d = HEAD_DIM
    gdn_w = w_a.shape[0]
    nsa_w = w_b.shape[0]
    heads = gdn_w // d
    nsa_heads = nsa_w // d
    groups = NSA_GROUPS
    hpg = nsa_heads // groups
    kvw = groups * d

    sizes = (3 * gdn_w, gdn_w, heads, heads, nsa_w, 6 * kvw, 3 * nsa_heads, 2 * dm)
    assert sum(sizes) == w_in.shape[1]
    offs = np.concatenate([[0], np.cumsum(sizes)])
    seg = lambda i: w_in[:, offs[i]:offs[i + 1]]
    w_big = jnp.concatenate([seg(0), seg(1), seg(4), seg(5), seg(7)], axis=1).astype(BF16)
    w_small = jnp.concatenate([seg(2), seg(3), seg(6)], axis=1).astype(BF16)
    z_col = 3 * heads
    nq_col = z_col + heads
    kv_col = nq_col + nsa_heads
    gate_col0 = (kv_col + 6 * groups) * d

    x2 = x.reshape(n, dm)
    xn = _rmsnorm(x2, g_mix, BF16)
    n_experts, _, ff2 = w_gate_up.shape
    proj, wgu_bf, wd_bf = _matmul_with_casts(
        xn, w_big, BF16, tm=1024, tn=512, name="in_proj",
        riders=[w_gate_up.reshape(n_experts * dm, ff2), w_down.reshape(n_experts * (ff2 // 2), dm)])
    wgu_bf = wgu_bf.reshape(w_gate_up.shape)
    wd_bf = wd_bf.reshape(w_down.shape)
    small = _matmul(xn, w_small, F32, tm=1024, tn=w_small.shape[1], name="in_proj_small")
    a_raw, b_raw, nsa_gate = small[:, :heads], small[:, heads:2 * heads], small[:, 2 * heads:]

    gc, beta, gl = _gdn_gates(a_raw, b_raw, a_log, dt_bias, GDN_CHUNK)
    gct = gc.reshape(batch, seq, heads).transpose(0, 2, 1).reshape(batch * heads, 1, seq)
    o_gdn = _gdn_mixer(proj, conv_w, gc, beta, gl, gct, norm_w, batch, seq, heads, GDN_CHUNK)

    k_cmp, v_cmp = _nsa_compress(proj, kv_col, kv_col + groups, pe_k, w1_k, w2_k, pe_v, w1_v, w2_v,
                                 batch, seq, groups)
    gates = nsa_gate.reshape(batch, seq, groups, 3 * hpg).transpose(0, 2, 1, 3).reshape(batch * groups, seq, 3 * hpg)
    o_nsa = _nsa_mixer(proj, nq_col, kv_col + 2 * groups, kv_col + 3 * groups, kv_col + 4 * groups,
                       kv_col + 5 * groups, k_cmp, v_cmp, gates, batch, seq, groups, hpg)

    merged = _merge(o_gdn, o_nsa, w_a.astype(BF16), w_b.astype(BF16), proj, gate_col0)
    h = _resid_matmul(merged, w_out.astype(BF16), x2)

    xn2, ids, wts, counts = _router(h, g_ffn, w_group, b_group, w_expert, b_expert)
    dest, blk_expert, n_active, n_blk = _dispatch_plan(ids, counts, MOE_BLOCK)
    tok_buf = _slot_scatter(dest.reshape(n * TOP_K), n_blk * MOE_BLOCK, TOP_K)
    y_buf = _experts(xn2, wgu_bf, wd_bf, blk_expert, tok_buf, n_active, MOE_BLOCK)
    tc = 128
    pos_tiles = dest.reshape(n // tc, tc, TOP_K).transpose(0, 2, 1).reshape(n * TOP_K)
    out = _combine(h, y_buf, pos_tiles, wts, g_out, tc)
    return out.reshape(batch, seq, dm)


def kernel(x, g_mix, w_in, gdn_conv_w, gdn_a_log, gdn_dt_bias, gdn_norm_w, cmp_pe_k, cmp_w1_k, cmp_w2_k,
           cmp_pe_v, cmp_w1_v, cmp_w2_v, w_branch_gdn, w_branch_nsa, w_out, g_ffn, w_group, b_group,
           w_expert, b_expert, w_gate_up, w_down, g_final):
    depth = g_mix.shape[0]
    assert depth == 1, "the fused final norm assumes a single layer"
    return _layer(x, g_mix[0], w_in[0], gdn_conv_w[0], gdn_a_log[0], gdn_dt_bias[0], gdn_norm_w[0],
                  cmp_pe_k[0], cmp_w1_k[0], cmp_w2_k[0], cmp_pe_v[0], cmp_w1_v[0], cmp_w2_v[0],
                  w_branch_gdn[0], w_branch_nsa[0], w_out[0], g_ffn[0], w_group[0], b_group[0],
                  w_expert[0], b_expert[0], w_gate_up[0], w_down[0], g_final)
```

```python
import functools
import math

import numpy as np
import jax
import jax.numpy as jnp
from jax import lax
from jax.experimental import pallas as pl
from jax.experimental.pallas import tpu as pltpu

F32 = jnp.float32
BF16 = jnp.bfloat16

RMS_EPS = 1e-6
NEG_INF = -1e30
FORCE_SCORE = 1e6

HEAD_DIM = 128
GDN_CHUNK = 64
NSA_GROUPS = 4
CMP_BLOCK = 32
CMP_STRIDE = 16
SEL_BLOCK = 64
SEL_TOP_N = 16
WINDOW = 512
N_GROUPS = 8
EXPERTS_PER_GROUP = 8
TOP_K = 2
MOE_BLOCK = 256
LANES = 128

V7X_VMEM_BYTES = 64 * 1024 * 1024


def _cparams(semantics, vmem_mb):
    assert vmem_mb * 1024 * 1024 < V7X_VMEM_BYTES
    return pltpu.CompilerParams(dimension_semantics=semantics, vmem_limit_bytes=vmem_mb * 1024 * 1024)


def _dot(a, b, **kw):
    return jnp.dot(a, b, preferred_element_type=F32, **kw)


def _dot_nt(a, b):
    return lax.dot_general(a, b, (((1,), (1,)), ((), ())), preferred_element_type=F32)


def _dot_tn(a, b):
    return lax.dot_general(a, b, (((0,), (0,)), ((), ())), preferred_element_type=F32)


def _sigmoid(x):
    return 1.0 / (1.0 + jnp.exp(-x))


def _silu(x):
    return x * _sigmoid(x)


def _rmsnorm_kernel(x_ref, g_ref, o_ref):
    x = x_ref[...].astype(F32)
    ms = jnp.mean(x * x, axis=-1, keepdims=True)
    o_ref[...] = (x * lax.rsqrt(ms + RMS_EPS) * g_ref[...]).astype(o_ref.dtype)


def _rmsnorm(x, gain, out_dtype, tm=512):
    n, d = x.shape
    return pl.pallas_call(
        _rmsnorm_kernel,
        out_shape=jax.ShapeDtypeStruct((n, d), out_dtype),
        grid=(n // tm,),
        in_specs=[pl.BlockSpec((tm, d), lambda i: (i, 0)), pl.BlockSpec((1, d), lambda i: (0, 0))],
        out_specs=pl.BlockSpec((tm, d), lambda i: (i, 0)),
        compiler_params=_cparams(("parallel",), 40),
        name="rmsnorm",
    )(x, gain.reshape(1, d).astype(F32))


def _mm_kernel(x_ref, w_ref, o_ref):
    o_ref[...] = _dot(x_ref[...], w_ref[...]).astype(o_ref.dtype)


def _matmul(x, w, out_dtype, tm, tn, name):
    m, k = x.shape
    n = w.shape[1]
    return pl.pallas_call(
        _mm_kernel,
        out_shape=jax.ShapeDtypeStruct((m, n), out_dtype),
        grid=(m // tm, n // tn),
        in_specs=[pl.BlockSpec((tm, k), lambda i, j: (i, 0)), pl.BlockSpec((k, tn), lambda i, j: (0, j))],
        out_specs=pl.BlockSpec((tm, tn), lambda i, j: (i, j)),
        compiler_params=_cparams(("parallel", "arbitrary"), 48),
        name=name,
    )(x, w)


def _mm_cast_kernel(x_ref, w_ref, *refs, n_riders):
    for r in range(n_riders):
        refs[n_riders + 1 + r][...] = refs[r][...].astype(BF16)
    o_ref = refs[n_riders]
    o_ref[...] = _dot(x_ref[...], w_ref[...]).astype(o_ref.dtype)


def _rider_rows(n_rows, max_blocks):
    rb = 16
    while n_rows % rb or n_rows // rb > max_blocks:
        rb += 16
    return rb


def _matmul_with_casts(x, w, out_dtype, tm, tn, name, riders):
    m, k = x.shape
    n = w.shape[1]
    ni, nj = m // tm, n // tn
    in_specs = [pl.BlockSpec((tm, k), lambda i, j: (i, 0)), pl.BlockSpec((k, tn), lambda i, j: (0, j))]
    out_specs = [pl.BlockSpec((tm, tn), lambda i, j: (i, j))]
    out_shape = [jax.ShapeDtypeStruct((m, n), out_dtype)]
    rider_specs = []
    for r in riders:
        rows, cols = r.shape
        rb = _rider_rows(rows, ni * nj)
        last = rows // rb - 1
        spec = pl.BlockSpec((rb, cols), lambda i, j, last=last: (jnp.minimum(i * nj + j, last), 0))
        rider_specs.append(spec)
        out_shape.append(jax.ShapeDtypeStruct((rows, cols), BF16))
    outs = pl.pallas_call(
        functools.partial(_mm_cast_kernel, n_riders=len(riders)),
        out_shape=tuple(out_shape),
        grid=(ni, nj),
        in_specs=in_specs[:2] + rider_specs,
        out_specs=tuple(out_specs + rider_specs),
        compiler_params=_cparams(("arbitrary", "arbitrary"), 48),
        name=name,
    )(x, w, *riders)
    return outs


def _merge_kernel(a_ref, b_ref, wa_ref, wb_ref, ga_ref, gb_ref, o_ref):
    ya = _dot(a_ref[...], wa_ref[...])
    yb = _dot(b_ref[...], wb_ref[...])
    o_ref[...] = (_sigmoid(ga_ref[...].astype(F32)) * ya + _sigmoid(gb_ref[...].astype(F32)) * yb).astype(o_ref.dtype)


def _merge(o_gdn, o_nsa, w_a, w_b, proj, gate_col0, tm=1024, tn=512):
    n, ka = o_gdn.shape
    kb = o_nsa.shape[1]
    d = w_a.shape[1]
    ja, jb = gate_col0 // tn, (gate_col0 + d) // tn
    return pl.pallas_call(
        _merge_kernel,
        out_shape=jax.ShapeDtypeStruct((n, d), BF16),
        grid=(n // tm, d // tn),
        in_specs=[pl.BlockSpec((tm, ka), lambda i, j: (i, 0)),
                  pl.BlockSpec((tm, kb), lambda i, j: (i, 0)),
                  pl.BlockSpec((ka, tn), lambda i, j: (0, j)),
                  pl.BlockSpec((kb, tn), lambda i, j: (0, j)),
                  pl.BlockSpec((tm, tn), lambda i, j: (i, ja + j)),
                  pl.BlockSpec((tm, tn), lambda i, j: (i, jb + j))],
        out_specs=pl.BlockSpec((tm, tn), lambda i, j: (i, j)),
        compiler_params=_cparams(("parallel", "arbitrary"), 48),
        name="branch_merge",
    )(o_gdn, o_nsa, w_a, w_b, proj, proj)


def _resid_mm_kernel(x_ref, w_ref, r_ref, o_ref):
    o_ref[...] = r_ref[...] + _dot(x_ref[...], w_ref[...])


def _resid_matmul(x, w, resid, tm=1024, tn=512):
    m, k = x.shape
    n = w.shape[1]
    return pl.pallas_call(
        _resid_mm_kernel,
        out_shape=jax.ShapeDtypeStruct((m, n), F32),
        grid=(m // tm, n // tn),
        in_specs=[pl.BlockSpec((tm, k), lambda i, j: (i, 0)),
                  pl.BlockSpec((k, tn), lambda i, j: (0, j)),
                  pl.BlockSpec((tm, tn), lambda i, j: (i, j))],
        out_specs=pl.BlockSpec((tm, tn), lambda i, j: (i, j)),
        compiler_params=_cparams(("parallel", "arbitrary"), 48),
        name="out_proj_residual",
    )(x, w, resid)


def _gdn_gate_kernel(a_ref, b_ref, alog_ref, dtb_ref, gc_ref, beta_ref, gl_ref, *, chunk):
    tm = a_ref.shape[0]
    x = a_ref[...] + dtb_ref[...]
    softplus = jnp.maximum(x, 0.0) + jnp.log(1.0 + jnp.exp(-jnp.abs(x)))
    g = -jnp.exp(alog_ref[...]) * softplus
    row = lax.broadcasted_iota(jnp.int32, (tm, tm), 0)
    col = lax.broadcasted_iota(jnp.int32, (tm, tm), 1)
    same = row // chunk == col // chunk
    tri = jnp.where((col <= row) & same, 1.0, 0.0).astype(F32)
    gc_ref[...] = _dot(tri, g, precision=lax.Precision.HIGHEST)
    gl_ref[...] = _dot(jnp.where(same, 1.0, 0.0).astype(F32), g, precision=lax.Precision.HIGHEST)
    beta_ref[...] = _sigmoid(b_ref[...])


def _gdn_gates(a_raw, b_raw, a_log, dt_bias, chunk, tm=512):
    n, h = a_raw.shape
    spec = pl.BlockSpec((tm, h), lambda i: (i, 0))
    vec = pl.BlockSpec((1, h), lambda i: (0, 0))
    sds = jax.ShapeDtypeStruct((n, h), F32)
    return pl.pallas_call(
        functools.partial(_gdn_gate_kernel, chunk=chunk),
        out_shape=(sds, sds, sds),
        grid=(n // tm,),
        in_specs=[spec, spec, vec, vec],
        out_specs=(spec, spec, spec),
        compiler_params=_cparams(("parallel",), 32),
        name="gdn_gates",
    )(a_raw, b_raw, a_log.reshape(1, h).astype(F32), dt_bias.reshape(1, h).astype(F32))


_CONV_PAD = 8


def _gdn_kernel(q_ref, k_ref, v_ref, z_ref, cwq_ref, cwk_ref, cwv_ref, gc_ref, beta_ref, gl_ref, gct_ref, nw_ref,
                o_ref, state, xbuf, *, chunk, hps):
    hblk = pl.program_id(1)
    t = pl.program_id(2)
    tt = q_ref.shape[0]
    d = HEAD_DIM
    width = cwq_ref.shape[0]
    c = chunk
    n_ch = tt // c
    n_lvl = int(math.log2(c))
    assert 2 ** n_lvl == c

    @pl.when(t == 0)
    def _():
        state[...] = jnp.zeros_like(state)
        xbuf[:, 0:_CONV_PAD, :] = jnp.zeros((3 * hps, _CONV_PAD, d), F32)

    def conv_silu(s, x_bf16, w):
        xbuf[s, _CONV_PAD:_CONV_PAD + tt, :] = x_bf16.astype(F32)
        y = None
        for j in range(width):
            term = xbuf[s, pl.ds(_CONV_PAD - (width - 1) + j, tt), :] * w[j:j + 1, :]
            y = term if y is None else y + term
        xbuf[s, 0:_CONV_PAD, :] = xbuf[s, tt:tt + _CONV_PAD, :]
        return _silu(y)

    def l2n(x):
        return x * lax.rsqrt(jnp.sum(x * x, axis=-1, keepdims=True) + RMS_EPS)

    ri = lax.broadcasted_iota(jnp.int32, (tt, tt), 0)
    ci = lax.broadcasted_iota(jnp.int32, (tt, tt), 1)
    incl = (ri >= ci) & (ri // c == ci // c)
    strict = ri > ci
    lane = lax.broadcasted_iota(jnp.int32, gc_ref.shape, 1)
    nw = nw_ref[...]

    hd = [dict() for _ in range(hps)]
    for hp, e in enumerate(hd):
        h = hblk * hps + hp
        hs = slice(hp * d, (hp + 1) * d)
        q = l2n(conv_silu(3 * hp, q_ref[:, hs], cwq_ref[:, hs])) * (d ** -0.5)
        k = l2n(conv_silu(3 * hp + 1, k_ref[:, hs], cwk_ref[:, hs]))
        v = conv_silu(3 * hp + 2, v_ref[:, hs], cwv_ref[:, hs])
        pick = lambda ref: jnp.sum(jnp.where(lane == h, ref[...], 0.0), axis=1, keepdims=True)
        gcc, bc, glc = pick(gc_ref), pick(beta_ref), pick(gl_ref)
        egc = jnp.exp(gcc)
        kb = k.astype(BF16)
        qk = _dot_nt(jnp.concatenate([q.astype(BF16), kb], axis=0), kb)
        dm = jnp.exp(jnp.where(incl, gcc - gct_ref[hp], NEG_INF))
        e["npow"] = jnp.where(strict, -(qk[tt:] * dm * bc), 0.0)
        e["x"] = jnp.concatenate([v * bc, k * (bc * egc)], axis=1)
        e["attn"] = (qk[:tt] * dm).astype(BF16)
        e["qe"] = q * egc
        e["kout"] = (k * jnp.exp(glc - gcc)).astype(BF16)
        e["dch"] = jnp.exp(glc)

    for lvl in range(n_lvl):
        for e in hd:
            nb = e["npow"].astype(BF16)
            if lvl + 1 < n_lvl:
                r = _dot(nb, jnp.concatenate([nb, e["x"].astype(BF16)], axis=1))
                e["npow"], e["x"] = r[:, :tt], e["x"] + r[:, tt:]
            else:
                e["x"] = e["x"] + _dot(nb, e["x"].astype(BF16))

    for hp, e in enumerate(hd):
        e["xb"] = e["x"].astype(BF16)
        au = _dot(e["attn"], e["xb"])
        e["o0"] = au[:, :d]
        e["qe"] = (e["qe"] - au[:, d:]).astype(BF16)
        e["s"] = state[hp]

    for ch in range(n_ch):
        sl = slice(ch * c, (ch + 1) * c)
        for hp, e in enumerate(hd):
            hs = slice(hp * d, (hp + 1) * d)
            ktx = _dot_tn(e["kout"][sl], e["xb"][sl])
            sb = e["s"].astype(BF16)
            o = _dot(e["qe"][sl], sb) + e["o0"][sl]
            e["s"] = e["s"] * e["dch"][ch * c:ch * c + 1, :] - _dot(ktx[:, d:].astype(BF16), sb) + ktx[:, :d]
            on = o * lax.rsqrt(jnp.mean(o * o, axis=-1, keepdims=True) + RMS_EPS) * nw
            o_ref[sl, hs] = (on * _silu(z_ref[sl, hs].astype(F32))).astype(o_ref.dtype)

    for hp, e in enumerate(hd):
        state[hp] = e["s"]


def _gdn_mixer(proj, conv_w, gc, beta, gl, gct, norm_w, batch, seq, heads, chunk, tt=256, hps=4):
    d = HEAD_DIM
    n = batch * seq
    nt = seq // tt
    hb = heads // hps
    width = conv_w.shape[0]
    assert width - 1 <= _CONV_PAD and tt % chunk == 0 and seq % tt == 0 and heads % hps == 0

    def slab(off):
        return pl.BlockSpec((tt, hps * d), lambda b, h, t: (b * nt + t, off + h))

    def cw(off):
        return pl.BlockSpec((width, hps * d), lambda b, h, t: (0, off + h))

    gspec = pl.BlockSpec((tt, heads), lambda b, h, t: (b * nt + t, 0))
    return pl.pallas_call(
        functools.partial(_gdn_kernel, chunk=chunk, hps=hps),
        out_shape=jax.ShapeDtypeStruct((n, heads * d), BF16),
        grid=(batch, hb, nt),
        in_specs=[slab(0), slab(hb), slab(2 * hb), slab(3 * hb),
                  cw(0), cw(hb), cw(2 * hb),
                  gspec, gspec, gspec,
                  pl.BlockSpec((hps, 1, tt), lambda b, h, t: (b * hb + h, 0, t)),
                  pl.BlockSpec((1, d), lambda b, h, t: (0, 0))],
        out_specs=pl.BlockSpec((tt, hps * d), lambda b, h, t: (b * nt + t, h)),
        scratch_shapes=[pltpu.VMEM((hps, d, d), F32), pltpu.VMEM((3 * hps, tt + _CONV_PAD, d), F32)],
        compiler_params=_cparams(("parallel", "parallel", "arbitrary"), 40),
        name="gdn_mixer",
    )(proj, proj, proj, proj, conv_w, conv_w, conv_w, gc, beta, gl, gct, norm_w.reshape(1, d).astype(F32))


def _nsa_cmp_kernel(kc_ref, vc_ref, pek_ref, w1k_ref, w2k_ref, pev_ref, w1v_ref, w2v_ref,
                    ko_ref, vo_ref, xf, sh, *, stride, block):
    seq, d = kc_ref.shape
    nh = seq // stride
    reps = block // stride
    assert nh % 8 == 0

    def compress(src_ref, pe_ref, w1_ref, w2_ref, dst_ref):
        xf[...] = src_ref[...].astype(F32)
        pe = pe_ref[...]
        hid = None
        for r in range(reps):
            acc = None
            for l in range(stride):
                li = r * stride + l
                rows = xf[pl.ds(l, nh, stride=stride), :] + pe[li:li + 1, :]
                term = _dot(rows.astype(BF16), w1_ref[li].astype(BF16))
                acc = term if acc is None else acc + term
            if r == 0:
                hid = acc
            else:
                sh[0:nh, :] = acc
                sh[nh:nh + 8, :] = jnp.zeros((8, d), F32)
                hid = hid + sh[pl.ds(r, nh), :]
        hid = _silu(hid)
        dst_ref[0] = _dot(hid.astype(BF16), w2_ref[...].astype(BF16)).astype(dst_ref.dtype)

    compress(kc_ref, pek_ref, w1k_ref, w2k_ref, ko_ref)
    compress(vc_ref, pev_ref, w1v_ref, w2v_ref, vo_ref)


def _nsa_compress(proj, kc_col, vc_col, pe_k, w1_k, w2_k, pe_v, w1_v, w2_v, batch, seq, groups):
    d = HEAD_DIM
    nh = seq // CMP_STRIDE
    full2 = lambda b, g: (0, 0)
    full3 = lambda b, g: (0, 0, 0)
    out_sds = jax.ShapeDtypeStruct((batch * groups, nh, d), BF16)
    out_spec = pl.BlockSpec((1, nh, d), lambda b, g: (b * groups + g, 0, 0))
    return pl.pallas_call(
        functools.partial(_nsa_cmp_kernel, stride=CMP_STRIDE, block=CMP_BLOCK),
        out_shape=(out_sds, out_sds),
        grid=(batch, groups),
        in_specs=[pl.BlockSpec((seq, d), lambda b, g: (b, kc_col + g)),
                  pl.BlockSpec((seq, d), lambda b, g: (b, vc_col + g)),
                  pl.BlockSpec((CMP_BLOCK, d), full2), pl.BlockSpec((CMP_BLOCK, d, d), full3), pl.BlockSpec((d, d), full2),
                  pl.BlockSpec((CMP_BLOCK, d), full2), pl.BlockSpec((CMP_BLOCK, d, d), full3), pl.BlockSpec((d, d), full2)],
        out_specs=(out_spec, out_spec),
        scratch_shapes=[pltpu.VMEM((seq, d), F32), pltpu.VMEM((nh + 8, d), F32)],
        compiler_params=_cparams(("parallel", "parallel"), 32),
        name="nsa_compress",
    )(proj, proj, pe_k, w1_k, w2_k, pe_v, w1_v, w2_v)


def _nsa_kernel(q_ref, kc_ref, vc_ref, ks_ref, vs_ref, kw_ref, vw_ref, gate_ref, selmap_ref, expand_ref,
                o_ref, score_t, *, hpg, gps, tk, n_sel, n_top):
    tq = q_ref.shape[0]
    d = HEAD_DIM
    gw = hpg * d
    n_cmp_pad = kc_ref.shape[1]
    t0 = pl.program_id(2) * tq
    rows = hpg * tq
    grp = list(range(gps))
    kv = lambda ref, gi, r0, n: ref[pl.ds(r0, n), gi * d:(gi + 1) * d]

    q4 = []
    for gi in grp:
        qg = jnp.concatenate([q_ref[:, gi * gw + hh * d:gi * gw + (hh + 1) * d] for hh in range(hpg)], axis=0)
        q4.append((qg.astype(F32) * (d ** -0.5 * math.log2(math.e))).astype(BF16))
    trow = t0 + lax.broadcasted_iota(jnp.int32, (tq, 1), 0)

    ncol = lax.broadcasted_iota(jnp.int32, (1, n_cmp_pad), 1)
    vis_c = ((ncol * CMP_STRIDE + (CMP_BLOCK - 1)) <= trow)[None]
    s_c = [_dot_nt(q4[gi], kc_ref[gi]).reshape(hpg, tq, n_cmp_pad) for gi in grp]
    p_c = []
    for gi in grp:
        s3 = jnp.where(vis_c, s_c[gi], NEG_INF)
        e = jnp.where(vis_c, jnp.exp2(s3 - jnp.max(s3, axis=-1, keepdims=True)), 0.0)
        den = jnp.sum(e, axis=-1, keepdims=True)
        p_c.append(e / jnp.where(den > 0.0, den, 1.0))
    o_cmp = [_dot(p_c[gi].reshape(rows, n_cmp_pad).astype(BF16), vc_ref[gi]).reshape(hpg, tq, d) for gi in grp]

    jrow = lax.broadcasted_iota(jnp.int32, (n_sel, 1), 0)
    tcol = t0 + lax.broadcasted_iota(jnp.int32, (1, tq), 1)
    cur = tcol // SEL_BLOCK
    forced = (jrow == 0) | (jrow == cur) | (jrow == cur - 1)
    causal_blk = jrow * SEL_BLOCK <= tcol
    keys = []
    for gi in grp:
        imp_t = lax.dot_general(selmap_ref[...], jnp.sum(p_c[gi], axis=0), (((1,), (1,)), ((), ())),
                                preferred_element_type=F32, precision=lax.Precision.HIGHEST)
        sc = jnp.where(forced, FORCE_SCORE, jnp.where(causal_blk, imp_t, NEG_INF))
        bits = pltpu.bitcast(sc, jnp.int32)
        keys.append(bits ^ ((bits >> 31) & jnp.int32(0x7FFFFFFF)))
        score_t[gi] = keys[gi]

    def rank_step(i, cnts):
        tie = jnp.where(jrow > i, 1, 0)
        return tuple(cnts[gi] + jnp.where(score_t[gi, pl.ds(i, 1), :] + tie > keys[gi], 1.0, 0.0) for gi in grp)

    n_rank = jnp.minimum((t0 + tq - 1) // SEL_BLOCK + 1, n_sel)
    cnts = lax.fori_loop(0, n_rank, rank_step, tuple(jnp.zeros((n_sel, tq), F32) for _ in grp))
    eye = jnp.where(lax.broadcasted_iota(jnp.int32, (n_sel, n_sel), 0)
                    == lax.broadcasted_iota(jnp.int32, (n_sel, n_sel), 1), 1.0, 0.0).astype(BF16)
    sel_bias = [_dot_tn(jnp.where(cnts[gi] < n_top, 0.0, NEG_INF).astype(BF16), eye).astype(BF16) for gi in grp]

    def sel_step(kt, carry, diagonal):
        k0 = pl.multiple_of(kt * tk, tk)
        ex = expand_ref[:, pl.ds(k0, tk)]
        s3 = []
        for gi in grp:
            b = _dot(sel_bias[gi], ex)
            if diagonal:
                kpos = k0 + lax.broadcasted_iota(jnp.int32, (1, tk), 1)
                b = jnp.where(kpos <= trow, b, NEG_INF)
            s3.append(_dot_nt(q4[gi], kv(ks_ref, gi, k0, tk)).reshape(hpg, tq, tk) + b[None])
        out = []
        for gi in grp:
            m, acc = carry[gi]
            m_new = jnp.maximum(m, jnp.max(s3[gi], axis=-1, keepdims=True))
            alpha = jnp.exp2(m - m_new)
            e = jnp.exp2((s3[gi] - m_new).astype(BF16))
            pv = _dot(e.reshape(rows, tk), with_ones(kv(vs_ref, gi, k0, tk)))
            out.append((m_new, alpha * acc + pv.reshape(hpg, tq, 2 * d)))
        return tuple(out)

    def with_ones(v_tile):
        return jnp.concatenate([v_tile, jnp.ones(v_tile.shape, BF16)], axis=1)

    n_full = t0 // tk
    init = (jnp.full((hpg, tq, 1), NEG_INF, F32), jnp.zeros((hpg, tq, 2 * d), F32))
    carry = lax.fori_loop(0, n_full, functools.partial(sel_step, diagonal=False), tuple(init for _ in grp))
    carry = sel_step(n_full, carry, diagonal=True)
    o_sel = [carry[gi][1][:, :, :d] / carry[gi][1][:, :, d:] for gi in grp]

    span = WINDOW + tq
    ws = pl.multiple_of(jnp.maximum(t0 - WINDOW, 0), tq)
    kpos = ws + lax.broadcasted_iota(jnp.int32, (1, span), 1)
    rel = trow - kpos
    b_w = jnp.where((rel >= 0) & (rel < WINDOW), 0.0, NEG_INF)[None]
    s_w = [_dot_nt(q4[gi], kv(kw_ref, gi, ws, span)).reshape(hpg, tq, span) + b_w for gi in grp]
    o_win = []
    for gi in grp:
        e = jnp.exp2((s_w[gi] - jnp.max(s_w[gi], axis=-1, keepdims=True)).astype(BF16))
        pv = _dot(e.reshape(rows, span), with_ones(kv(vw_ref, gi, ws, span))).reshape(hpg, tq, 2 * d)
        o_win.append(pv[:, :, :d] / pv[:, :, d:])

    for gi in grp:
        gates = _sigmoid(gate_ref[gi])
        for hh in range(hpg):
            g0 = gates[:, 3 * hh:3 * hh + 1]
            g1 = gates[:, 3 * hh + 1:3 * hh + 2]
            g2 = gates[:, 3 * hh + 2:3 * hh + 3]
            o_ref[:, gi * gw + hh * d:gi * gw + (hh + 1) * d] = (
                g0 * o_cmp[gi][hh] + g1 * o_sel[gi][hh] + g2 * o_win[gi][hh]).astype(o_ref.dtype)


def _nsa_mixer(proj, q_col, ks_col, vs_col, kw_col, vw_col, k_cmp, v_cmp, gates, batch, seq, groups, hpg,
               tq=256, tk=1024, gps=2):
    d = HEAD_DIM
    n = batch * seq
    nq = seq // tq
    n_sel = seq // SEL_BLOCK
    n_top = min(SEL_TOP_N, n_sel)
    n_cmp = (seq - CMP_BLOCK) // CMP_STRIDE + 1
    n_cmp_pad = k_cmp.shape[1]
    assert n_sel % 8 == 0 and seq >= WINDOW + tq and WINDOW % tq == 0 and tk % tq == 0 and seq % tk == 0

    cmp_start = np.arange(n_cmp_pad) * CMP_STRIDE
    sel_start = np.arange(n_sel) * SEL_BLOCK
    overlap = (np.minimum(cmp_start[None, :] + CMP_BLOCK, sel_start[:, None] + SEL_BLOCK)
               - np.maximum(cmp_start[None, :], sel_start[:, None]))
    sel_map_t = np.clip(overlap, 0, None) / CMP_STRIDE
    sel_map_t[:, n_cmp:] = 0.0
    expand = (np.arange(seq)[None, :] // SEL_BLOCK == np.arange(n_sel)[:, None]).astype(np.float32)

    gb = groups // gps
    qw = gps * hpg * d
    assert groups % gps == 0 and all(c % gps == 0 for c in (ks_col, vs_col, kw_col, vw_col)) and q_col % (gps * hpg) == 0

    def kv(col):
        return pl.BlockSpec((seq, gps * d), lambda b, g, t: (b, col // gps + g))

    cmp_spec = pl.BlockSpec((gps, n_cmp_pad, d), lambda b, g, t: (b * gb + g, 0, 0))
    return pl.pallas_call(
        functools.partial(_nsa_kernel, hpg=hpg, gps=gps, tk=tk, n_sel=n_sel, n_top=n_top),
        out_shape=jax.ShapeDtypeStruct((n, groups * hpg * d), BF16),
        grid=(batch, gb, nq),
        in_specs=[pl.BlockSpec((tq, qw), lambda b, g, t: (b * nq + t, q_col // (gps * hpg) + g)),
                  cmp_spec, cmp_spec, kv(ks_col), kv(vs_col), kv(kw_col), kv(vw_col),
                  pl.BlockSpec((gps, tq, 3 * hpg), lambda b, g, t: (b * gb + g, t, 0)),
                  pl.BlockSpec((n_sel, n_cmp_pad), lambda b, g, t: (0, 0)),
                  pl.BlockSpec((n_sel, seq), lambda b, g, t: (0, 0))],
        out_specs=pl.BlockSpec((tq, qw), lambda b, g, t: (b * nq + t, g)),
        scratch_shapes=[pltpu.VMEM((gps, n_sel, tq), jnp.int32)],
        compiler_params=_cparams(("parallel", "parallel", "arbitrary"), 48),
        name="nsa_attention",
    )(proj, k_cmp, v_cmp, proj, proj, proj, proj, gates, jnp.asarray(sel_map_t, F32), jnp.asarray(expand, BF16))


def _router_kernel(h_ref, g_ref, wr_ref, wrl_ref, br_ref, xn_ref, ids_ref, wts_ref, cnt_ref):
    x = h_ref[...]
    xn = x * lax.rsqrt(jnp.mean(x * x, axis=-1, keepdims=True) + RMS_EPS) * g_ref[...]
    xn_ref[...] = xn
    x_hi = xn.astype(BF16)
    x_lo = (xn - x_hi.astype(F32)).astype(BF16)
    logits = (_dot(x_hi, wr_ref[...]) + _dot(x_hi, wrl_ref[...]) + _dot(x_lo, wr_ref[...])) + br_ref[...]
    lane = lax.broadcasted_iota(jnp.int32, logits.shape, 1)
    big = jnp.int32(2 * LANES)
    is_g = lane < N_GROUPS
    gl = jnp.where(is_g, logits, NEG_INF)
    gm = jnp.max(gl, axis=-1, keepdims=True)
    p_g = 1.0 / jnp.sum(jnp.where(is_g, jnp.exp(gl - gm), 0.0), axis=-1, keepdims=True)
    grp = jnp.min(jnp.where(gl == gm, lane, big), axis=-1, keepdims=True)
    eidx = lane - N_GROUPS
    in_grp = (eidx >= 0) & (eidx // EXPERTS_PER_GROUP == grp) & (eidx < N_GROUPS * EXPERTS_PER_GROUP)
    el = jnp.where(in_grp, logits, NEG_INF)
    em = jnp.max(el, axis=-1, keepdims=True)
    ee = jnp.where(in_grp, jnp.exp(el - em), 0.0)
    pe = ee / jnp.sum(ee, axis=-1, keepdims=True)
    p1 = jnp.max(jnp.where(in_grp, pe, -1.0), axis=-1, keepdims=True)
    i1 = jnp.min(jnp.where(in_grp & (pe == p1), lane, big), axis=-1, keepdims=True)
    rest = in_grp & (lane != i1)
    p2 = jnp.max(jnp.where(rest, pe, -1.0), axis=-1, keepdims=True)
    i2 = jnp.min(jnp.where(rest & (pe == p2), lane, big), axis=-1, keepdims=True)
    denom = p1 + p2
    wts_ref[...] = jnp.where(lane == 0, p_g * p1 / denom, p_g * p2 / denom)

    @pl.when(pl.program_id(0) == 0)
    def _():
        cnt_ref[...] = jnp.zeros_like(cnt_ref)

    tm = logits.shape[0]
    hot = jnp.where((lane == i1) | (lane == i2), 1.0, 0.0)
    below = jnp.where(lax.broadcasted_iota(jnp.int32, (tm, tm), 0) > lax.broadcasted_iota(jnp.int32, (tm, tm), 1),
                      1.0, 0.0).astype(BF16)
    before = _dot(below, hot.astype(BF16)) + cnt_ref[...]
    r1 = jnp.sum(jnp.where(lane == i1, before, 0.0), axis=-1, keepdims=True).astype(jnp.int32)
    r2 = jnp.sum(jnp.where(lane == i2, before, 0.0), axis=-1, keepdims=True).astype(jnp.int32)
    cnt_ref[...] += jnp.sum(hot, axis=0, keepdims=True)
    ids_ref[...] = jnp.where(lane == 0, i1 - N_GROUPS, jnp.where(lane == 1, i2 - N_GROUPS, jnp.where(lane == 2, r1, r2)))


def _router(h, g_ffn, w_group, b_group, w_expert, b_expert, tm=256):
    n, d = h.shape
    n_g, n_e = w_group.shape[1], w_expert.shape[1]
    n_r = n_g + n_e
    assert n_r <= LANES and TOP_K == 2 and n_g == N_GROUPS
    wr = jnp.concatenate([w_group, w_expert, jnp.zeros((d, LANES - n_r), F32)], axis=1).astype(F32)
    wr_hi = wr.astype(BF16)
    wr_lo = (wr - wr_hi.astype(F32)).astype(BF16)
    br = jnp.concatenate([b_group, b_expert, jnp.zeros((LANES - n_r,), F32)]).reshape(1, LANES).astype(F32)
    row = lambda i: (i, 0)
    fixed = lambda i: (0, 0)
    xn, ids, wts, cnt = pl.pallas_call(
        _router_kernel,
        out_shape=(jax.ShapeDtypeStruct((n, d), F32), jax.ShapeDtypeStruct((n, LANES), jnp.int32),
                   jax.ShapeDtypeStruct((n, LANES), F32), jax.ShapeDtypeStruct((1, LANES), F32)),
        grid=(n // tm,),
        in_specs=[pl.BlockSpec((tm, d), row), pl.BlockSpec((1, d), fixed),
                  pl.BlockSpec((d, LANES), fixed), pl.BlockSpec((d, LANES), fixed), pl.BlockSpec((1, LANES), fixed)],
        out_specs=(pl.BlockSpec((tm, d), row), pl.BlockSpec((tm, LANES), row), pl.BlockSpec((tm, LANES), row),
                   pl.BlockSpec((1, LANES), fixed)),
        compiler_params=_cparams(("arbitrary",), 40),
        name="moe_router",
    )(h, g_ffn.reshape(1, d).astype(F32), wr_hi, wr_lo, br)
    return xn, ids, wts, cnt[0, n_g:n_r]


def _row_gather(idx_ref, base, src_hbm, dst, sem, n_rows, wait):
    for r in range(n_rows):
        cp = pltpu.make_async_copy(src_hbm.at[pl.ds(idx_ref[base + r], 1), :], dst.at[pl.ds(r, 1), :], sem)
        if wait:
            cp.wait()
        else:
            cp.start()


def _slot_scatter_kernel(dest_ref, tok_ref, *, top_k, unroll):
    n_out = tok_ref.shape[0]
    n_slot = dest_ref.shape[0]

    def clear(j, carry):
        for u in range(unroll):
            tok_ref[j * unroll + u] = 0
        return carry

    def place(j, carry):
        rows = [dest_ref[j * unroll + u] for u in range(unroll)]
        for u in range(unroll):
            tok_ref[rows[u]] = (j * unroll + u) // top_k
        return carry

    lax.fori_loop(0, n_out // unroll, clear, 0)
    lax.fori_loop(0, n_slot // unroll, place, 0)


def _slot_scatter(dest_flat, n_out, top_k, unroll=8):
    assert n_out % unroll == 0 and dest_flat.shape[0] % unroll == 0
    return pl.pallas_call(
        functools.partial(_slot_scatter_kernel, top_k=top_k, unroll=unroll),
        out_shape=jax.ShapeDtypeStruct((n_out,), jnp.int32),
        in_specs=[pl.BlockSpec(memory_space=pltpu.SMEM)],
        out_specs=pl.BlockSpec(memory_space=pltpu.SMEM),
        name="moe_slot_scatter",
    )(dest_flat)


def _expert_kernel(be_ref, tok_ref, nact_ref, x_hbm, wgu_ref, wd_ref, y_ref, xbuf, sem, *, blk, d_ff):
    i = pl.program_id(0)
    n_act = nact_ref[0]
    slot = i % 2

    @pl.when(i == 0)
    def _():
        _row_gather(tok_ref, 0, x_hbm, xbuf.at[0], sem.at[0], blk, wait=False)

    def run_block(prefetch_next):
        _row_gather(tok_ref, i * blk, x_hbm, xbuf.at[slot], sem.at[slot], blk, wait=True)
        if prefetch_next:
            _row_gather(tok_ref, (i + 1) * blk, x_hbm, xbuf.at[1 - slot], sem.at[1 - slot], blk, wait=False)
        xb = xbuf[slot].astype(BF16)
        gu = _dot(xb, wgu_ref[0])
        act = _silu(gu[:, :d_ff]) * gu[:, d_ff:]
        y_ref[...] = _dot(act.astype(BF16), wd_ref[0])

    pl.when(i + 1 < n_act)(functools.partial(run_block, True))
    pl.when(i + 1 == n_act)(functools.partial(run_block, False))

    @pl.when(i >= n_act)
    def _():
        y_ref[...] = jnp.zeros_like(y_ref)


def _experts(xn, w_gate_up, w_down, blk_expert, tok_buf, n_active, blk):
    n, d = xn.shape
    n_exp, _, ff2 = w_gate_up.shape
    d_ff = ff2 // 2
    n_blk = blk_expert.shape[0]
    return pl.pallas_call(
        functools.partial(_expert_kernel, blk=blk, d_ff=d_ff),
        out_shape=jax.ShapeDtypeStruct((n_blk * blk, d), F32),
        grid_spec=pltpu.PrefetchScalarGridSpec(
            num_scalar_prefetch=3,
            grid=(n_blk,),
            in_specs=[pl.BlockSpec(memory_space=pl.ANY),
                      pl.BlockSpec((1, d, ff2), lambda i, be, tok, na: (be[i], 0, 0)),
                      pl.BlockSpec((1, d_ff, d), lambda i, be, tok, na: (be[i], 0, 0))],
            out_specs=pl.BlockSpec((blk, d), lambda i, be, tok, na: (i, 0)),
            scratch_shapes=[pltpu.VMEM((2, blk, d), F32), pltpu.SemaphoreType.DMA((2,))]),
        compiler_params=_cparams(("arbitrary",), 56),
        name="moe_experts",
    )(blk_expert, tok_buf, n_active, xn, w_gate_up, w_down)


def _combine_kernel(pos_ref, h_ref, g_ref, w_ref, y_hbm, o_ref, ybuf, sem, *, tc):
    i = pl.program_id(0)
    n = pl.num_programs(0)
    slot = i % 2
    rows = TOP_K * tc

    @pl.when(i == 0)
    def _():
        _row_gather(pos_ref, 0, y_hbm, ybuf.at[0], sem.at[0], rows, wait=False)

    def run_tile(prefetch_next):
        _row_gather(pos_ref, i * rows, y_hbm, ybuf.at[slot], sem.at[slot], rows, wait=True)
        if prefetch_next:
            _row_gather(pos_ref, (i + 1) * rows, y_hbm, ybuf.at[1 - slot], sem.at[1 - slot], rows, wait=False)
        hh = h_ref[...]
        w = w_ref[...]
        for kk in range(TOP_K):
            hh = hh + w[:, kk:kk + 1] * ybuf[slot, kk * tc:(kk + 1) * tc, :]
        o_ref[...] = hh * lax.rsqrt(jnp.mean(hh * hh, axis=-1, keepdims=True) + RMS_EPS) * g_ref[...]

    pl.when(i + 1 < n)(functools.partial(run_tile, True))
    pl.when(i + 1 == n)(functools.partial(run_tile, False))


def _combine(h, y_buf, pos, wts, g_final, tc=128):
    n, d = h.shape
    return pl.pallas_call(
        functools.partial(_combine_kernel, tc=tc),
        out_shape=jax.ShapeDtypeStruct((n, d), F32),
        grid_spec=pltpu.PrefetchScalarGridSpec(
            num_scalar_prefetch=1,
            grid=(n // tc,),
            in_specs=[pl.BlockSpec((tc, d), lambda i, pos: (i, 0)),
                      pl.BlockSpec((1, d), lambda i, pos: (0, 0)),
                      pl.BlockSpec((tc, LANES), lambda i, pos: (i, 0)),
                      pl.BlockSpec(memory_space=pl.ANY)],
            out_specs=pl.BlockSpec((tc, d), lambda i, pos: (i, 0)),
            scratch_shapes=[pltpu.VMEM((2, TOP_K * tc, d), F32), pltpu.SemaphoreType.DMA((2,))]),
        compiler_params=_cparams(("arbitrary",), 40),
        name="moe_combine_final_norm",
    )(pos, h, g_final.reshape(1, d).astype(F32), wts, y_buf)


def _dispatch_plan(ids, counts, blk):
    n_tok = ids.shape[0]
    n_experts = counts.shape[0]
    n_slot = n_tok * TOP_K
    counts = counts.astype(jnp.int32)
    padded = (counts + blk - 1) // blk * blk
    pad_end = jnp.cumsum(padded)
    pad_start = pad_end - padded
    dest = pad_start[ids[:, :TOP_K]] + ids[:, TOP_K:2 * TOP_K]
    n_blk = -(-n_slot // blk) + n_experts
    blk_start = jnp.arange(n_blk, dtype=jnp.int32) * blk
    blk_expert = jnp.minimum(jnp.sum(pad_end[None, :] <= blk_start[:, None], axis=1), n_experts - 1).astype(jnp.int32)
    n_active = (pad_end[-1] // blk).astype(jnp.int32).reshape(1)
    return dest.astype(jnp.int32), blk_expert, n_active, n_blk


def _layer(x, g_mix, w_in, conv_w, a_log, dt_bias, norm_w, pe_k, w1_k, w2_k, pe_v, w1_v, w2_v,
           w_a, w_b, w_out, g_ffn, w_group, b_group, w_expert, b_expert, w_gate_up, w_down, g_out):
    batch, seq, dm = x.shape
    n = batch * seq
    d = HEAD_DIM
    gdn_w = w_a.shape[0]
    nsa_w = w_b.shape[0]
    heads = gdn_w // d
    nsa_heads = nsa_w // d
    groups = NSA_GROUPS
    hpg = nsa_heads // groups
    kvw = groups * d

    sizes = (3 * gdn_w, gdn_w, heads, heads, nsa_w, 6 * kvw, 3 * nsa_heads, 2 * dm)
    assert sum(sizes) == w_in.shape[1]
    offs = np.concatenate([[0], np.cumsum(sizes)])
    seg = lambda i: w_in[:, offs[i]:offs[i + 1]]
    w_big = jnp.concatenate([seg(i).astype(BF16) for i in (0, 1, 4, 5, 7)], axis=1)
    w_small = jnp.concatenate([seg(2), seg(3), seg(6)], axis=1).astype(BF16)
    z_col = 3 * heads
    nq_col = z_col + heads
    kv_col = nq_col + nsa_heads
    gate_col0 = (kv_col + 6 * groups) * d

    x2 = x.reshape(n, dm)
    xn = _rmsnorm(x2, g_mix, BF16)
    n_experts, _, ff2 = w_gate_up.shape
    proj, wgu_bf, wd_bf = _matmul_with_casts(
        xn, w_big, BF16, tm=1024, tn=512, name="in_proj",
        riders=[w_gate_up.reshape(n_experts * dm, ff2), w_down.reshape(n_experts * (ff2 // 2), dm)])
    wgu_bf = wgu_bf.reshape(w_gate_up.shape)
    wd_bf = wd_bf.reshape(w_down.shape)
    small = _matmul(xn, w_small, F32, tm=1024, tn=w_small.shape[1], name="in_proj_small")
    a_raw, b_raw, nsa_gate = small[:, :heads], small[:, heads:2 * heads], small[:, 2 * heads:]

    gc, beta, gl = _gdn_gates(a_raw, b_raw, a_log, dt_bias, GDN_CHUNK)
    gct = gc.reshape(batch, seq, heads).transpose(0, 2, 1).reshape(batch * heads, 1, seq)
    o_gdn = _gdn_mixer(proj, conv_w, gc, beta, gl, gct, norm_w, batch, seq, heads, GDN_CHUNK)

    k_cmp, v_cmp = _nsa_compress(proj, kv_col, kv_col + groups, pe_k, w1_k, w2_k, pe_v, w1_v, w2_v,
                                 batch, seq, groups)
    gates = nsa_gate.reshape(batch, seq, groups, 3 * hpg).transpose(0, 2, 1, 3).reshape(batch * groups, seq, 3 * hpg)
    o_nsa = _nsa_mixer(proj, nq_col, kv_col + 2 * groups, kv_col + 3 * groups, kv_col + 4 * groups,
                       kv_col + 5 * groups, k_cmp, v_cmp, gates, batch, seq, groups, hpg)

    merged = _merge(o_gdn, o_nsa, w_a.astype(BF16), w_b.astype(BF16), proj, gate_col0)
    h = _resid_matmul(merged, w_out.astype(BF16), x2)

    xn2, ids, wts, counts = _router(h, g_ffn, w_group, b_group, w_expert, b_expert)
    dest, blk_expert, n_active, n_blk = _dispatch_plan(ids, counts, MOE_BLOCK)
    tok_buf = _slot_scatter(dest.reshape(n * TOP_K), n_blk * MOE_BLOCK, TOP_K)
    y_buf = _experts(xn2, wgu_bf, wd_bf, blk_expert, tok_buf, n_active, MOE_BLOCK)
    tc = 128
    pos_tiles = dest.reshape(n // tc, tc, TOP_K).transpose(0, 2, 1).reshape(n * TOP_K)
    out = _combine(h, y_buf, pos_tiles, wts, g_out, tc)
    return out.reshape(batch, seq, dm)


def kernel(x, g_mix, w_in, gdn_conv_w, gdn_a_log, gdn_dt_bias, gdn_norm_w, cmp_pe_k, cmp_w1_k, cmp_w2_k,
           cmp_pe_v, cmp_w1_v, cmp_w2_v, w_branch_gdn, w_branch_nsa, w_out, g_ffn, w_group, b_group,
           w_expert, b_expert, w_gate_up, w_down, g_final):
    depth = g_mix.shape[0]
    assert depth == 1, "the fused final norm assumes a single layer"
    return _layer(x, g_mix[0], w_in[0], gdn_conv_w[0], gdn_a_log[0], gdn_dt_bias[0], gdn_norm_w[0],
                  cmp_pe_k[0], cmp_w1_k[0], cmp_w2_k[0], cmp_pe_v[0], cmp_w1_v[0], cmp_w2_v[0],
                  w_branch_gdn[0], w_branch_nsa[0], w_out[0], g_ffn[0], w_group[0], b_group[0],
                  w_expert[0], b_expert[0], w_gate_up[0], w_down[0], g_final)
```

```python
import functools
import math

import numpy as np
import jax
import jax.numpy as jnp
from jax import lax
from jax.experimental import pallas as pl
from jax.experimental.pallas import tpu as pltpu

F32 = jnp.float32
BF16 = jnp.bfloat16

RMS_EPS = 1e-6
NEG_INF = -1e30
FORCE_SCORE = 1e6

HEAD_DIM = 128
GDN_CHUNK = 64
NSA_GROUPS = 4
CMP_BLOCK = 32
CMP_STRIDE = 16
SEL_BLOCK = 64
SEL_TOP_N = 16
WINDOW = 512
N_GROUPS = 8
EXPERTS_PER_GROUP = 8
TOP_K = 2
MOE_BLOCK = 128
LANES = 128

V7X_VMEM_BYTES = 64 * 1024 * 1024


def _cparams(semantics, vmem_mb):
    assert vmem_mb * 1024 * 1024 < V7X_VMEM_BYTES
    return pltpu.CompilerParams(dimension_semantics=semantics, vmem_limit_bytes=vmem_mb * 1024 * 1024)


def _dot(a, b, **kw):
    return jnp.dot(a, b, preferred_element_type=F32, **kw)


def _dot_nt(a, b):
    return lax.dot_general(a, b, (((1,), (1,)), ((), ())), preferred_element_type=F32)


def _dot_tn(a, b):
    return lax.dot_general(a, b, (((0,), (0,)), ((), ())), preferred_element_type=F32)


def _sigmoid(x):
    return 1.0 / (1.0 + jnp.exp(-x))


def _silu(x):
    return x * _sigmoid(x)


def _rmsnorm_kernel(x_ref, g_ref, o_ref):
    x = x_ref[...].astype(F32)
    ms = jnp.mean(x * x, axis=-1, keepdims=True)
    o_ref[...] = (x * lax.rsqrt(ms + RMS_EPS) * g_ref[...]).astype(o_ref.dtype)


def _rmsnorm(x, gain, out_dtype, tm=512):
    n, d = x.shape
    return pl.pallas_call(
        _rmsnorm_kernel,
        out_shape=jax.ShapeDtypeStruct((n, d), out_dtype),
        grid=(n // tm,),
        in_specs=[pl.BlockSpec((tm, d), lambda i: (i, 0)), pl.BlockSpec((1, d), lambda i: (0, 0))],
        out_specs=pl.BlockSpec((tm, d), lambda i: (i, 0)),
        compiler_params=_cparams(("parallel",), 40),
        name="rmsnorm",
    )(x, gain.reshape(1, d).astype(F32))


def _mm_kernel(x_ref, w_ref, o_ref):
    o_ref[...] = _dot(x_ref[...], w_ref[...]).astype(o_ref.dtype)


def _matmul(x, w, out_dtype, tm, tn, name):
    m, k = x.shape
    n = w.shape[1]
    return pl.pallas_call(
        _mm_kernel,
        out_shape=jax.ShapeDtypeStruct((m, n), out_dtype),
        grid=(m // tm, n // tn),
        in_specs=[pl.BlockSpec((tm, k), lambda i, j: (i, 0)), pl.BlockSpec((k, tn), lambda i, j: (0, j))],
        out_specs=pl.BlockSpec((tm, tn), lambda i, j: (i, j)),
        compiler_params=_cparams(("parallel", "arbitrary"), 48),
        name=name,
    )(x, w)


def _pack_cols_kernel(shift_ref, base_ref, *refs, shifts, tn):
    o_ref = refs[-1]
    wide = jnp.concatenate([r[...] for r in refs[:-1]], axis=1)
    j = pl.program_id(1)
    for s in shifts:
        @pl.when(shift_ref[j] == s)
        def _(s=s):
            o_ref[...] = wide[:, s:s + tn].astype(o_ref.dtype)


def _pack_columns(w, starts, tn=512, tr=512):
    k, n_in = w.shape
    nb = tn // LANES + 1
    starts = np.asarray(starts)
    assert np.all(starts + tn <= n_in) and k % tr == 0
    base = jnp.asarray(starts // LANES, jnp.int32)
    shift = jnp.asarray(starts % LANES, jnp.int32)
    shifts = tuple(sorted(set(int(s) for s in starts % LANES)))
    last_blk = (n_in - 1) // LANES
    in_specs = [pl.BlockSpec((tr, LANES), lambda i, j, sh, bs, b=b: (i, jnp.minimum(bs[j] + b, last_blk)))
                for b in range(nb)]
    return pl.pallas_call(
        functools.partial(_pack_cols_kernel, shifts=shifts, tn=tn),
        out_shape=jax.ShapeDtypeStruct((k, len(starts) * tn), BF16),
        grid_spec=pltpu.PrefetchScalarGridSpec(
            num_scalar_prefetch=2,
            grid=(k // tr, len(starts)),
            in_specs=in_specs,
            out_specs=pl.BlockSpec((tr, tn), lambda i, j, sh, bs: (i, j))),
        compiler_params=_cparams(("parallel", "arbitrary"), 32),
        name="pack_in_proj_weights",
    )(shift, base, *([w] * nb))


def _mm_cast_kernel(x_ref, w_ref, *refs, n_riders):
    for r in range(n_riders):
        refs[n_riders + 1 + r][...] = refs[r][...].astype(BF16)
    o_ref = refs[n_riders]
    o_ref[...] = _dot(x_ref[...], w_ref[...]).astype(o_ref.dtype)


def _rider_rows(n_rows, max_blocks):
    rb = 16
    while n_rows % rb or n_rows // rb > max_blocks:
        rb += 16
    return rb


def _matmul_with_casts(x, w, n, out_dtype, tm, tn, name, riders):
    m, k = x.shape
    ni, nj = m // tm, n // tn
    in_specs = [pl.BlockSpec((tm, k), lambda i, j: (i, 0)), pl.BlockSpec((k, tn), lambda i, j: (0, j))]
    out_specs = [pl.BlockSpec((tm, tn), lambda i, j: (i, j))]
    out_shape = [jax.ShapeDtypeStruct((m, n), out_dtype)]
    rider_specs = []
    for r in riders:
        rows, cols = r.shape
        rb = _rider_rows(rows, ni * nj)
        last = rows // rb - 1
        spec = pl.BlockSpec((rb, cols), lambda i, j, last=last: (jnp.minimum(i * nj + j, last), 0))
        rider_specs.append(spec)
        out_shape.append(jax.ShapeDtypeStruct((rows, cols), BF16))
    outs = pl.pallas_call(
        functools.partial(_mm_cast_kernel, n_riders=len(riders)),
        out_shape=tuple(out_shape),
        grid=(ni, nj),
        in_specs=in_specs[:2] + rider_specs,
        out_specs=tuple(out_specs + rider_specs),
        compiler_params=_cparams(("arbitrary", "arbitrary"), 48),
        name=name,
    )(x, w, *riders)
    return outs


def _merge_kernel(a_ref, b_ref, wa_ref, wb_ref, ga_ref, gb_ref, o_ref):
    ya = _dot(a_ref[...], wa_ref[...])
    yb = _dot(b_ref[...], wb_ref[...])
    o_ref[...] = (_sigmoid(ga_ref[...].astype(F32)) * ya + _sigmoid(gb_ref[...].astype(F32)) * yb).astype(o_ref.dtype)


def _merge(o_gdn, o_nsa, w_a, w_b, proj, gate_col0, tm=1024, tn=512):
    n, ka = o_gdn.shape
    kb = o_nsa.shape[1]
    d = w_a.shape[1]
    ja, jb = gate_col0 // tn, (gate_col0 + d) // tn
    return pl.pallas_call(
        _merge_kernel,
        out_shape=jax.ShapeDtypeStruct((n, d), BF16),
        grid=(n // tm, d // tn),
        in_specs=[pl.BlockSpec((tm, ka), lambda i, j: (i, 0)),
                  pl.BlockSpec((tm, kb), lambda i, j: (i, 0)),
                  pl.BlockSpec((ka, tn), lambda i, j: (0, j)),
                  pl.BlockSpec((kb, tn), lambda i, j: (0, j)),
                  pl.BlockSpec((tm, tn), lambda i, j: (i, ja + j)),
                  pl.BlockSpec((tm, tn), lambda i, j: (i, jb + j))],
        out_specs=pl.BlockSpec((tm, tn), lambda i, j: (i, j)),
        compiler_params=_cparams(("parallel", "arbitrary"), 48),
        name="branch_merge",
    )(o_gdn, o_nsa, w_a, w_b, proj, proj)


def _resid_mm_kernel(x_ref, w_ref, r_ref, o_ref):
    o_ref[...] = r_ref[...] + _dot(x_ref[...], w_ref[...])


def _resid_matmul(x, w, resid, tm=1024, tn=512):
    m, k = x.shape
    n = w.shape[1]
    return pl.pallas_call(
        _resid_mm_kernel,
        out_shape=jax.ShapeDtypeStruct((m, n), F32),
        grid=(m // tm, n // tn),
        in_specs=[pl.BlockSpec((tm, k), lambda i, j: (i, 0)),
                  pl.BlockSpec((k, tn), lambda i, j: (0, j)),
                  pl.BlockSpec((tm, tn), lambda i, j: (i, j))],
        out_specs=pl.BlockSpec((tm, tn), lambda i, j: (i, j)),
        compiler_params=_cparams(("parallel", "arbitrary"), 48),
        name="out_proj_residual",
    )(x, w, resid)


def _gdn_gate_kernel(a_ref, b_ref, alog_ref, dtb_ref, gc_ref, beta_ref, gl_ref, *, chunk):
    tm = a_ref.shape[0]
    x = a_ref[...] + dtb_ref[...]
    softplus = jnp.maximum(x, 0.0) + jnp.log(1.0 + jnp.exp(-jnp.abs(x)))
    g = -jnp.exp(alog_ref[...]) * softplus
    row = lax.broadcasted_iota(jnp.int32, (tm, tm), 0)
    col = lax.broadcasted_iota(jnp.int32, (tm, tm), 1)
    same = row // chunk == col // chunk
    tri = jnp.where((col <= row) & same, 1.0, 0.0).astype(F32)
    gc_ref[...] = _dot(tri, g, precision=lax.Precision.HIGHEST)
    gl_ref[...] = _dot(jnp.where(same, 1.0, 0.0).astype(F32), g, precision=lax.Precision.HIGHEST)
    beta_ref[...] = _sigmoid(b_ref[...])


def _gdn_gates(a_raw, b_raw, a_log, dt_bias, chunk, tm=512):
    n, h = a_raw.shape
    spec = pl.BlockSpec((tm, h), lambda i: (i, 0))
    vec = pl.BlockSpec((1, h), lambda i: (0, 0))
    sds = jax.ShapeDtypeStruct((n, h), F32)
    return pl.pallas_call(
        functools.partial(_gdn_gate_kernel, chunk=chunk),
        out_shape=(sds, sds, sds),
        grid=(n // tm,),
        in_specs=[spec, spec, vec, vec],
        out_specs=(spec, spec, spec),
        compiler_params=_cparams(("parallel",), 32),
        name="gdn_gates",
    )(a_raw, b_raw, a_log.reshape(1, h).astype(F32), dt_bias.reshape(1, h).astype(F32))


_CONV_PAD = 8


def _gdn_kernel(q_ref, k_ref, v_ref, z_ref, cwq_ref, cwk_ref, cwv_ref, gc_ref, beta_ref, gl_ref, gct_ref, nw_ref,
                o_ref, state, xbuf, *, chunk, hps):
    hblk = pl.program_id(1)
    t = pl.program_id(2)
    tt = q_ref.shape[0]
    d = HEAD_DIM
    width = cwq_ref.shape[0]
    c = chunk
    n_ch = tt // c
    n_lvl = int(math.log2(c))
    assert 2 ** n_lvl == c

    @pl.when(t == 0)
    def _():
        state[...] = jnp.zeros_like(state)
        xbuf[:, 0:_CONV_PAD, :] = jnp.zeros((3 * hps, _CONV_PAD, d), F32)

    def conv_silu(s, x_bf16, w):
        xbuf[s, _CONV_PAD:_CONV_PAD + tt, :] = x_bf16.astype(F32)
        y = None
        for j in range(width):
            term = xbuf[s, pl.ds(_CONV_PAD - (width - 1) + j, tt), :] * w[j:j + 1, :]
            y = term if y is None else y + term
        xbuf[s, 0:_CONV_PAD, :] = xbuf[s, tt:tt + _CONV_PAD, :]
        return _silu(y)

    def l2n(x):
        return x * lax.rsqrt(jnp.sum(x * x, axis=-1, keepdims=True) + RMS_EPS)

    ri = lax.broadcasted_iota(jnp.int32, (tt, tt), 0)
    ci = lax.broadcasted_iota(jnp.int32, (tt, tt), 1)
    incl = (ri >= ci) & (ri // c == ci // c)
    strict = ri > ci
    lane = lax.broadcasted_iota(jnp.int32, gc_ref.shape, 1)
    nw = nw_ref[...]

    hd = [dict() for _ in range(hps)]
    for hp, e in enumerate(hd):
        h = hblk * hps + hp
        hs = slice(hp * d, (hp + 1) * d)
        q = l2n(conv_silu(3 * hp, q_ref[:, hs], cwq_ref[:, hs])) * (d ** -0.5)
        k = l2n(conv_silu(3 * hp + 1, k_ref[:, hs], cwk_ref[:, hs]))
        v = conv_silu(3 * hp + 2, v_ref[:, hs], cwv_ref[:, hs])
        pick = lambda ref: jnp.sum(jnp.where(lane == h, ref[...], 0.0), axis=1, keepdims=True)
        gcc, bc, glc = pick(gc_ref), pick(beta_ref), pick(gl_ref)
        egc = jnp.exp(gcc)
        kb = k.astype(BF16)
        qk = _dot_nt(jnp.concatenate([q.astype(BF16), kb], axis=0), kb)
        dm = jnp.exp(jnp.where(incl, gcc - gct_ref[hp], NEG_INF))
        e["npow"] = jnp.where(strict, -(qk[tt:] * dm * bc), 0.0)
        e["x"] = jnp.concatenate([v * bc, k * (bc * egc)], axis=1)
        e["attn"] = (qk[:tt] * dm).astype(BF16)
        e["qe"] = q * egc
        e["kout"] = (k * jnp.exp(glc - gcc)).astype(BF16)
        e["dch"] = jnp.exp(glc)

    for lvl in range(n_lvl):
        for e in hd:
            nb = e["npow"].astype(BF16)
            if lvl + 1 < n_lvl:
                r = _dot(nb, jnp.concatenate([nb, e["x"].astype(BF16)], axis=1))
                e["npow"], e["x"] = r[:, :tt], e["x"] + r[:, tt:]
            else:
                e["x"] = e["x"] + _dot(nb, e["x"].astype(BF16))

    for hp, e in enumerate(hd):
        e["xb"] = e["x"].astype(BF16)
        au = _dot(e["attn"], e["xb"])
        e["o0"] = au[:, :d]
        e["qe"] = (e["qe"] - au[:, d:]).astype(BF16)
        e["s"] = state[hp]

    for ch in range(n_ch):
        sl = slice(ch * c, (ch + 1) * c)
        for hp, e in enumerate(hd):
            hs = slice(hp * d, (hp + 1) * d)
            ktx = _dot_tn(e["kout"][sl], e["xb"][sl])
            sb = e["s"].astype(BF16)
            o = _dot(e["qe"][sl], sb) + e["o0"][sl]
            e["s"] = e["s"] * e["dch"][ch * c:ch * c + 1, :] - _dot(ktx[:, d:].astype(BF16), sb) + ktx[:, :d]
            on = o * lax.rsqrt(jnp.mean(o * o, axis=-1, keepdims=True) + RMS_EPS) * nw
            o_ref[sl, hs] = (on * _silu(z_ref[sl, hs].astype(F32))).astype(o_ref.dtype)

    for hp, e in enumerate(hd):
        state[hp] = e["s"]


def _gdn_mixer(proj, conv_w, gc, beta, gl, gct, norm_w, batch, seq, heads, chunk, tt=256, hps=4):
    d = HEAD_DIM
    n = batch * seq
    nt = seq // tt
    hb = heads // hps
    width = conv_w.shape[0]
    assert width - 1 <= _CONV_PAD and tt % chunk == 0 and seq % tt == 0 and heads % hps == 0

    def slab(off):
        return pl.BlockSpec((tt, hps * d), lambda b, h, t: (b * nt + t, off + h))

    def cw(off):
        return pl.BlockSpec((width, hps * d), lambda b, h, t: (0, off + h))

    gspec = pl.BlockSpec((tt, heads), lambda b, h, t: (b * nt + t, 0))
    return pl.pallas_call(
        functools.partial(_gdn_kernel, chunk=chunk, hps=hps),
        out_shape=jax.ShapeDtypeStruct((n, heads * d), BF16),
        grid=(batch, hb, nt),
        in_specs=[slab(0), slab(hb), slab(2 * hb), slab(3 * hb),
                  cw(0), cw(hb), cw(2 * hb),
                  gspec, gspec, gspec,
                  pl.BlockSpec((hps, 1, tt), lambda b, h, t: (b * hb + h, 0, t)),
                  pl.BlockSpec((1, d), lambda b, h, t: (0, 0))],
        out_specs=pl.BlockSpec((tt, hps * d), lambda b, h, t: (b * nt + t, h)),
        scratch_shapes=[pltpu.VMEM((hps, d, d), F32), pltpu.VMEM((3 * hps, tt + _CONV_PAD, d), F32)],
        compiler_params=_cparams(("parallel", "parallel", "arbitrary"), 40),
        name="gdn_mixer",
    )(proj, proj, proj, proj, conv_w, conv_w, conv_w, gc, beta, gl, gct, norm_w.reshape(1, d).astype(F32))


def _nsa_cmp_kernel(kc_ref, vc_ref, pek_ref, w1k_ref, w2k_ref, pev_ref, w1v_ref, w2v_ref,
                    ko_ref, vo_ref, xf, sh, *, stride, block):
    seq, d = kc_ref.shape
    nh = seq // stride
    reps = block // stride
    assert nh % 8 == 0

    def compress(src_ref, pe_ref, w1_ref, w2_ref, dst_ref):
        xf[...] = src_ref[...].astype(F32)
        pe = pe_ref[...]
        hid = None
        for r in range(reps):
            acc = None
            for l in range(stride):
                li = r * stride + l
                rows = xf[pl.ds(l, nh, stride=stride), :] + pe[li:li + 1, :]
                term = _dot(rows.astype(BF16), w1_ref[li].astype(BF16))
                acc = term if acc is None else acc + term
            if r == 0:
                hid = acc
            else:
                sh[0:nh, :] = acc
                sh[nh:nh + 8, :] = jnp.zeros((8, d), F32)
                hid = hid + sh[pl.ds(r, nh), :]
        hid = _silu(hid)
        dst_ref[0] = _dot(hid.astype(BF16), w2_ref[...].astype(BF16)).astype(dst_ref.dtype)

    compress(kc_ref, pek_ref, w1k_ref, w2k_ref, ko_ref)
    compress(vc_ref, pev_ref, w1v_ref, w2v_ref, vo_ref)


def _nsa_compress(proj, kc_col, vc_col, pe_k, w1_k, w2_k, pe_v, w1_v, w2_v, batch, seq, groups):
    d = HEAD_DIM
    nh = seq // CMP_STRIDE
    full2 = lambda b, g: (0, 0)
    full3 = lambda b, g: (0, 0, 0)
    out_sds = jax.ShapeDtypeStruct((batch * groups, nh, d), BF16)
    out_spec = pl.BlockSpec((1, nh, d), lambda b, g: (b * groups + g, 0, 0))
    return pl.pallas_call(
        functools.partial(_nsa_cmp_kernel, stride=CMP_STRIDE, block=CMP_BLOCK),
        out_shape=(out_sds, out_sds),
        grid=(batch, groups),
        in_specs=[pl.BlockSpec((seq, d), lambda b, g: (b, kc_col + g)),
                  pl.BlockSpec((seq, d), lambda b, g: (b, vc_col + g)),
                  pl.BlockSpec((CMP_BLOCK, d), full2), pl.BlockSpec((CMP_BLOCK, d, d), full3), pl.BlockSpec((d, d), full2),
                  pl.BlockSpec((CMP_BLOCK, d), full2), pl.BlockSpec((CMP_BLOCK, d, d), full3), pl.BlockSpec((d, d), full2)],
        out_specs=(out_spec, out_spec),
        scratch_shapes=[pltpu.VMEM((seq, d), F32), pltpu.VMEM((nh + 8, d), F32)],
        compiler_params=_cparams(("parallel", "parallel"), 32),
        name="nsa_compress",
    )(proj, proj, pe_k, w1_k, w2_k, pe_v, w1_v, w2_v)


def _nsa_kernel(q_ref, kc_ref, vc_ref, ks_ref, vs_ref, kw_ref, vw_ref, gate_ref, selmap_ref, expand_ref,
                o_ref, score_t, *, hpg, gps, tk, n_sel, n_top):
    tq = q_ref.shape[0]
    d = HEAD_DIM
    gw = hpg * d
    n_cmp_pad = kc_ref.shape[1]
    t0 = pl.program_id(2) * tq
    rows = hpg * tq
    grp = list(range(gps))
    kv = lambda ref, gi, r0, n: ref[pl.ds(r0, n), gi * d:(gi + 1) * d]

    q4 = []
    for gi in grp:
        qg = jnp.concatenate([q_ref[:, gi * gw + hh * d:gi * gw + (hh + 1) * d] for hh in range(hpg)], axis=0)
        q4.append((qg.astype(F32) * (d ** -0.5 * math.log2(math.e))).astype(BF16))
    trow = t0 + lax.broadcasted_iota(jnp.int32, (tq, 1), 0)

    ncol = lax.broadcasted_iota(jnp.int32, (1, n_cmp_pad), 1)
    vis_c = ((ncol * CMP_STRIDE + (CMP_BLOCK - 1)) <= trow)[None]
    s_c = [_dot_nt(q4[gi], kc_ref[gi]).reshape(hpg, tq, n_cmp_pad) for gi in grp]
    p_c = []
    for gi in grp:
        s3 = jnp.where(vis_c, s_c[gi], NEG_INF)
        e = jnp.where(vis_c, jnp.exp2(s3 - jnp.max(s3, axis=-1, keepdims=True)), 0.0)
        den = jnp.sum(e, axis=-1, keepdims=True)
        p_c.append(e / jnp.where(den > 0.0, den, 1.0))
    o_cmp = [_dot(p_c[gi].reshape(rows, n_cmp_pad).astype(BF16), vc_ref[gi]).reshape(hpg, tq, d) for gi in grp]

    jrow = lax.broadcasted_iota(jnp.int32, (n_sel, 1), 0)
    tcol = t0 + lax.broadcasted_iota(jnp.int32, (1, tq), 1)
    cur = tcol // SEL_BLOCK
    forced = (jrow == 0) | (jrow == cur) | (jrow == cur - 1)
    causal_blk = jrow * SEL_BLOCK <= tcol
    keys = []
    for gi in grp:
        imp_t = lax.dot_general(selmap_ref[...], jnp.sum(p_c[gi], axis=0), (((1,), (1,)), ((), ())),
                                preferred_element_type=F32, precision=lax.Precision.HIGHEST)
        sc = jnp.where(forced, FORCE_SCORE, jnp.where(causal_blk, imp_t, NEG_INF))
        bits = pltpu.bitcast(sc, jnp.int32)
        keys.append(bits ^ ((bits >> 31) & jnp.int32(0x7FFFFFFF)))
        score_t[gi] = keys[gi]

    def rank_step(i, cnts):
        tie = jnp.where(jrow > i, 1, 0)
        return tuple(cnts[gi] + jnp.where(score_t[gi, pl.ds(i, 1), :] + tie > keys[gi], 1.0, 0.0) for gi in grp)

    n_rank = jnp.minimum((t0 + tq - 1) // SEL_BLOCK + 1, n_sel)
    cnts = lax.fori_loop(0, n_rank, rank_step, tuple(jnp.zeros((n_sel, tq), F32) for _ in grp))
    eye = jnp.where(lax.broadcasted_iota(jnp.int32, (n_sel, n_sel), 0)
                    == lax.broadcasted_iota(jnp.int32, (n_sel, n_sel), 1), 1.0, 0.0).astype(BF16)
    sel_bias = [_dot_tn(jnp.where(cnts[gi] < n_top, 0.0, NEG_INF).astype(BF16), eye).astype(BF16) for gi in grp]

    def sel_step(kt, carry, diagonal):
        k0 = pl.multiple_of(kt * tk, tk)
        ex = expand_ref[:, pl.ds(k0, tk)]
        s3 = []
        for gi in grp:
            b = _dot(sel_bias[gi], ex)
            if diagonal:
                kpos = k0 + lax.broadcasted_iota(jnp.int32, (1, tk), 1)
                b = jnp.where(kpos <= trow, b, NEG_INF)
            s3.append(_dot_nt(q4[gi], kv(ks_ref, gi, k0, tk)).reshape(hpg, tq, tk) + b[None])
        out = []
        for gi in grp:
            m, acc = carry[gi]
            m_new = jnp.maximum(m, jnp.max(s3[gi], axis=-1, keepdims=True))
            alpha = jnp.exp2(m - m_new)
            e = jnp.exp2((s3[gi] - m_new).astype(BF16))
            pv = _dot(e.reshape(rows, tk), with_ones(kv(vs_ref, gi, k0, tk)))
            out.append((m_new, alpha * acc + pv.reshape(hpg, tq, 2 * d)))
        return tuple(out)

    def with_ones(v_tile):
        return jnp.concatenate([v_tile, jnp.ones(v_tile.shape, BF16)], axis=1)

    n_full = t0 // tk
    init = (jnp.full((hpg, tq, 1), NEG_INF, F32), jnp.zeros((hpg, tq, 2 * d), F32))
    carry = lax.fori_loop(0, n_full, functools.partial(sel_step, diagonal=False), tuple(init for _ in grp))
    carry = sel_step(n_full, carry, diagonal=True)
    o_sel = [carry[gi][1][:, :, :d] / carry[gi][1][:, :, d:] for gi in grp]

    span = WINDOW + tq
    ws = pl.multiple_of(jnp.maximum(t0 - WINDOW, 0), tq)
    kpos = ws + lax.broadcasted_iota(jnp.int32, (1, span), 1)
    rel = trow - kpos
    b_w = jnp.where((rel >= 0) & (rel < WINDOW), 0.0, NEG_INF)[None]
    s_w = [_dot_nt(q4[gi], kv(kw_ref, gi, ws, span)).reshape(hpg, tq, span) + b_w for gi in grp]
    o_win = []
    for gi in grp:
        e = jnp.exp2((s_w[gi] - jnp.max(s_w[gi], axis=-1, keepdims=True)).astype(BF16))
        pv = _dot(e.reshape(rows, span), with_ones(kv(vw_ref, gi, ws, span))).reshape(hpg, tq, 2 * d)
        o_win.append(pv[:, :, :d] / pv[:, :, d:])

    for gi in grp:
        gates = _sigmoid(gate_ref[gi])
        for hh in range(hpg):
            g0 = gates[:, 3 * hh:3 * hh + 1]
            g1 = gates[:, 3 * hh + 1:3 * hh + 2]
            g2 = gates[:, 3 * hh + 2:3 * hh + 3]
            o_ref[:, gi * gw + hh * d:gi * gw + (hh + 1) * d] = (
                g0 * o_cmp[gi][hh] + g1 * o_sel[gi][hh] + g2 * o_win[gi][hh]).astype(o_ref.dtype)


def _nsa_mixer(proj, q_col, ks_col, vs_col, kw_col, vw_col, k_cmp, v_cmp, gates, batch, seq, groups, hpg,
               tq=256, tk=1024, gps=2):
    d = HEAD_DIM
    n = batch * seq
    nq = seq // tq
    n_sel = seq // SEL_BLOCK
    n_top = min(SEL_TOP_N, n_sel)
    n_cmp = (seq - CMP_BLOCK) // CMP_STRIDE + 1
    n_cmp_pad = k_cmp.shape[1]
    assert n_sel % 8 == 0 and seq >= WINDOW + tq and WINDOW % tq == 0 and tk % tq == 0 and seq % tk == 0

    cmp_start = np.arange(n_cmp_pad) * CMP_STRIDE
    sel_start = np.arange(n_sel) * SEL_BLOCK
    overlap = (np.minimum(cmp_start[None, :] + CMP_BLOCK, sel_start[:, None] + SEL_BLOCK)
               - np.maximum(cmp_start[None, :], sel_start[:, None]))
    sel_map_t = np.clip(overlap, 0, None) / CMP_STRIDE
    sel_map_t[:, n_cmp:] = 0.0
    expand = (np.arange(seq)[None, :] // SEL_BLOCK == np.arange(n_sel)[:, None]).astype(np.float32)

    gb = groups // gps
    qw = gps * hpg * d
    assert groups % gps == 0 and all(c % gps == 0 for c in (ks_col, vs_col, kw_col, vw_col)) and q_col % (gps * hpg) == 0

    def kv(col):
        return pl.BlockSpec((seq, gps * d), lambda b, g, t: (b, col // gps + g))

    cmp_spec = pl.BlockSpec((gps, n_cmp_pad, d), lambda b, g, t: (b * gb + g, 0, 0))
    return pl.pallas_call(
        functools.partial(_nsa_kernel, hpg=hpg, gps=gps, tk=tk, n_sel=n_sel, n_top=n_top),
        out_shape=jax.ShapeDtypeStruct((n, groups * hpg * d), BF16),
        grid=(batch, gb, nq),
        in_specs=[pl.BlockSpec((tq, qw), lambda b, g, t: (b * nq + t, q_col // (gps * hpg) + g)),
                  cmp_spec, cmp_spec, kv(ks_col), kv(vs_col), kv(kw_col), kv(vw_col),
                  pl.BlockSpec((gps, tq, 3 * hpg), lambda b, g, t: (b * gb + g, t, 0)),
                  pl.BlockSpec((n_sel, n_cmp_pad), lambda b, g, t: (0, 0)),
                  pl.BlockSpec((n_sel, seq), lambda b, g, t: (0, 0))],
        out_specs=pl.BlockSpec((tq, qw), lambda b, g, t: (b * nq + t, g)),
        scratch_shapes=[pltpu.VMEM((gps, n_sel, tq), jnp.int32)],
        compiler_params=_cparams(("parallel", "parallel", "arbitrary"), 48),
        name="nsa_attention",
    )(proj, k_cmp, v_cmp, proj, proj, proj, proj, gates, jnp.asarray(sel_map_t, F32), jnp.asarray(expand, BF16))


def _router_kernel(h_ref, g_ref, wr_ref, wrl_ref, br_ref, xn_ref, ids_ref, wts_ref, cnt_ref):
    x = h_ref[...]
    xn = x * lax.rsqrt(jnp.mean(x * x, axis=-1, keepdims=True) + RMS_EPS) * g_ref[...]
    xn_ref[...] = xn
    x_hi = xn.astype(BF16)
    x_lo = (xn - x_hi.astype(F32)).astype(BF16)
    logits = (_dot(x_hi, wr_ref[...]) + _dot(x_hi, wrl_ref[...]) + _dot(x_lo, wr_ref[...])) + br_ref[...]
    lane = lax.broadcasted_iota(jnp.int32, logits.shape, 1)
    big = jnp.int32(2 * LANES)
    is_g = lane < N_GROUPS
    gl = jnp.where(is_g, logits, NEG_INF)
    gm = jnp.max(gl, axis=-1, keepdims=True)
    p_g = 1.0 / jnp.sum(jnp.where(is_g, jnp.exp(gl - gm), 0.0), axis=-1, keepdims=True)
    grp = jnp.min(jnp.where(gl == gm, lane, big), axis=-1, keepdims=True)
    eidx = lane - N_GROUPS
    in_grp = (eidx >= 0) & (eidx // EXPERTS_PER_GROUP == grp) & (eidx < N_GROUPS * EXPERTS_PER_GROUP)
    el = jnp.where(in_grp, logits, NEG_INF)
    em = jnp.max(el, axis=-1, keepdims=True)
    ee = jnp.where(in_grp, jnp.exp(el - em), 0.0)
    pe = ee / jnp.sum(ee, axis=-1, keepdims=True)
    p1 = jnp.max(jnp.where(in_grp, pe, -1.0), axis=-1, keepdims=True)
    i1 = jnp.min(jnp.where(in_grp & (pe == p1), lane, big), axis=-1, keepdims=True)
    rest = in_grp & (lane != i1)
    p2 = jnp.max(jnp.where(rest, pe, -1.0), axis=-1, keepdims=True)
    i2 = jnp.min(jnp.where(rest & (pe == p2), lane, big), axis=-1, keepdims=True)
    denom = p1 + p2
    wts_ref[...] = jnp.where(lane == 0, p_g * p1 / denom, p_g * p2 / denom)

    @pl.when(pl.program_id(0) == 0)
    def _():
        cnt_ref[...] = jnp.zeros_like(cnt_ref)

    tm = logits.shape[0]
    hot = jnp.where((lane == i1) | (lane == i2), 1.0, 0.0)
    below = jnp.where(lax.broadcasted_iota(jnp.int32, (tm, tm), 0) > lax.broadcasted_iota(jnp.int32, (tm, tm), 1),
                      1.0, 0.0).astype(BF16)
    before = _dot(below, hot.astype(BF16)) + cnt_ref[...]
    r1 = jnp.sum(jnp.where(lane == i1, before, 0.0), axis=-1, keepdims=True).astype(jnp.int32)
    r2 = jnp.sum(jnp.where(lane == i2, before, 0.0), axis=-1, keepdims=True).astype(jnp.int32)
    cnt_ref[...] += jnp.sum(hot, axis=0, keepdims=True)
    ids_ref[...] = jnp.where(lane == 0, i1 - N_GROUPS, jnp.where(lane == 1, i2 - N_GROUPS, jnp.where(lane == 2, r1, r2)))


def _router(h, g_ffn, w_group, b_group, w_expert, b_expert, tm=256):
    n, d = h.shape
    n_g, n_e = w_group.shape[1], w_expert.shape[1]
    n_r = n_g + n_e
    assert n_r <= LANES and TOP_K == 2 and n_g == N_GROUPS
    wr = jnp.concatenate([w_group, w_expert, jnp.zeros((d, LANES - n_r), F32)], axis=1).astype(F32)
    wr_hi = wr.astype(BF16)
    wr_lo = (wr - wr_hi.astype(F32)).astype(BF16)
    br = jnp.concatenate([b_group, b_expert, jnp.zeros((LANES - n_r,), F32)]).reshape(1, LANES).astype(F32)
    row = lambda i: (i, 0)
    fixed = lambda i: (0, 0)
    xn, ids, wts, cnt = pl.pallas_call(
        _router_kernel,
        out_shape=(jax.ShapeDtypeStruct((n, d), F32), jax.ShapeDtypeStruct((n, LANES), jnp.int32),
                   jax.ShapeDtypeStruct((n, LANES), F32), jax.ShapeDtypeStruct((1, LANES), F32)),
        grid=(n // tm,),
        in_specs=[pl.BlockSpec((tm, d), row), pl.BlockSpec((1, d), fixed),
                  pl.BlockSpec((d, LANES), fixed), pl.BlockSpec((d, LANES), fixed), pl.BlockSpec((1, LANES), fixed)],
        out_specs=(pl.BlockSpec((tm, d), row), pl.BlockSpec((tm, LANES), row), pl.BlockSpec((tm, LANES), row),
                   pl.BlockSpec((1, LANES), fixed)),
        compiler_params=_cparams(("arbitrary",), 40),
        name="moe_router",
    )(h, g_ffn.reshape(1, d).astype(F32), wr_hi, wr_lo, br)
    return xn, ids, wts, cnt[0, n_g:n_r]


def _row_gather(idx_ref, base, src_hbm, dst, sem, n_rows, wait):
    for r in range(n_rows):
        cp = pltpu.make_async_copy(src_hbm.at[pl.ds(idx_ref[base + r], 1), :], dst.at[pl.ds(r, 1), :], sem)
        if wait:
            cp.wait()
        else:
            cp.start()


def _slot_scatter_kernel(dest_ref, tok_ref, *, top_k, unroll):
    n_out = tok_ref.shape[0]
    n_slot = dest_ref.shape[0]

    def clear(j, carry):
        for u in range(unroll):
            tok_ref[j * unroll + u] = 0
        return carry

    def place(j, carry):
        rows = [dest_ref[j * unroll + u] for u in range(unroll)]
        for u in range(unroll):
            tok_ref[rows[u]] = (j * unroll + u) // top_k
        return carry

    lax.fori_loop(0, n_out // unroll, clear, 0)
    lax.fori_loop(0, n_slot // unroll, place, 0)


def _slot_scatter(dest_flat, n_out, top_k, unroll=64):
    assert n_out % unroll == 0 and dest_flat.shape[0] % unroll == 0
    return pl.pallas_call(
        functools.partial(_slot_scatter_kernel, top_k=top_k, unroll=unroll),
        out_shape=jax.ShapeDtypeStruct((n_out,), jnp.int32),
        in_specs=[pl.BlockSpec(memory_space=pltpu.SMEM)],
        out_specs=pl.BlockSpec(memory_space=pltpu.SMEM),
        name="moe_slot_scatter",
    )(dest_flat)


def _expert_kernel(be_ref, tok_ref, nact_ref, x_hbm, wgu_ref, wd_ref, y_ref, xbuf, sem, *, blk, d_ff):
    i = pl.program_id(0)
    n_act = nact_ref[0]
    slot = i % 2

    @pl.when(i == 0)
    def _():
        _row_gather(tok_ref, 0, x_hbm, xbuf.at[0], sem.at[0], blk, wait=False)

    def run_block(prefetch_next):
        _row_gather(tok_ref, i * blk, x_hbm, xbuf.at[slot], sem.at[slot], blk, wait=True)
        if prefetch_next:
            _row_gather(tok_ref, (i + 1) * blk, x_hbm, xbuf.at[1 - slot], sem.at[1 - slot], blk, wait=False)
        xb = xbuf[slot].astype(BF16)
        gu = _dot(xb, wgu_ref[0])
        act = _silu(gu[:, :d_ff]) * gu[:, d_ff:]
        y_ref[...] = _dot(act.astype(BF16), wd_ref[0])

    pl.when(i + 1 < n_act)(functools.partial(run_block, True))
    pl.when(i + 1 == n_act)(functools.partial(run_block, False))

    @pl.when(i >= n_act)
    def _():
        y_ref[...] = jnp.zeros_like(y_ref)


def _experts(xn, w_gate_up, w_down, blk_expert, tok_buf, n_active, blk):
    n, d = xn.shape
    n_exp, _, ff2 = w_gate_up.shape
    d_ff = ff2 // 2
    n_blk = blk_expert.shape[0]
    return pl.pallas_call(
        functools.partial(_expert_kernel, blk=blk, d_ff=d_ff),
        out_shape=jax.ShapeDtypeStruct((n_blk * blk, d), F32),
        grid_spec=pltpu.PrefetchScalarGridSpec(
            num_scalar_prefetch=3,
            grid=(n_blk,),
            in_specs=[pl.BlockSpec(memory_space=pl.ANY),
                      pl.BlockSpec((1, d, ff2), lambda i, be, tok, na: (be[i], 0, 0)),
                      pl.BlockSpec((1, d_ff, d), lambda i, be, tok, na: (be[i], 0, 0))],
            out_specs=pl.BlockSpec((blk, d), lambda i, be, tok, na: (i, 0)),
            scratch_shapes=[pltpu.VMEM((2, blk, d), F32), pltpu.SemaphoreType.DMA((2,))]),
        compiler_params=_cparams(("arbitrary",), 56),
        name="moe_experts",
    )(blk_expert, tok_buf, n_active, xn, w_gate_up, w_down)


def _combine_kernel(pos_ref, h_ref, g_ref, w_ref, y_hbm, o_ref, ybuf, sem, *, tc):
    i = pl.program_id(0)
    n = pl.num_programs(0)
    slot = i % 2
    rows = TOP_K * tc

    @pl.when(i == 0)
    def _():
        _row_gather(pos_ref, 0, y_hbm, ybuf.at[0], sem.at[0], rows, wait=False)

    def run_tile(prefetch_next):
        _row_gather(pos_ref, i * rows, y_hbm, ybuf.at[slot], sem.at[slot], rows, wait=True)
        if prefetch_next:
            _row_gather(pos_ref, (i + 1) * rows, y_hbm, ybuf.at[1 - slot], sem.at[1 - slot], rows, wait=False)
        hh = h_ref[...]
        w = w_ref[...]
        for kk in range(TOP_K):
            hh = hh + w[:, kk:kk + 1] * ybuf[slot, kk * tc:(kk + 1) * tc, :]
        o_ref[...] = hh * lax.rsqrt(jnp.mean(hh * hh, axis=-1, keepdims=True) + RMS_EPS) * g_ref[...]

    pl.when(i + 1 < n)(functools.partial(run_tile, True))
    pl.when(i + 1 == n)(functools.partial(run_tile, False))


def _combine(h, y_buf, pos, wts, g_final, tc=128):
    n, d = h.shape
    return pl.pallas_call(
        functools.partial(_combine_kernel, tc=tc),
        out_shape=jax.ShapeDtypeStruct((n, d), F32),
        grid_spec=pltpu.PrefetchScalarGridSpec(
            num_scalar_prefetch=1,
            grid=(n // tc,),
            in_specs=[pl.BlockSpec((tc, d), lambda i, pos: (i, 0)),
                      pl.BlockSpec((1, d), lambda i, pos: (0, 0)),
                      pl.BlockSpec((tc, LANES), lambda i, pos: (i, 0)),
                      pl.BlockSpec(memory_space=pl.ANY)],
            out_specs=pl.BlockSpec((tc, d), lambda i, pos: (i, 0)),
            scratch_shapes=[pltpu.VMEM((2, TOP_K * tc, d), F32), pltpu.SemaphoreType.DMA((2,))]),
        compiler_params=_cparams(("arbitrary",), 40),
        name="moe_combine_final_norm",
    )(pos, h, g_final.reshape(1, d).astype(F32), wts, y_buf)


def _dispatch_plan(ids, counts, blk):
    n_tok = ids.shape[0]
    n_experts = counts.shape[0]
    n_slot = n_tok * TOP_K
    counts = counts.astype(jnp.int32)
    padded = (counts + blk - 1) // blk * blk
    pad_end = jnp.cumsum(padded)
    pad_start = pad_end - padded
    dest = pad_start[ids[:, :TOP_K]] + ids[:, TOP_K:2 * TOP_K]
    n_blk = -(-n_slot // blk) + n_experts
    blk_start = jnp.arange(n_blk, dtype=jnp.int32) * blk
    blk_expert = jnp.minimum(jnp.sum(pad_end[None, :] <= blk_start[:, None], axis=1), n_experts - 1).astype(jnp.int32)
    n_active = (pad_end[-1] // blk).astype(jnp.int32).reshape(1)
    return dest.astype(jnp.int32), blk_expert, n_active, n_blk


def _layer(x, g_mix, w_in, conv_w, a_log, dt_bias, norm_w, pe_k, w1_k, w2_k, pe_v, w1_v, w2_v,
           w_a, w_b, w_out, g_ffn, w_group, b_group, w_expert, b_expert, w_gate_up, w_down, g_out):
    batch, seq, dm = x.shape
    n = batch * seq
    d = HEAD_DIM
    gdn_w = w_a.shape[0]
    nsa_w = w_b.shape[0]
    heads = gdn_w // d
    nsa_heads = nsa_w // d
    groups = NSA_GROUPS
    hpg = nsa_heads // groups
    kvw = groups * d

    sizes = (3 * gdn_w, gdn_w, heads, heads, nsa_w, 6 * kvw, 3 * nsa_heads, 2 * dm)
    assert sum(sizes) == w_in.shape[1]
    offs = np.concatenate([[0], np.cumsum(sizes)])
    tn_in = 512
    wide = (0, 1, 4, 5, 7)
    assert all(sizes[i] % tn_in == 0 for i in wide) and sizes[2] + sizes[3] <= tn_in and sizes[6] <= tn_in
    starts = [c for i in wide for c in range(offs[i], offs[i + 1], tn_in)]
    n_big = len(starts) * tn_in
    w_pack = _pack_columns(w_in, starts + [offs[2], offs[6]], tn=tn_in)
    w_small = jnp.concatenate([w_pack[:, n_big:n_big + sizes[2] + sizes[3]],
                               w_pack[:, n_big + tn_in:n_big + tn_in + sizes[6]]], axis=1)
    z_col = 3 * heads
    nq_col = z_col + heads
    kv_col = nq_col + nsa_heads
    gate_col0 = (kv_col + 6 * groups) * d

    x2 = x.reshape(n, dm)
    xn = _rmsnorm(x2, g_mix, BF16)
    n_experts, _, ff2 = w_gate_up.shape
    proj, wgu_bf, wd_bf = _matmul_with_casts(
        xn, w_pack, n_big, BF16, tm=1024, tn=tn_in, name="in_proj",
        riders=[w_gate_up.reshape(n_experts * dm, ff2), w_down.reshape(n_experts * (ff2 // 2), dm)])
    wgu_bf = wgu_bf.reshape(w_gate_up.shape)
    wd_bf = wd_bf.reshape(w_down.shape)
    small = _matmul(xn, w_small, F32, tm=1024, tn=w_small.shape[1], name="in_proj_small")
    a_raw, b_raw, nsa_gate = small[:, :heads], small[:, heads:2 * heads], small[:, 2 * heads:]

    gc, beta, gl = _gdn_gates(a_raw, b_raw, a_log, dt_bias, GDN_CHUNK)
    gct = gc.reshape(batch, seq, heads).transpose(0, 2, 1).reshape(batch * heads, 1, seq)
    o_gdn = _gdn_mixer(proj, conv_w, gc, beta, gl, gct, norm_w, batch, seq, heads, GDN_CHUNK)

    k_cmp, v_cmp = _nsa_compress(proj, kv_col, kv_col + groups, pe_k, w1_k, w2_k, pe_v, w1_v, w2_v,
                                 batch, seq, groups)
    gates = nsa_gate.reshape(batch, seq, groups, 3 * hpg).transpose(0, 2, 1, 3).reshape(batch * groups, seq, 3 * hpg)
    o_nsa = _nsa_mixer(proj, nq_col, kv_col + 2 * groups, kv_col + 3 * groups, kv_col + 4 * groups,
                       kv_col + 5 * groups, k_cmp, v_cmp, gates, batch, seq, groups, hpg)

    merged = _merge(o_gdn, o_nsa, w_a.astype(BF16), w_b.astype(BF16), proj, gate_col0)
    h = _resid_matmul(merged, w_out.astype(BF16), x2)

    xn2, ids, wts, counts = _router(h, g_ffn, w_group, b_group, w_expert, b_expert)
    dest, blk_expert, n_active, n_blk = _dispatch_plan(ids, counts, MOE_BLOCK)
    tok_buf = _slot_scatter(dest.reshape(n * TOP_K), n_blk * MOE_BLOCK, TOP_K)
    y_buf = _experts(xn2, wgu_bf, wd_bf, blk_expert, tok_buf, n_active, MOE_BLOCK)
    tc = 128
    pos_tiles = dest.reshape(n // tc, tc, TOP_K).transpose(0, 2, 1).reshape(n * TOP_K)
    out = _combine(h, y_buf, pos_tiles, wts, g_out, tc)
    return out.reshape(batch, seq, dm)


def kernel(x, g_mix, w_in, gdn_conv_w, gdn_a_log, gdn_dt_bias, gdn_norm_w, cmp_pe_k, cmp_w1_k, cmp_w2_k,
           cmp_pe_v, cmp_w1_v, cmp_w2_v, w_branch_gdn, w_branch_nsa, w_out, g_ffn, w_group, b_group,
           w_expert, b_expert, w_gate_up, w_down, g_final):
    depth = g_mix.shape[0]
    assert depth == 1, "the fused final norm assumes a single layer"
    return _layer(x, g_mix[0], w_in[0], gdn_conv_w[0], gdn_a_log[0], gdn_dt_bias[0], gdn_norm_w[0],
                  cmp_pe_k[0], cmp_w1_k[0], cmp_w2_k[0], cmp_pe_v[0], cmp_w1_v[0], cmp_w2_v[0],
                  w_branch_gdn[0], w_branch_nsa[0], w_out[0], g_ffn[0], w_group[0], b_group[0],
                  w_expert[0], b_expert[0], w_gate_up[0], w_down[0], g_final)
```

```python
import functools
import math

import numpy as np
import jax
import jax.numpy as jnp
from jax import lax
from jax.experimental import pallas as pl
from jax.experimental.pallas import tpu as pltpu

F32 = jnp.float32
BF16 = jnp.bfloat16

RMS_EPS = 1e-6
NEG_INF = -1e30
FORCE_SCORE = 1e6

HEAD_DIM = 128
GDN_CHUNK = 64
NSA_GROUPS = 4
CMP_BLOCK = 32
CMP_STRIDE = 16
SEL_BLOCK = 64
SEL_TOP_N = 16
WINDOW = 512
N_GROUPS = 8
EXPERTS_PER_GROUP = 8
TOP_K = 2
MOE_BLOCK = 128
LANES = 128

V7X_VMEM_BYTES = 64 * 1024 * 1024


def _cparams(semantics, vmem_mb):
    assert vmem_mb * 1024 * 1024 < V7X_VMEM_BYTES
    return pltpu.CompilerParams(dimension_semantics=semantics, vmem_limit_bytes=vmem_mb * 1024 * 1024)


def _dot(a, b, **kw):
    return jnp.dot(a, b, preferred_element_type=F32, **kw)


def _dot_nt(a, b):
    return lax.dot_general(a, b, (((1,), (1,)), ((), ())), preferred_element_type=F32)


def _dot_tn(a, b):
    return lax.dot_general(a, b, (((0,), (0,)), ((), ())), preferred_element_type=F32)


def _sigmoid(x):
    return 1.0 / (1.0 + jnp.exp(-x))


def _silu(x):
    return x * _sigmoid(x)


def _rmsnorm_kernel(x_ref, g_ref, o_ref):
    x = x_ref[...].astype(F32)
    ms = jnp.mean(x * x, axis=-1, keepdims=True)
    o_ref[...] = (x * lax.rsqrt(ms + RMS_EPS) * g_ref[...]).astype(o_ref.dtype)


def _rmsnorm(x, gain, out_dtype, tm=512):
    n, d = x.shape
    return pl.pallas_call(
        _rmsnorm_kernel,
        out_shape=jax.ShapeDtypeStruct((n, d), out_dtype),
        grid=(n // tm,),
        in_specs=[pl.BlockSpec((tm, d), lambda i: (i, 0)), pl.BlockSpec((1, d), lambda i: (0, 0))],
        out_specs=pl.BlockSpec((tm, d), lambda i: (i, 0)),
        compiler_params=_cparams(("parallel",), 40),
        name="rmsnorm",
    )(x, gain.reshape(1, d).astype(F32))


def _mm_kernel(x_ref, w_ref, o_ref):
    o_ref[...] = _dot(x_ref[...], w_ref[...]).astype(o_ref.dtype)


def _matmul(x, w, out_dtype, tm, tn, name):
    m, k = x.shape
    n = w.shape[1]
    return pl.pallas_call(
        _mm_kernel,
        out_shape=jax.ShapeDtypeStruct((m, n), out_dtype),
        grid=(m // tm, n // tn),
        in_specs=[pl.BlockSpec((tm, k), lambda i, j: (i, 0)), pl.BlockSpec((k, tn), lambda i, j: (0, j))],
        out_specs=pl.BlockSpec((tm, tn), lambda i, j: (i, j)),
        compiler_params=_cparams(("parallel", "arbitrary"), 48),
        name=name,
    )(x, w)


def _pack_cols_kernel(shift_ref, base_ref, *refs, shifts, tn):
    o_ref = refs[-1]
    tall = jnp.concatenate([r[...] for r in refs[:-1]], axis=0)
    j = pl.program_id(1)
    for s in shifts:
        @pl.when(shift_ref[j] == s)
        def _(s=s):
            o_ref[...] = tall[s:s + tn, :].T.astype(o_ref.dtype)


def _pack_columns(w_t, starts, tn=512, tr=512):
    n_in, k = w_t.shape
    nb = tn // LANES + 1
    starts = np.asarray(starts)
    assert np.all(starts + tn <= n_in) and k % tr == 0 and np.all(starts % 8 == 0)
    base = jnp.asarray(starts // LANES, jnp.int32)
    shift = jnp.asarray(starts % LANES, jnp.int32)
    shifts = tuple(sorted(set(int(s) for s in starts % LANES)))
    last_blk = (n_in - 1) // LANES
    in_specs = [pl.BlockSpec((LANES, tr), lambda i, j, sh, bs, b=b: (jnp.minimum(bs[j] + b, last_blk), i))
                for b in range(nb)]
    return pl.pallas_call(
        functools.partial(_pack_cols_kernel, shifts=shifts, tn=tn),
        out_shape=jax.ShapeDtypeStruct((k, len(starts) * tn), BF16),
        grid_spec=pltpu.PrefetchScalarGridSpec(
            num_scalar_prefetch=2,
            grid=(k // tr, len(starts)),
            in_specs=in_specs,
            out_specs=pl.BlockSpec((tr, tn), lambda i, j, sh, bs: (i, j))),
        compiler_params=_cparams(("parallel", "arbitrary"), 32),
        name="pack_in_proj_weights",
    )(shift, base, *([w_t] * nb))


def _mm_cast_kernel(x_ref, w_ref, *refs, n_riders):
    for r in range(n_riders):
        refs[n_riders + 1 + r][...] = refs[r][...].astype(BF16)
    o_ref = refs[n_riders]
    o_ref[...] = _dot(x_ref[...], w_ref[...]).astype(o_ref.dtype)


def _rider_rows(n_rows, max_blocks):
    rb = 16
    while n_rows % rb or n_rows // rb > max_blocks:
        rb += 16
    return rb


def _matmul_with_casts(x, w, n, out_dtype, tm, tn, name, riders):
    m, k = x.shape
    ni, nj = m // tm, n // tn
    in_specs = [pl.BlockSpec((tm, k), lambda i, j: (i, 0)), pl.BlockSpec((k, tn), lambda i, j: (0, j))]
    out_specs = [pl.BlockSpec((tm, tn), lambda i, j: (i, j))]
    out_shape = [jax.ShapeDtypeStruct((m, n), out_dtype)]
    rider_specs = []
    for r in riders:
        rows, cols = r.shape
        rb = _rider_rows(rows, ni * nj)
        last = rows // rb - 1
        spec = pl.BlockSpec((rb, cols), lambda i, j, last=last: (jnp.minimum(i * nj + j, last), 0))
        rider_specs.append(spec)
        out_shape.append(jax.ShapeDtypeStruct((rows, cols), BF16))
    outs = pl.pallas_call(
        functools.partial(_mm_cast_kernel, n_riders=len(riders)),
        out_shape=tuple(out_shape),
        grid=(ni, nj),
        in_specs=in_specs[:2] + rider_specs,
        out_specs=tuple(out_specs + rider_specs),
        compiler_params=_cparams(("arbitrary", "arbitrary"), 48),
        name=name,
    )(x, w, *riders)
    return outs


def _merge_kernel(a_ref, b_ref, wa_ref, wb_ref, ga_ref, gb_ref, o_ref):
    ya = _dot(a_ref[...], wa_ref[...])
    yb = _dot(b_ref[...], wb_ref[...])
    o_ref[...] = (_sigmoid(ga_ref[...].astype(F32)) * ya + _sigmoid(gb_ref[...].astype(F32)) * yb).astype(o_ref.dtype)


def _merge(o_gdn, o_nsa, w_a, w_b, proj, gate_col0, tm=1024, tn=512):
    n, ka = o_gdn.shape
    kb = o_nsa.shape[1]
    d = w_a.shape[1]
    ja, jb = gate_col0 // tn, (gate_col0 + d) // tn
    return pl.pallas_call(
        _merge_kernel,
        out_shape=jax.ShapeDtypeStruct((n, d), BF16),
        grid=(n // tm, d // tn),
        in_specs=[pl.BlockSpec((tm, ka), lambda i, j: (i, 0)),
                  pl.BlockSpec((tm, kb), lambda i, j: (i, 0)),
                  pl.BlockSpec((ka, tn), lambda i, j: (0, j)),
                  pl.BlockSpec((kb, tn), lambda i, j: (0, j)),
                  pl.BlockSpec((tm, tn), lambda i, j: (i, ja + j)),
                  pl.BlockSpec((tm, tn), lambda i, j: (i, jb + j))],
        out_specs=pl.BlockSpec((tm, tn), lambda i, j: (i, j)),
        compiler_params=_cparams(("parallel", "arbitrary"), 48),
        name="branch_merge",
    )(o_gdn, o_nsa, w_a, w_b, proj, proj)


def _resid_mm_kernel(x_ref, w_ref, r_ref, o_ref):
    o_ref[...] = r_ref[...] + _dot(x_ref[...], w_ref[...])


def _resid_matmul(x, w, resid, tm=1024, tn=512):
    m, k = x.shape
    n = w.shape[1]
    return pl.pallas_call(
        _resid_mm_kernel,
        out_shape=jax.ShapeDtypeStruct((m, n), F32),
        grid=(m // tm, n // tn),
        in_specs=[pl.BlockSpec((tm, k), lambda i, j: (i, 0)),
                  pl.BlockSpec((k, tn), lambda i, j: (0, j)),
                  pl.BlockSpec((tm, tn), lambda i, j: (i, j))],
        out_specs=pl.BlockSpec((tm, tn), lambda i, j: (i, j)),
        compiler_params=_cparams(("parallel", "arbitrary"), 48),
        name="out_proj_residual",
    )(x, w, resid)


def _gdn_gate_kernel(a_ref, b_ref, alog_ref, dtb_ref, gc_ref, beta_ref, gl_ref, *, chunk):
    tm = a_ref.shape[0]
    x = a_ref[...] + dtb_ref[...]
    softplus = jnp.maximum(x, 0.0) + jnp.log(1.0 + jnp.exp(-jnp.abs(x)))
    g = -jnp.exp(alog_ref[...]) * softplus
    row = lax.broadcasted_iota(jnp.int32, (tm, tm), 0)
    col = lax.broadcasted_iota(jnp.int32, (tm, tm), 1)
    same = row // chunk == col // chunk
    tri = jnp.where((col <= row) & same, 1.0, 0.0).astype(F32)
    gc_ref[...] = _dot(tri, g, precision=lax.Precision.HIGHEST)
    gl_ref[...] = _dot(jnp.where(same, 1.0, 0.0).astype(F32), g, precision=lax.Precision.HIGHEST)
    beta_ref[...] = _sigmoid(b_ref[...])


def _gdn_gates(a_raw, b_raw, a_log, dt_bias, chunk, tm=512):
    n, h = a_raw.shape
    spec = pl.BlockSpec((tm, h), lambda i: (i, 0))
    vec = pl.BlockSpec((1, h), lambda i: (0, 0))
    sds = jax.ShapeDtypeStruct((n, h), F32)
    return pl.pallas_call(
        functools.partial(_gdn_gate_kernel, chunk=chunk),
        out_shape=(sds, sds, sds),
        grid=(n // tm,),
        in_specs=[spec, spec, vec, vec],
        out_specs=(spec, spec, spec),
        compiler_params=_cparams(("parallel",), 32),
        name="gdn_gates",
    )(a_raw, b_raw, a_log.reshape(1, h).astype(F32), dt_bias.reshape(1, h).astype(F32))


_CONV_PAD = 8


def _gdn_kernel(q_ref, k_ref, v_ref, z_ref, cwq_ref, cwk_ref, cwv_ref, gc_ref, beta_ref, gl_ref, gct_ref, nw_ref,
                o_ref, state, xbuf, *, chunk, hps):
    hblk = pl.program_id(1)
    t = pl.program_id(2)
    tt = q_ref.shape[0]
    d = HEAD_DIM
    width = cwq_ref.shape[0]
    c = chunk
    n_ch = tt // c
    n_lvl = int(math.log2(c))
    assert 2 ** n_lvl == c

    @pl.when(t == 0)
    def _():
        state[...] = jnp.zeros_like(state)
        xbuf[:, 0:_CONV_PAD, :] = jnp.zeros((3 * hps, _CONV_PAD, d), F32)

    def conv_silu(s, x_bf16, w):
        xbuf[s, _CONV_PAD:_CONV_PAD + tt, :] = x_bf16.astype(F32)
        y = None
        for j in range(width):
            term = xbuf[s, pl.ds(_CONV_PAD - (width - 1) + j, tt), :] * w[j:j + 1, :]
            y = term if y is None else y + term
        xbuf[s, 0:_CONV_PAD, :] = xbuf[s, tt:tt + _CONV_PAD, :]
        return _silu(y)

    def l2n(x):
        return x * lax.rsqrt(jnp.sum(x * x, axis=-1, keepdims=True) + RMS_EPS)

    ri = lax.broadcasted_iota(jnp.int32, (tt, tt), 0)
    ci = lax.broadcasted_iota(jnp.int32, (tt, tt), 1)
    incl = (ri >= ci) & (ri // c == ci // c)
    strict = ri > ci
    lane = lax.broadcasted_iota(jnp.int32, gc_ref.shape, 1)
    nw = nw_ref[...]

    hd = [dict() for _ in range(hps)]
    for hp, e in enumerate(hd):
        h = hblk * hps + hp
        hs = slice(hp * d, (hp + 1) * d)
        q = l2n(conv_silu(3 * hp, q_ref[:, hs], cwq_ref[:, hs])) * (d ** -0.5)
        k = l2n(conv_silu(3 * hp + 1, k_ref[:, hs], cwk_ref[:, hs]))
        v = conv_silu(3 * hp + 2, v_ref[:, hs], cwv_ref[:, hs])
        pick = lambda ref: jnp.sum(jnp.where(lane == h, ref[...], 0.0), axis=1, keepdims=True)
        gcc, bc, glc = pick(gc_ref), pick(beta_ref), pick(gl_ref)
        egc = jnp.exp(gcc)
        kb = k.astype(BF16)
        qk = _dot_nt(jnp.concatenate([q.astype(BF16), kb], axis=0), kb)
        dm = jnp.exp(jnp.where(incl, gcc - gct_ref[hp], NEG_INF))
        e["npow"] = jnp.where(strict, -(qk[tt:] * dm * bc), 0.0)
        e["x"] = jnp.concatenate([v * bc, k * (bc * egc)], axis=1)
        e["attn"] = (qk[:tt] * dm).astype(BF16)
        e["qe"] = q * egc
        e["kout"] = (k * jnp.exp(glc - gcc)).astype(BF16)
        e["dch"] = jnp.exp(glc)

    for lvl in range(n_lvl):
        for e in hd:
            nb = e["npow"].astype(BF16)
            if lvl + 1 < n_lvl:
                r = _dot(nb, jnp.concatenate([nb, e["x"].astype(BF16)], axis=1))
                e["npow"], e["x"] = r[:, :tt], e["x"] + r[:, tt:]
            else:
                e["x"] = e["x"] + _dot(nb, e["x"].astype(BF16))

    for hp, e in enumerate(hd):
        e["xb"] = e["x"].astype(BF16)
        au = _dot(e["attn"], e["xb"])
        e["o0"] = au[:, :d]
        e["qe"] = (e["qe"] - au[:, d:]).astype(BF16)
        e["s"] = state[hp]

    for ch in range(n_ch):
        sl = slice(ch * c, (ch + 1) * c)
        for hp, e in enumerate(hd):
            hs = slice(hp * d, (hp + 1) * d)
            ktx = _dot_tn(e["kout"][sl], e["xb"][sl])
            sb = e["s"].astype(BF16)
            o = _dot(e["qe"][sl], sb) + e["o0"][sl]
            e["s"] = e["s"] * e["dch"][ch * c:ch * c + 1, :] - _dot(ktx[:, d:].astype(BF16), sb) + ktx[:, :d]
            on = o * lax.rsqrt(jnp.mean(o * o, axis=-1, keepdims=True) + RMS_EPS) * nw
            o_ref[sl, hs] = (on * _silu(z_ref[sl, hs].astype(F32))).astype(o_ref.dtype)

    for hp, e in enumerate(hd):
        state[hp] = e["s"]


def _gdn_mixer(proj, conv_w, gc, beta, gl, gct, norm_w, batch, seq, heads, chunk, tt=256, hps=4):
    d = HEAD_DIM
    n = batch * seq
    nt = seq // tt
    hb = heads // hps
    width = conv_w.shape[0]
    assert width - 1 <= _CONV_PAD and tt % chunk == 0 and seq % tt == 0 and heads % hps == 0

    def slab(off):
        return pl.BlockSpec((tt, hps * d), lambda b, h, t: (b * nt + t, off + h))

    def cw(off):
        return pl.BlockSpec((width, hps * d), lambda b, h, t: (0, off + h))

    gspec = pl.BlockSpec((tt, heads), lambda b, h, t: (b * nt + t, 0))
    return pl.pallas_call(
        functools.partial(_gdn_kernel, chunk=chunk, hps=hps),
        out_shape=jax.ShapeDtypeStruct((n, heads * d), BF16),
        grid=(batch, hb, nt),
        in_specs=[slab(0), slab(hb), slab(2 * hb), slab(3 * hb),
                  cw(0), cw(hb), cw(2 * hb),
                  gspec, gspec, gspec,
                  pl.BlockSpec((hps, 1, tt), lambda b, h, t: (b * hb + h, 0, t)),
                  pl.BlockSpec((1, d), lambda b, h, t: (0, 0))],
        out_specs=pl.BlockSpec((tt, hps * d), lambda b, h, t: (b * nt + t, h)),
        scratch_shapes=[pltpu.VMEM((hps, d, d), F32), pltpu.VMEM((3 * hps, tt + _CONV_PAD, d), F32)],
        compiler_params=_cparams(("parallel", "parallel", "arbitrary"), 40),
        name="gdn_mixer",
    )(proj, proj, proj, proj, conv_w, conv_w, conv_w, gc, beta, gl, gct, norm_w.reshape(1, d).astype(F32))


def _nsa_cmp_kernel(kc_ref, vc_ref, pek_ref, w1k_ref, w2k_ref, pev_ref, w1v_ref, w2v_ref,
                    ko_ref, vo_ref, xf, sh, *, stride, block):
    seq, d = kc_ref.shape
    nh = seq // stride
    reps = block // stride
    assert nh % 8 == 0

    def compress(src_ref, pe_ref, w1_ref, w2_ref, dst_ref):
        xf[...] = src_ref[...].astype(F32)
        pe = pe_ref[...]
        hid = None
        for r in range(reps):
            acc = None
            for l in range(stride):
                li = r * stride + l
                rows = xf[pl.ds(l, nh, stride=stride), :] + pe[li:li + 1, :]
                term = _dot(rows.astype(BF16), w1_ref[li].astype(BF16))
                acc = term if acc is None else acc + term
            if r == 0:
                hid = acc
            else:
                sh[0:nh, :] = acc
                sh[nh:nh + 8, :] = jnp.zeros((8, d), F32)
                hid = hid + sh[pl.ds(r, nh), :]
        hid = _silu(hid)
        dst_ref[0] = _dot(hid.astype(BF16), w2_ref[...].astype(BF16)).astype(dst_ref.dtype)

    compress(kc_ref, pek_ref, w1k_ref, w2k_ref, ko_ref)
    compress(vc_ref, pev_ref, w1v_ref, w2v_ref, vo_ref)


def _nsa_compress(proj, kc_col, vc_col, pe_k, w1_k, w2_k, pe_v, w1_v, w2_v, batch, seq, groups):
    d = HEAD_DIM
    nh = seq // CMP_STRIDE
    full2 = lambda b, g: (0, 0)
    full3 = lambda b, g: (0, 0, 0)
    out_sds = jax.ShapeDtypeStruct((batch * groups, nh, d), BF16)
    out_spec = pl.BlockSpec((1, nh, d), lambda b, g: (b * groups + g, 0, 0))
    return pl.pallas_call(
        functools.partial(_nsa_cmp_kernel, stride=CMP_STRIDE, block=CMP_BLOCK),
        out_shape=(out_sds, out_sds),
        grid=(batch, groups),
        in_specs=[pl.BlockSpec((seq, d), lambda b, g: (b, kc_col + g)),
                  pl.BlockSpec((seq, d), lambda b, g: (b, vc_col + g)),
                  pl.BlockSpec((CMP_BLOCK, d), full2), pl.BlockSpec((CMP_BLOCK, d, d), full3), pl.BlockSpec((d, d), full2),
                  pl.BlockSpec((CMP_BLOCK, d), full2), pl.BlockSpec((CMP_BLOCK, d, d), full3), pl.BlockSpec((d, d), full2)],
        out_specs=(out_spec, out_spec),
        scratch_shapes=[pltpu.VMEM((seq, d), F32), pltpu.VMEM((nh + 8, d), F32)],
        compiler_params=_cparams(("parallel", "parallel"), 32),
        name="nsa_compress",
    )(proj, proj, pe_k, w1_k, w2_k, pe_v, w1_v, w2_v)


def _nsa_kernel(q_ref, kc_ref, vc_ref, ks_ref, vs_ref, kw_ref, vw_ref, gate_ref, selmap_ref, expand_ref,
                o_ref, score_t, *, hpg, gps, tk, n_sel, n_top):
    tq = q_ref.shape[0]
    d = HEAD_DIM
    gw = hpg * d
    n_cmp_pad = kc_ref.shape[1]
    t0 = pl.program_id(2) * tq
    rows = hpg * tq
    grp = list(range(gps))
    kv = lambda ref, gi, r0, n: ref[pl.ds(r0, n), gi * d:(gi + 1) * d]

    q4 = []
    for gi in grp:
        qg = jnp.concatenate([q_ref[:, gi * gw + hh * d:gi * gw + (hh + 1) * d] for hh in range(hpg)], axis=0)
        q4.append((qg.astype(F32) * (d ** -0.5 * math.log2(math.e))).astype(BF16))
    trow = t0 + lax.broadcasted_iota(jnp.int32, (tq, 1), 0)

    ncol = lax.broadcasted_iota(jnp.int32, (1, n_cmp_pad), 1)
    vis_c = ((ncol * CMP_STRIDE + (CMP_BLOCK - 1)) <= trow)[None]
    s_c = [_dot_nt(q4[gi], kc_ref[gi]).reshape(hpg, tq, n_cmp_pad) for gi in grp]
    p_c = []
    for gi in grp:
        s3 = jnp.where(vis_c, s_c[gi], NEG_INF)
        e = jnp.where(vis_c, jnp.exp2(s3 - jnp.max(s3, axis=-1, keepdims=True)), 0.0)
        den = jnp.sum(e, axis=-1, keepdims=True)
        p_c.append(e / jnp.where(den > 0.0, den, 1.0))
    o_cmp = [_dot(p_c[gi].reshape(rows, n_cmp_pad).astype(BF16), vc_ref[gi]).reshape(hpg, tq, d) for gi in grp]

    jrow = lax.broadcasted_iota(jnp.int32, (n_sel, 1), 0)
    tcol = t0 + lax.broadcasted_iota(jnp.int32, (1, tq), 1)
    cur = tcol // SEL_BLOCK
    forced = (jrow == 0) | (jrow == cur) | (jrow == cur - 1)
    causal_blk = jrow * SEL_BLOCK <= tcol
    keys = []
    for gi in grp:
        imp_t = lax.dot_general(selmap_ref[...], jnp.sum(p_c[gi], axis=0), (((1,), (1,)), ((), ())),
                                preferred_element_type=F32, precision=lax.Precision.HIGHEST)
        sc = jnp.where(forced, FORCE_SCORE, jnp.where(causal_blk, imp_t, NEG_INF))
        bits = pltpu.bitcast(sc, jnp.int32)
        keys.append(bits ^ ((bits >> 31) & jnp.int32(0x7FFFFFFF)))
        score_t[gi] = keys[gi]

    def rank_step(i, cnts):
        tie = jnp.where(jrow > i, 1, 0)
        return tuple(cnts[gi] + jnp.where(score_t[gi, pl.ds(i, 1), :] + tie > keys[gi], 1.0, 0.0) for gi in grp)

    n_rank = jnp.minimum((t0 + tq - 1) // SEL_BLOCK + 1, n_sel)
    cnts = lax.fori_loop(0, n_rank, rank_step, tuple(jnp.zeros((n_sel, tq), F32) for _ in grp))
    eye = jnp.where(lax.broadcasted_iota(jnp.int32, (n_sel, n_sel), 0)
                    == lax.broadcasted_iota(jnp.int32, (n_sel, n_sel), 1), 1.0, 0.0).astype(BF16)
    sel_bias = [_dot_tn(jnp.where(cnts[gi] < n_top, 0.0, NEG_INF).astype(BF16), eye).astype(BF16) for gi in grp]

    def sel_step(kt, carry, diagonal):
        k0 = pl.multiple_of(kt * tk, tk)
        ex = expand_ref[:, pl.ds(k0, tk)]
        s3 = []
        for gi in grp:
            b = _dot(sel_bias[gi], ex)
            if diagonal:
                kpos = k0 + lax.broadcasted_iota(jnp.int32, (1, tk), 1)
                b = jnp.where(kpos <= trow, b, NEG_INF)
            s3.append(_dot_nt(q4[gi], kv(ks_ref, gi, k0, tk)).reshape(hpg, tq, tk) + b[None])
        out = []
        for gi in grp:
            m, acc = carry[gi]
            m_new = jnp.maximum(m, jnp.max(s3[gi], axis=-1, keepdims=True))
            alpha = jnp.exp2(m - m_new)
            e = jnp.exp2((s3[gi] - m_new).astype(BF16))
            pv = _dot(e.reshape(rows, tk), with_ones(kv(vs_ref, gi, k0, tk)))
            out.append((m_new, alpha * acc + pv.reshape(hpg, tq, 2 * d)))
        return tuple(out)

    def with_ones(v_tile):
        return jnp.concatenate([v_tile, jnp.ones(v_tile.shape, BF16)], axis=1)

    n_full = t0 // tk
    init = (jnp.full((hpg, tq, 1), NEG_INF, F32), jnp.zeros((hpg, tq, 2 * d), F32))
    carry = lax.fori_loop(0, n_full, functools.partial(sel_step, diagonal=False), tuple(init for _ in grp))
    carry = sel_step(n_full, carry, diagonal=True)
    o_sel = [carry[gi][1][:, :, :d] / carry[gi][1][:, :, d:] for gi in grp]

    span = WINDOW + tq
    ws = pl.multiple_of(jnp.maximum(t0 - WINDOW, 0), tq)
    kpos = ws + lax.broadcasted_iota(jnp.int32, (1, span), 1)
    rel = trow - kpos
    b_w = jnp.where((rel >= 0) & (rel < WINDOW), 0.0, NEG_INF)[None]
    s_w = [_dot_nt(q4[gi], kv(kw_ref, gi, ws, span)).reshape(hpg, tq, span) + b_w for gi in grp]
    o_win = []
    for gi in grp:
        e = jnp.exp2((s_w[gi] - jnp.max(s_w[gi], axis=-1, keepdims=True)).astype(BF16))
        pv = _dot(e.reshape(rows, span), with_ones(kv(vw_ref, gi, ws, span))).reshape(hpg, tq, 2 * d)
        o_win.append(pv[:, :, :d] / pv[:, :, d:])

    for gi in grp:
        gates = _sigmoid(gate_ref[gi])
        for hh in range(hpg):
            g0 = gates[:, 3 * hh:3 * hh + 1]
            g1 = gates[:, 3 * hh + 1:3 * hh + 2]
            g2 = gates[:, 3 * hh + 2:3 * hh + 3]
            o_ref[:, gi * gw + hh * d:gi * gw + (hh + 1) * d] = (
                g0 * o_cmp[gi][hh] + g1 * o_sel[gi][hh] + g2 * o_win[gi][hh]).astype(o_ref.dtype)


def _nsa_mixer(proj, q_col, ks_col, vs_col, kw_col, vw_col, k_cmp, v_cmp, gates, batch, seq, groups, hpg,
               tq=256, tk=1024, gps=2):
    d = HEAD_DIM
    n = batch * seq
    nq = seq // tq
    n_sel = seq // SEL_BLOCK
    n_top = min(SEL_TOP_N, n_sel)
    n_cmp = (seq - CMP_BLOCK) // CMP_STRIDE + 1
    n_cmp_pad = k_cmp.shape[1]
    assert n_sel % 8 == 0 and seq >= WINDOW + tq and WINDOW % tq == 0 and tk % tq == 0 and seq % tk == 0

    cmp_start = np.arange(n_cmp_pad) * CMP_STRIDE
    sel_start = np.arange(n_sel) * SEL_BLOCK
    overlap = (np.minimum(cmp_start[None, :] + CMP_BLOCK, sel_start[:, None] + SEL_BLOCK)
               - np.maximum(cmp_start[None, :], sel_start[:, None]))
    sel_map_t = np.clip(overlap, 0, None) / CMP_STRIDE
    sel_map_t[:, n_cmp:] = 0.0
    expand = (np.arange(seq)[None, :] // SEL_BLOCK == np.arange(n_sel)[:, None]).astype(np.float32)

    gb = groups // gps
    qw = gps * hpg * d
    assert groups % gps == 0 and all(c % gps == 0 for c in (ks_col, vs_col, kw_col, vw_col)) and q_col % (gps * hpg) == 0

    def kv(col):
        return pl.BlockSpec((seq, gps * d), lambda b, g, t: (b, col // gps + g))

    cmp_spec = pl.BlockSpec((gps, n_cmp_pad, d), lambda b, g, t: (b * gb + g, 0, 0))
    return pl.pallas_call(
        functools.partial(_nsa_kernel, hpg=hpg, gps=gps, tk=tk, n_sel=n_sel, n_top=n_top),
        out_shape=jax.ShapeDtypeStruct((n, groups * hpg * d), BF16),
        grid=(batch, gb, nq),
        in_specs=[pl.BlockSpec((tq, qw), lambda b, g, t: (b * nq + t, q_col // (gps * hpg) + g)),
                  cmp_spec, cmp_spec, kv(ks_col), kv(vs_col), kv(kw_col), kv(vw_col),
                  pl.BlockSpec((gps, tq, 3 * hpg), lambda b, g, t: (b * gb + g, t, 0)),
                  pl.BlockSpec((n_sel, n_cmp_pad), lambda b, g, t: (0, 0)),
                  pl.BlockSpec((n_sel, seq), lambda b, g, t: (0, 0))],
        out_specs=pl.BlockSpec((tq, qw), lambda b, g, t: (b * nq + t, g)),
        scratch_shapes=[pltpu.VMEM((gps, n_sel, tq), jnp.int32)],
        compiler_params=_cparams(("parallel", "parallel", "arbitrary"), 48),
        name="nsa_attention",
    )(proj, k_cmp, v_cmp, proj, proj, proj, proj, gates, jnp.asarray(sel_map_t, F32), jnp.asarray(expand, BF16))


def _router_kernel(h_ref, g_ref, wr_ref, wrl_ref, br_ref, xn_ref, ids_ref, wts_ref, cnt_ref):
    x = h_ref[...]
    xn = x * lax.rsqrt(jnp.mean(x * x, axis=-1, keepdims=True) + RMS_EPS) * g_ref[...]
    xn_ref[...] = xn
    x_hi = xn.astype(BF16)
    x_lo = (xn - x_hi.astype(F32)).astype(BF16)
    logits = (_dot(x_hi, wr_ref[...]) + _dot(x_hi, wrl_ref[...]) + _dot(x_lo, wr_ref[...])) + br_ref[...]
    lane = lax.broadcasted_iota(jnp.int32, logits.shape, 1)
    big = jnp.int32(2 * LANES)
    is_g = lane < N_GROUPS
    gl = jnp.where(is_g, logits, NEG_INF)
    gm = jnp.max(gl, axis=-1, keepdims=True)
    p_g = 1.0 / jnp.sum(jnp.where(is_g, jnp.exp(gl - gm), 0.0), axis=-1, keepdims=True)
    grp = jnp.min(jnp.where(gl == gm, lane, big), axis=-1, keepdims=True)
    eidx = lane - N_GROUPS
    in_grp = (eidx >= 0) & (eidx // EXPERTS_PER_GROUP == grp) & (eidx < N_GROUPS * EXPERTS_PER_GROUP)
    el = jnp.where(in_grp, logits, NEG_INF)
    em = jnp.max(el, axis=-1, keepdims=True)
    ee = jnp.where(in_grp, jnp.exp(el - em), 0.0)
    pe = ee / jnp.sum(ee, axis=-1, keepdims=True)
    p1 = jnp.max(jnp.where(in_grp, pe, -1.0), axis=-1, keepdims=True)
    i1 = jnp.min(jnp.where(in_grp & (pe == p1), lane, big), axis=-1, keepdims=True)
    rest = in_grp & (lane != i1)
    p2 = jnp.max(jnp.where(rest, pe, -1.0), axis=-1, keepdims=True)
    i2 = jnp.min(jnp.where(rest & (pe == p2), lane, big), axis=-1, keepdims=True)
    denom = p1 + p2
    wts_ref[...] = jnp.where(lane == 0, p_g * p1 / denom, p_g * p2 / denom)

    @pl.when(pl.program_id(0) == 0)
    def _():
        cnt_ref[...] = jnp.zeros_like(cnt_ref)

    tm = logits.shape[0]
    hot = jnp.where((lane == i1) | (lane == i2), 1.0, 0.0)
    below = jnp.where(lax.broadcasted_iota(jnp.int32, (tm, tm), 0) > lax.broadcasted_iota(jnp.int32, (tm, tm), 1),
                      1.0, 0.0).astype(BF16)
    before = _dot(below, hot.astype(BF16)) + cnt_ref[...]
    r1 = jnp.sum(jnp.where(lane == i1, before, 0.0), axis=-1, keepdims=True).astype(jnp.int32)
    r2 = jnp.sum(jnp.where(lane == i2, before, 0.0), axis=-1, keepdims=True).astype(jnp.int32)
    cnt_ref[...] += jnp.sum(hot, axis=0, keepdims=True)
    ids_ref[...] = jnp.where(lane == 0, i1 - N_GROUPS, jnp.where(lane == 1, i2 - N_GROUPS, jnp.where(lane == 2, r1, r2)))


def _router(h, g_ffn, w_group, b_group, w_expert, b_expert, tm=256):
    n, d = h.shape
    n_g, n_e = w_group.shape[1], w_expert.shape[1]
    n_r = n_g + n_e
    assert n_r <= LANES and TOP_K == 2 and n_g == N_GROUPS
    wr = jnp.concatenate([w_group, w_expert, jnp.zeros((d, LANES - n_r), F32)], axis=1).astype(F32)
    wr_hi = wr.astype(BF16)
    wr_lo = (wr - wr_hi.astype(F32)).astype(BF16)
    br = jnp.concatenate([b_group, b_expert, jnp.zeros((LANES - n_r,), F32)]).reshape(1, LANES).astype(F32)
    row = lambda i: (i, 0)
    fixed = lambda i: (0, 0)
    xn, ids, wts, cnt = pl.pallas_call(
        _router_kernel,
        out_shape=(jax.ShapeDtypeStruct((n, d), F32), jax.ShapeDtypeStruct((n, LANES), jnp.int32),
                   jax.ShapeDtypeStruct((n, LANES), F32), jax.ShapeDtypeStruct((1, LANES), F32)),
        grid=(n // tm,),
        in_specs=[pl.BlockSpec((tm, d), row), pl.BlockSpec((1, d), fixed),
                  pl.BlockSpec((d, LANES), fixed), pl.BlockSpec((d, LANES), fixed), pl.BlockSpec((1, LANES), fixed)],
        out_specs=(pl.BlockSpec((tm, d), row), pl.BlockSpec((tm, LANES), row), pl.BlockSpec((tm, LANES), row),
                   pl.BlockSpec((1, LANES), fixed)),
        compiler_params=_cparams(("arbitrary",), 40),
        name="moe_router",
    )(h, g_ffn.reshape(1, d).astype(F32), wr_hi, wr_lo, br)
    return xn, ids, wts, cnt[0, n_g:n_r]


def _row_gather(idx_ref, base, src_hbm, dst, sem, n_rows, wait):
    for r in range(n_rows):
        cp = pltpu.make_async_copy(src_hbm.at[pl.ds(idx_ref[base + r], 1), :], dst.at[pl.ds(r, 1), :], sem)
        if wait:
            cp.wait()
        else:
            cp.start()


def _slot_scatter_kernel(dest_ref, tok_ref, *, top_k, unroll):
    n_out = tok_ref.shape[0]
    n_slot = dest_ref.shape[0]

    def clear(j, carry):
        for u in range(unroll):
            tok_ref[j * unroll + u] = 0
        return carry

    def place(j, carry):
        rows = [dest_ref[j * unroll + u] for u in range(unroll)]
        for u in range(unroll):
            tok_ref[rows[u]] = (j * unroll + u) // top_k
        return carry

    lax.fori_loop(0, n_out // unroll, clear, 0)
    lax.fori_loop(0, n_slot // unroll, place, 0)


def _slot_scatter(dest_flat, n_out, top_k, unroll=64):
    assert n_out % unroll == 0 and dest_flat.shape[0] % unroll == 0
    return pl.pallas_call(
        functools.partial(_slot_scatter_kernel, top_k=top_k, unroll=unroll),
        out_shape=jax.ShapeDtypeStruct((n_out,), jnp.int32),
        in_specs=[pl.BlockSpec(memory_space=pltpu.SMEM)],
        out_specs=pl.BlockSpec(memory_space=pltpu.SMEM),
        name="moe_slot_scatter",
    )(dest_flat)


def _expert_kernel(be_ref, tok_ref, nact_ref, x_hbm, wgu_ref, wd_ref, y_ref, xbuf, sem, *, blk, d_ff):
    i = pl.program_id(0)
    n_act = nact_ref[0]
    slot = i % 2

    @pl.when(i == 0)
    def _():
        _row_gather(tok_ref, 0, x_hbm, xbuf.at[0], sem.at[0], blk, wait=False)

    def run_block(prefetch_next):
        _row_gather(tok_ref, i * blk, x_hbm, xbuf.at[slot], sem.at[slot], blk, wait=True)
        if prefetch_next:
            _row_gather(tok_ref, (i + 1) * blk, x_hbm, xbuf.at[1 - slot], sem.at[1 - slot], blk, wait=False)
        xb = xbuf[slot].astype(BF16)
        gu = _dot(xb, wgu_ref[0])
        act = _silu(gu[:, :d_ff]) * gu[:, d_ff:]
        y_ref[...] = _dot(act.astype(BF16), wd_ref[0])

    pl.when(i + 1 < n_act)(functools.partial(run_block, True))
    pl.when(i + 1 == n_act)(functools.partial(run_block, False))

    @pl.when(i >= n_act)
    def _():
        y_ref[...] = jnp.zeros_like(y_ref)


def _experts(xn, w_gate_up, w_down, blk_expert, tok_buf, n_active, blk):
    n, d = xn.shape
    n_exp, _, ff2 = w_gate_up.shape
    d_ff = ff2 // 2
    n_blk = blk_expert.shape[0]
    return pl.pallas_call(
        functools.partial(_expert_kernel, blk=blk, d_ff=d_ff),
        out_shape=jax.ShapeDtypeStruct((n_blk * blk, d), F32),
        grid_spec=pltpu.PrefetchScalarGridSpec(
            num_scalar_prefetch=3,
            grid=(n_blk,),
            in_specs=[pl.BlockSpec(memory_space=pl.ANY),
                      pl.BlockSpec((1, d, ff2), lambda i, be, tok, na: (be[i], 0, 0)),
                      pl.BlockSpec((1, d_ff, d), lambda i, be, tok, na: (be[i], 0, 0))],
            out_specs=pl.BlockSpec((blk, d), lambda i, be, tok, na: (i, 0)),
            scratch_shapes=[pltpu.VMEM((2, blk, d), F32), pltpu.SemaphoreType.DMA((2,))]),
        compiler_params=_cparams(("arbitrary",), 56),
        name="moe_experts",
    )(blk_expert, tok_buf, n_active, xn, w_gate_up, w_down)


def _combine_kernel(pos_ref, h_ref, g_ref, w_ref, y_hbm, o_ref, ybuf, sem, *, tc):
    i = pl.program_id(0)
    n = pl.num_programs(0)
    slot = i % 2
    rows = TOP_K * tc

    @pl.when(i == 0)
    def _():
        _row_gather(pos_ref, 0, y_hbm, ybuf.at[0], sem.at[0], rows, wait=False)

    def run_tile(prefetch_next):
        _row_gather(pos_ref, i * rows, y_hbm, ybuf.at[slot], sem.at[slot], rows, wait=True)
        if prefetch_next:
            _row_gather(pos_ref, (i + 1) * rows, y_hbm, ybuf.at[1 - slot], sem.at[1 - slot], rows, wait=False)
        hh = h_ref[...]
        w = w_ref[...]
        for kk in range(TOP_K):
            hh = hh + w[:, kk:kk + 1] * ybuf[slot, kk * tc:(kk + 1) * tc, :]
        o_ref[...] = hh * lax.rsqrt(jnp.mean(hh * hh, axis=-1, keepdims=True) + RMS_EPS) * g_ref[...]

    pl.when(i + 1 < n)(functools.partial(run_tile, True))
    pl.when(i + 1 == n)(functools.partial(run_tile, False))


def _combine(h, y_buf, pos, wts, g_final, tc=128):
    n, d = h.shape
    return pl.pallas_call(
        functools.partial(_combine_kernel, tc=tc),
        out_shape=jax.ShapeDtypeStruct((n, d), F32),
        grid_spec=pltpu.PrefetchScalarGridSpec(
            num_scalar_prefetch=1,
            grid=(n // tc,),
            in_specs=[pl.BlockSpec((tc, d), lambda i, pos: (i, 0)),
                      pl.BlockSpec((1, d), lambda i, pos: (0, 0)),
                      pl.BlockSpec((tc, LANES), lambda i, pos: (i, 0)),
                      pl.BlockSpec(memory_space=pl.ANY)],
            out_specs=pl.BlockSpec((tc, d), lambda i, pos: (i, 0)),
            scratch_shapes=[pltpu.VMEM((2, TOP_K * tc, d), F32), pltpu.SemaphoreType.DMA((2,))]),
        compiler_params=_cparams(("arbitrary",), 40),
        name="moe_combine_final_norm",
    )(pos, h, g_final.reshape(1, d).astype(F32), wts, y_buf)


def _dispatch_plan(ids, counts, blk):
    n_tok = ids.shape[0]
    n_experts = counts.shape[0]
    n_slot = n_tok * TOP_K
    counts = counts.astype(jnp.int32)
    padded = (counts + blk - 1) // blk * blk
    pad_end = jnp.cumsum(padded)
    pad_start = pad_end - padded
    dest = pad_start[ids[:, :TOP_K]] + ids[:, TOP_K:2 * TOP_K]
    n_blk = -(-n_slot // blk) + n_experts
    blk_start = jnp.arange(n_blk, dtype=jnp.int32) * blk
    blk_expert = jnp.minimum(jnp.sum(pad_end[None, :] <= blk_start[:, None], axis=1), n_experts - 1).astype(jnp.int32)
    n_active = (pad_end[-1] // blk).astype(jnp.int32).reshape(1)
    return dest.astype(jnp.int32), blk_expert, n_active, n_blk


def _layer(x, g_mix, w_in, conv_w, a_log, dt_bias, norm_w, pe_k, w1_k, w2_k, pe_v, w1_v, w2_v,
           w_a, w_b, w_out, g_ffn, w_group, b_group, w_expert, b_expert, w_gate_up, w_down, g_out):
    batch, seq, dm = x.shape
    n = batch * seq
    d = HEAD_DIM
    gdn_w = w_a.shape[0]
    nsa_w = w_b.shape[0]
    heads = gdn_w // d
    nsa_heads = nsa_w // d
    groups = NSA_GROUPS
    hpg = nsa_heads // groups
    kvw = groups * d

    sizes = (3 * gdn_w, gdn_w, heads, heads, nsa_w, 6 * kvw, 3 * nsa_heads, 2 * dm)
    assert sum(sizes) == w_in.shape[1]
    offs = np.concatenate([[0], np.cumsum(sizes)])
    tn_in = 512
    wide = (0, 1, 4, 5, 7)
    assert all(sizes[i] % tn_in == 0 for i in wide) and sizes[2] + sizes[3] <= tn_in and sizes[6] <= tn_in
    starts = [c for i in wide for c in range(offs[i], offs[i + 1], tn_in)]
    n_big = len(starts) * tn_in
    w_pack = _pack_columns(w_in.T, starts + [offs[2], offs[6]], tn=tn_in)
    w_small = jnp.concatenate([w_pack[:, n_big:n_big + sizes[2] + sizes[3]],
                               w_pack[:, n_big + tn_in:n_big + tn_in + sizes[6]]], axis=1)
    z_col = 3 * heads
    nq_col = z_col + heads
    kv_col = nq_col + nsa_heads
    gate_col0 = (kv_col + 6 * groups) * d

    x2 = x.reshape(n, dm)
    xn = _rmsnorm(x2, g_mix, BF16)
    n_experts, _, ff2 = w_gate_up.shape
    proj, wgu_bf, wd_bf = _matmul_with_casts(
        xn, w_pack, n_big, BF16, tm=1024, tn=tn_in, name="in_proj",
        riders=[w_gate_up.reshape(n_experts * dm, ff2), w_down.reshape(n_experts * (ff2 // 2), dm)])
    wgu_bf = wgu_bf.reshape(w_gate_up.shape)
    wd_bf = wd_bf.reshape(w_down.shape)
    small = _matmul(xn, w_small, F32, tm=1024, tn=w_small.shape[1], name="in_proj_small")
    a_raw, b_raw, nsa_gate = small[:, :heads], small[:, heads:2 * heads], small[:, 2 * heads:]

    gc, beta, gl = _gdn_gates(a_raw, b_raw, a_log, dt_bias, GDN_CHUNK)
    gct = gc.reshape(batch, seq, heads).transpose(0, 2, 1).reshape(batch * heads, 1, seq)
    o_gdn = _gdn_mixer(proj, conv_w, gc, beta, gl, gct, norm_w, batch, seq, heads, GDN_CHUNK)

    k_cmp, v_cmp = _nsa_compress(proj, kv_col, kv_col + groups, pe_k, w1_k, w2_k, pe_v, w1_v, w2_v,
                                 batch, seq, groups)
    gates = nsa_gate.reshape(batch, seq, groups, 3 * hpg).transpose(0, 2, 1, 3).reshape(batch * groups, seq, 3 * hpg)
    o_nsa = _nsa_mixer(proj, nq_col, kv_col + 2 * groups, kv_col + 3 * groups, kv_col + 4 * groups,
                       kv_col + 5 * groups, k_cmp, v_cmp, gates, batch, seq, groups, hpg)

    merged = _merge(o_gdn, o_nsa, w_a.astype(BF16), w_b.astype(BF16), proj, gate_col0)
    h = _resid_matmul(merged, w_out.astype(BF16), x2)

    xn2, ids, wts, counts = _router(h, g_ffn, w_group, b_group, w_expert, b_expert)
    dest, blk_expert, n_active, n_blk = _dispatch_plan(ids, counts, MOE_BLOCK)
    tok_buf = _slot_scatter(dest.reshape(n * TOP_K), n_blk * MOE_BLOCK, TOP_K)
    y_buf = _experts(xn2, wgu_bf, wd_bf, blk_expert, tok_buf, n_active, MOE_BLOCK)
    tc = 128
    pos_tiles = dest.reshape(n // tc, tc, TOP_K).transpose(0, 2, 1).reshape(n * TOP_K)
    out = _combine(h, y_buf, pos_tiles, wts, g_out, tc)
    return out.reshape(batch, seq, dm)


def kernel(x, g_mix, w_in, gdn_conv_w, gdn_a_log, gdn_dt_bias, gdn_norm_w, cmp_pe_k, cmp_w1_k, cmp_w2_k,
           cmp_pe_v, cmp_w1_v, cmp_w2_v, w_branch_gdn, w_branch_nsa, w_out, g_ffn, w_group, b_group,
           w_expert, b_expert, w_gate_up, w_down, g_final):
    depth = g_mix.shape[0]
    assert depth == 1, "the fused final norm assumes a single layer"
    return _layer(x, g_mix[0], w_in[0], gdn_conv_w[0], gdn_a_log[0], gdn_dt_bias[0], gdn_norm_w[0],
                  cmp_pe_k[0], cmp_w1_k[0], cmp_w2_k[0], cmp_pe_v[0], cmp_w1_v[0], cmp_w2_v[0],
                  w_branch_gdn[0], w_branch_nsa[0], w_out[0], g_ffn[0], w_group[0], b_group[0],
                  w_expert[0], b_expert[0], w_gate_up[0], w_down[0], g_final)
```

```python
import functools
import math

import numpy as np
import jax
import jax.numpy as jnp
from jax import lax
from jax.experimental import pallas as pl
from jax.experimental.pallas import tpu as pltpu

F32 = jnp.float32
BF16 = jnp.bfloat16

RMS_EPS = 1e-6
NEG_INF = -1e30
FORCE_SCORE = 1e6

HEAD_DIM = 128
GDN_CHUNK = 64
NSA_GROUPS = 4
CMP_BLOCK = 32
CMP_STRIDE = 16
SEL_BLOCK = 64
SEL_TOP_N = 16
WINDOW = 512
N_GROUPS = 8
EXPERTS_PER_GROUP = 8
TOP_K = 2
MOE_BLOCK = 128
LANES = 128

V7X_VMEM_BYTES = 64 * 1024 * 1024


def _cparams(semantics, vmem_mb):
    assert vmem_mb * 1024 * 1024 < V7X_VMEM_BYTES
    return pltpu.CompilerParams(dimension_semantics=semantics, vmem_limit_bytes=vmem_mb * 1024 * 1024)


def _dot(a, b, **kw):
    return jnp.dot(a, b, preferred_element_type=F32, **kw)


def _dot_nt(a, b):
    return lax.dot_general(a, b, (((1,), (1,)), ((), ())), preferred_element_type=F32)


def _dot_tn(a, b):
    return lax.dot_general(a, b, (((0,), (0,)), ((), ())), preferred_element_type=F32)


def _sigmoid(x):
    return 1.0 / (1.0 + jnp.exp(-x))


def _silu(x):
    return x * _sigmoid(x)


def _rmsnorm_kernel(x_ref, g_ref, o_ref):
    x = x_ref[...].astype(F32)
    ms = jnp.mean(x * x, axis=-1, keepdims=True)
    o_ref[...] = (x * lax.rsqrt(ms + RMS_EPS) * g_ref[...]).astype(o_ref.dtype)


def _rmsnorm(x, gain, out_dtype, tm=512):
    n, d = x.shape
    return pl.pallas_call(
        _rmsnorm_kernel,
        out_shape=jax.ShapeDtypeStruct((n, d), out_dtype),
        grid=(n // tm,),
        in_specs=[pl.BlockSpec((tm, d), lambda i: (i, 0)), pl.BlockSpec((1, d), lambda i: (0, 0))],
        out_specs=pl.BlockSpec((tm, d), lambda i: (i, 0)),
        compiler_params=_cparams(("parallel",), 40),
        name="rmsnorm",
    )(x, gain.reshape(1, d).astype(F32))


def _mm_kernel(x_ref, w_ref, o_ref):
    o_ref[...] = _dot(x_ref[...], w_ref[...]).astype(o_ref.dtype)


def _matmul(x, w, out_dtype, tm, tn, name):
    m, k = x.shape
    n = w.shape[1]
    return pl.pallas_call(
        _mm_kernel,
        out_shape=jax.ShapeDtypeStruct((m, n), out_dtype),
        grid=(m // tm, n // tn),
        in_specs=[pl.BlockSpec((tm, k), lambda i, j: (i, 0)), pl.BlockSpec((k, tn), lambda i, j: (0, j))],
        out_specs=pl.BlockSpec((tm, tn), lambda i, j: (i, j)),
        compiler_params=_cparams(("parallel", "arbitrary"), 48),
        name=name,
    )(x, w)


def _pack_cols_kernel(shift_ref, base_ref, *refs, shifts, tn):
    o_ref = refs[-1]
    tall = jnp.concatenate([r[...] for r in refs[:-1]], axis=0)
    j = pl.program_id(1)
    for s in shifts:
        @pl.when(shift_ref[j] == s)
        def _(s=s):
            o_ref[...] = tall[s:s + tn, :].T.astype(o_ref.dtype)


def _pack_columns(w_t, starts, tn=512, tr=2048):
    n_in, k = w_t.shape
    tr = min(tr, k)
    nb = tn // LANES + 1
    starts = np.asarray(starts)
    assert np.all(starts + tn <= n_in) and k % tr == 0 and np.all(starts % 8 == 0)
    base = jnp.asarray(starts // LANES, jnp.int32)
    shift = jnp.asarray(starts % LANES, jnp.int32)
    shifts = tuple(sorted(set(int(s) for s in starts % LANES)))
    last_blk = (n_in - 1) // LANES
    in_specs = [pl.BlockSpec((LANES, tr), lambda i, j, sh, bs, b=b: (jnp.minimum(bs[j] + b, last_blk), i))
                for b in range(nb)]
    return pl.pallas_call(
        functools.partial(_pack_cols_kernel, shifts=shifts, tn=tn),
        out_shape=jax.ShapeDtypeStruct((k, len(starts) * tn), BF16),
        grid_spec=pltpu.PrefetchScalarGridSpec(
            num_scalar_prefetch=2,
            grid=(k // tr, len(starts)),
            in_specs=in_specs,
            out_specs=pl.BlockSpec((tr, tn), lambda i, j, sh, bs: (i, j))),
        compiler_params=_cparams(("parallel", "arbitrary"), 32),
        name="pack_in_proj_weights",
    )(shift, base, *([w_t] * nb))


def _mm_cast_kernel(x_ref, w_ref, *refs, n_riders):
    for r in range(n_riders):
        refs[n_riders + 1 + r][...] = refs[r][...].astype(BF16)
    o_ref = refs[n_riders]
    o_ref[...] = _dot(x_ref[...], w_ref[...]).astype(o_ref.dtype)


def _rider_rows(n_rows, max_blocks):
    rb = 16
    while n_rows % rb or n_rows // rb > max_blocks:
        rb += 16
    return rb


def _matmul_with_casts(x, w, n, out_dtype, tm, tn, name, riders):
    m, k = x.shape
    ni, nj = m // tm, n // tn
    in_specs = [pl.BlockSpec((tm, k), lambda i, j: (i, 0)), pl.BlockSpec((k, tn), lambda i, j: (0, j))]
    out_specs = [pl.BlockSpec((tm, tn), lambda i, j: (i, j))]
    out_shape = [jax.ShapeDtypeStruct((m, n), out_dtype)]
    rider_specs = []
    for r in riders:
        rows, cols = r.shape
        rb = _rider_rows(rows, ni * nj)
        last = rows // rb - 1
        spec = pl.BlockSpec((rb, cols), lambda i, j, last=last: (jnp.minimum(i * nj + j, last), 0))
        rider_specs.append(spec)
        out_shape.append(jax.ShapeDtypeStruct((rows, cols), BF16))
    outs = pl.pallas_call(
        functools.partial(_mm_cast_kernel, n_riders=len(riders)),
        out_shape=tuple(out_shape),
        grid=(ni, nj),
        in_specs=in_specs[:2] + rider_specs,
        out_specs=tuple(out_specs + rider_specs),
        compiler_params=_cparams(("arbitrary", "arbitrary"), 48),
        name=name,
    )(x, w, *riders)
    return outs


def _merge_kernel(a_ref, b_ref, wa_ref, wb_ref, ga_ref, gb_ref, o_ref):
    ya = _dot(a_ref[...], wa_ref[...])
    yb = _dot(b_ref[...], wb_ref[...])
    o_ref[...] = (_sigmoid(ga_ref[...].astype(F32)) * ya + _sigmoid(gb_ref[...].astype(F32)) * yb).astype(o_ref.dtype)


def _merge(o_gdn, o_nsa, w_a, w_b, proj, gate_col0, tm=1024, tn=512):
    n, ka = o_gdn.shape
    kb = o_nsa.shape[1]
    d = w_a.shape[1]
    ja, jb = gate_col0 // tn, (gate_col0 + d) // tn
    return pl.pallas_call(
        _merge_kernel,
        out_shape=jax.ShapeDtypeStruct((n, d), BF16),
        grid=(n // tm, d // tn),
        in_specs=[pl.BlockSpec((tm, ka), lambda i, j: (i, 0)),
                  pl.BlockSpec((tm, kb), lambda i, j: (i, 0)),
                  pl.BlockSpec((ka, tn), lambda i, j: (0, j)),
                  pl.BlockSpec((kb, tn), lambda i, j: (0, j)),
                  pl.BlockSpec((tm, tn), lambda i, j: (i, ja + j)),
                  pl.BlockSpec((tm, tn), lambda i, j: (i, jb + j))],
        out_specs=pl.BlockSpec((tm, tn), lambda i, j: (i, j)),
        compiler_params=_cparams(("parallel", "arbitrary"), 48),
        name="branch_merge",
    )(o_gdn, o_nsa, w_a, w_b, proj, proj)


def _resid_mm_kernel(x_ref, w_ref, r_ref, o_ref):
    o_ref[...] = r_ref[...] + _dot(x_ref[...], w_ref[...])


def _resid_matmul(x, w, resid, tm=1024, tn=512):
    m, k = x.shape
    n = w.shape[1]
    return pl.pallas_call(
        _resid_mm_kernel,
        out_shape=jax.ShapeDtypeStruct((m, n), F32),
        grid=(m // tm, n // tn),
        in_specs=[pl.BlockSpec((tm, k), lambda i, j: (i, 0)),
                  pl.BlockSpec((k, tn), lambda i, j: (0, j)),
                  pl.BlockSpec((tm, tn), lambda i, j: (i, j))],
        out_specs=pl.BlockSpec((tm, tn), lambda i, j: (i, j)),
        compiler_params=_cparams(("parallel", "arbitrary"), 48),
        name="out_proj_residual",
    )(x, w, resid)


def _gdn_gate_kernel(a_ref, b_ref, alog_ref, dtb_ref, gc_ref, beta_ref, gl_ref, *, chunk):
    tm = a_ref.shape[0]
    x = a_ref[...] + dtb_ref[...]
    softplus = jnp.maximum(x, 0.0) + jnp.log(1.0 + jnp.exp(-jnp.abs(x)))
    g = -jnp.exp(alog_ref[...]) * softplus
    row = lax.broadcasted_iota(jnp.int32, (tm, tm), 0)
    col = lax.broadcasted_iota(jnp.int32, (tm, tm), 1)
    same = row // chunk == col // chunk
    tri = jnp.where((col <= row) & same, 1.0, 0.0).astype(F32)
    gc_ref[...] = _dot(tri, g, precision=lax.Precision.HIGHEST)
    gl_ref[...] = _dot(jnp.where(same, 1.0, 0.0).astype(F32), g, precision=lax.Precision.HIGHEST)
    beta_ref[...] = _sigmoid(b_ref[...])


def _gdn_gates(a_raw, b_raw, a_log, dt_bias, chunk, tm=512):
    n, h = a_raw.shape
    spec = pl.BlockSpec((tm, h), lambda i: (i, 0))
    vec = pl.BlockSpec((1, h), lambda i: (0, 0))
    sds = jax.ShapeDtypeStruct((n, h), F32)
    return pl.pallas_call(
        functools.partial(_gdn_gate_kernel, chunk=chunk),
        out_shape=(sds, sds, sds),
        grid=(n // tm,),
        in_specs=[spec, spec, vec, vec],
        out_specs=(spec, spec, spec),
        compiler_params=_cparams(("parallel",), 32),
        name="gdn_gates",
    )(a_raw, b_raw, a_log.reshape(1, h).astype(F32), dt_bias.reshape(1, h).astype(F32))


_CONV_PAD = 8


def _gdn_kernel(q_ref, k_ref, v_ref, z_ref, cwq_ref, cwk_ref, cwv_ref, gc_ref, beta_ref, gl_ref, gct_ref, nw_ref,
                o_ref, state, xbuf, *, chunk, hps):
    hblk = pl.program_id(1)
    t = pl.program_id(2)
    tt = q_ref.shape[0]
    d = HEAD_DIM
    width = cwq_ref.shape[0]
    c = chunk
    n_ch = tt // c
    n_lvl = int(math.log2(c))
    assert 2 ** n_lvl == c

    @pl.when(t == 0)
    def _():
        state[...] = jnp.zeros_like(state)
        xbuf[:, 0:_CONV_PAD, :] = jnp.zeros((3 * hps, _CONV_PAD, d), F32)

    def conv_silu(s, x_bf16, w):
        xbuf[s, _CONV_PAD:_CONV_PAD + tt, :] = x_bf16.astype(F32)
        y = None
        for j in range(width):
            term = xbuf[s, pl.ds(_CONV_PAD - (width - 1) + j, tt), :] * w[j:j + 1, :]
            y = term if y is None else y + term
        xbuf[s, 0:_CONV_PAD, :] = xbuf[s, tt:tt + _CONV_PAD, :]
        return _silu(y)

    def l2n(x):
        return x * lax.rsqrt(jnp.sum(x * x, axis=-1, keepdims=True) + RMS_EPS)

    ri = lax.broadcasted_iota(jnp.int32, (tt, tt), 0)
    ci = lax.broadcasted_iota(jnp.int32, (tt, tt), 1)
    incl = (ri >= ci) & (ri // c == ci // c)
    strict = ri > ci
    lane = lax.broadcasted_iota(jnp.int32, gc_ref.shape, 1)
    nw = nw_ref[...]

    hd = [dict() for _ in range(hps)]
    for hp, e in enumerate(hd):
        h = hblk * hps + hp
        hs = slice(hp * d, (hp + 1) * d)
        q = l2n(conv_silu(3 * hp, q_ref[:, hs], cwq_ref[:, hs])) * (d ** -0.5)
        k = l2n(conv_silu(3 * hp + 1, k_ref[:, hs], cwk_ref[:, hs]))
        v = conv_silu(3 * hp + 2, v_ref[:, hs], cwv_ref[:, hs])
        pick = lambda ref: jnp.sum(jnp.where(lane == h, ref[...], 0.0), axis=1, keepdims=True)
        gcc, bc, glc = pick(gc_ref), pick(beta_ref), pick(gl_ref)
        egc = jnp.exp(gcc)
        kb = k.astype(BF16)
        qk = _dot_nt(jnp.concatenate([q.astype(BF16), kb], axis=0), kb)
        dm = jnp.exp(jnp.where(incl, gcc - gct_ref[hp], NEG_INF))
        e["npow"] = jnp.where(strict, -(qk[tt:] * dm * bc), 0.0)
        e["x"] = jnp.concatenate([v * bc, k * (bc * egc)], axis=1)
        e["attn"] = (qk[:tt] * dm).astype(BF16)
        e["qe"] = q * egc
        e["kout"] = (k * jnp.exp(glc - gcc)).astype(BF16)
        e["dch"] = jnp.exp(glc)

    for lvl in range(n_lvl):
        for e in hd:
            nb = e["npow"].astype(BF16)
            if lvl + 1 < n_lvl:
                r = _dot(nb, jnp.concatenate([nb, e["x"].astype(BF16)], axis=1))
                e["npow"], e["x"] = r[:, :tt], e["x"] + r[:, tt:]
            else:
                e["x"] = e["x"] + _dot(nb, e["x"].astype(BF16))

    for hp, e in enumerate(hd):
        e["xb"] = e["x"].astype(BF16)
        au = _dot(e["attn"], e["xb"])
        e["o0"] = au[:, :d]
        e["qe"] = (e["qe"] - au[:, d:]).astype(BF16)
        e["s"] = state[hp]

    for ch in range(n_ch):
        sl = slice(ch * c, (ch + 1) * c)
        for hp, e in enumerate(hd):
            hs = slice(hp * d, (hp + 1) * d)
            ktx = _dot_tn(e["kout"][sl], e["xb"][sl])
            sb = e["s"].astype(BF16)
            o = _dot(e["qe"][sl], sb) + e["o0"][sl]
            e["s"] = e["s"] * e["dch"][ch * c:ch * c + 1, :] - _dot(ktx[:, d:].astype(BF16), sb) + ktx[:, :d]
            on = o * lax.rsqrt(jnp.mean(o * o, axis=-1, keepdims=True) + RMS_EPS) * nw
            o_ref[sl, hs] = (on * _silu(z_ref[sl, hs].astype(F32))).astype(o_ref.dtype)

    for hp, e in enumerate(hd):
        state[hp] = e["s"]


def _gdn_mixer(proj, conv_w, gc, beta, gl, gct, norm_w, batch, seq, heads, chunk, tt=256, hps=8):
    d = HEAD_DIM
    n = batch * seq
    nt = seq // tt
    hps = math.gcd(hps, heads)
    hb = heads // hps
    width = conv_w.shape[0]
    assert width - 1 <= _CONV_PAD and tt % chunk == 0 and seq % tt == 0 and heads % hps == 0

    def slab(off):
        return pl.BlockSpec((tt, hps * d), lambda b, h, t: (b * nt + t, off + h))

    def cw(off):
        return pl.BlockSpec((width, hps * d), lambda b, h, t: (0, off + h))

    gspec = pl.BlockSpec((tt, heads), lambda b, h, t: (b * nt + t, 0))
    return pl.pallas_call(
        functools.partial(_gdn_kernel, chunk=chunk, hps=hps),
        out_shape=jax.ShapeDtypeStruct((n, heads * d), BF16),
        grid=(batch, hb, nt),
        in_specs=[slab(0), slab(hb), slab(2 * hb), slab(3 * hb),
                  cw(0), cw(hb), cw(2 * hb),
                  gspec, gspec, gspec,
                  pl.BlockSpec((hps, 1, tt), lambda b, h, t: (b * hb + h, 0, t)),
                  pl.BlockSpec((1, d), lambda b, h, t: (0, 0))],
        out_specs=pl.BlockSpec((tt, hps * d), lambda b, h, t: (b * nt + t, h)),
        scratch_shapes=[pltpu.VMEM((hps, d, d), F32), pltpu.VMEM((3 * hps, tt + _CONV_PAD, d), F32)],
        compiler_params=_cparams(("parallel", "parallel", "arbitrary"), 40),
        name="gdn_mixer",
    )(proj, proj, proj, proj, conv_w, conv_w, conv_w, gc, beta, gl, gct, norm_w.reshape(1, d).astype(F32))


def _nsa_cmp_kernel(kc_ref, vc_ref, pek_ref, w1k_ref, w2k_ref, pev_ref, w1v_ref, w2v_ref,
                    ko_ref, vo_ref, xf, sh, *, stride, block):
    seq, d = kc_ref.shape
    nh = seq // stride
    reps = block // stride
    assert nh % 8 == 0

    def compress(src_ref, pe_ref, w1_ref, w2_ref, dst_ref):
        xf[...] = src_ref[...].astype(F32)
        pe = pe_ref[...]
        hid = None
        for r in range(reps):
            acc = None
            for l in range(stride):
                li = r * stride + l
                rows = xf[pl.ds(l, nh, stride=stride), :] + pe[li:li + 1, :]
                term = _dot(rows.astype(BF16), w1_ref[li].astype(BF16))
                acc = term if acc is None else acc + term
            if r == 0:
                hid = acc
            else:
                sh[0:nh, :] = acc
                sh[nh:nh + 8, :] = jnp.zeros((8, d), F32)
                hid = hid + sh[pl.ds(r, nh), :]
        hid = _silu(hid)
        dst_ref[0] = _dot(hid.astype(BF16), w2_ref[...].astype(BF16)).astype(dst_ref.dtype)

    compress(kc_ref, pek_ref, w1k_ref, w2k_ref, ko_ref)
    compress(vc_ref, pev_ref, w1v_ref, w2v_ref, vo_ref)


def _nsa_compress(proj, kc_col, vc_col, pe_k, w1_k, w2_k, pe_v, w1_v, w2_v, batch, seq, groups):
    d = HEAD_DIM
    nh = seq // CMP_STRIDE
    full2 = lambda b, g: (0, 0)
    full3 = lambda b, g: (0, 0, 0)
    out_sds = jax.ShapeDtypeStruct((batch * groups, nh, d), BF16)
    out_spec = pl.BlockSpec((1, nh, d), lambda b, g: (b * groups + g, 0, 0))
    return pl.pallas_call(
        functools.partial(_nsa_cmp_kernel, stride=CMP_STRIDE, block=CMP_BLOCK),
        out_shape=(out_sds, out_sds),
        grid=(batch, groups),
        in_specs=[pl.BlockSpec((seq, d), lambda b, g: (b, kc_col + g)),
                  pl.BlockSpec((seq, d), lambda b, g: (b, vc_col + g)),
                  pl.BlockSpec((CMP_BLOCK, d), full2), pl.BlockSpec((CMP_BLOCK, d, d), full3), pl.BlockSpec((d, d), full2),
                  pl.BlockSpec((CMP_BLOCK, d), full2), pl.BlockSpec((CMP_BLOCK, d, d), full3), pl.BlockSpec((d, d), full2)],
        out_specs=(out_spec, out_spec),
        scratch_shapes=[pltpu.VMEM((seq, d), F32), pltpu.VMEM((nh + 8, d), F32)],
        compiler_params=_cparams(("parallel", "parallel"), 32),
        name="nsa_compress",
    )(proj, proj, pe_k, w1_k, w2_k, pe_v, w1_v, w2_v)


def _nsa_kernel(q_ref, kc_ref, vc_ref, ks_ref, vs_ref, kw_ref, vw_ref, gate_ref, selmap_ref, expand_ref,
                o_ref, score_t, *, hpg, gps, tk, n_sel, n_top):
    tq = q_ref.shape[0]
    d = HEAD_DIM
    gw = hpg * d
    n_cmp_pad = kc_ref.shape[1]
    t0 = pl.program_id(2) * tq
    rows = hpg * tq
    grp = list(range(gps))
    kv = lambda ref, gi, r0, n: ref[pl.ds(r0, n), gi * d:(gi + 1) * d]

    q4 = []
    for gi in grp:
        qg = jnp.concatenate([q_ref[:, gi * gw + hh * d:gi * gw + (hh + 1) * d] for hh in range(hpg)], axis=0)
        q4.append((qg.astype(F32) * (d ** -0.5 * math.log2(math.e))).astype(BF16))
    trow = t0 + lax.broadcasted_iota(jnp.int32, (tq, 1), 0)

    ncol = lax.broadcasted_iota(jnp.int32, (1, n_cmp_pad), 1)
    vis_c = ((ncol * CMP_STRIDE + (CMP_BLOCK - 1)) <= trow)[None]
    s_c = [_dot_nt(q4[gi], kc_ref[gi]).reshape(hpg, tq, n_cmp_pad) for gi in grp]
    p_c = []
    for gi in grp:
        s3 = jnp.where(vis_c, s_c[gi], NEG_INF)
        e = jnp.where(vis_c, jnp.exp2(s3 - jnp.max(s3, axis=-1, keepdims=True)), 0.0)
        den = jnp.sum(e, axis=-1, keepdims=True)
        p_c.append(e / jnp.where(den > 0.0, den, 1.0))
    o_cmp = [_dot(p_c[gi].reshape(rows, n_cmp_pad).astype(BF16), vc_ref[gi]).reshape(hpg, tq, d) for gi in grp]

    jrow = lax.broadcasted_iota(jnp.int32, (n_sel, 1), 0)
    tcol = t0 + lax.broadcasted_iota(jnp.int32, (1, tq), 1)
    cur = tcol // SEL_BLOCK
    forced = (jrow == 0) | (jrow == cur) | (jrow == cur - 1)
    causal_blk = jrow * SEL_BLOCK <= tcol
    keys = []
    for gi in grp:
        imp_t = lax.dot_general(selmap_ref[...], jnp.sum(p_c[gi], axis=0), (((1,), (1,)), ((), ())),
                                preferred_element_type=F32, precision=lax.Precision.HIGHEST)
        sc = jnp.where(forced, FORCE_SCORE, jnp.where(causal_blk, imp_t, NEG_INF))
        bits = pltpu.bitcast(sc, jnp.int32)
        keys.append(bits ^ ((bits >> 31) & jnp.int32(0x7FFFFFFF)))
        score_t[gi] = keys[gi]

    def rank_step(i, cnts):
        tie = jnp.where(jrow > i, 1, 0)
        return tuple(cnts[gi] + jnp.where(score_t[gi, pl.ds(i, 1), :] + tie > keys[gi], 1.0, 0.0) for gi in grp)

    n_rank = jnp.minimum((t0 + tq - 1) // SEL_BLOCK + 1, n_sel)
    cnts = lax.fori_loop(0, n_rank, rank_step, tuple(jnp.zeros((n_sel, tq), F32) for _ in grp))
    eye = jnp.where(lax.broadcasted_iota(jnp.int32, (n_sel, n_sel), 0)
                    == lax.broadcasted_iota(jnp.int32, (n_sel, n_sel), 1), 1.0, 0.0).astype(BF16)
    sel_bias = [_dot_tn(jnp.where(cnts[gi] < n_top, 0.0, NEG_INF).astype(BF16), eye).astype(BF16) for gi in grp]

    def sel_step(kt, carry, diagonal):
        k0 = pl.multiple_of(kt * tk, tk)
        ex = expand_ref[:, pl.ds(k0, tk)]
        s3 = []
        for gi in grp:
            b = _dot(sel_bias[gi], ex)
            if diagonal:
                kpos = k0 + lax.broadcasted_iota(jnp.int32, (1, tk), 1)
                b = jnp.where(kpos <= trow, b, NEG_INF)
            s3.append(_dot_nt(q4[gi], kv(ks_ref, gi, k0, tk)).reshape(hpg, tq, tk) + b[None])
        out = []
        for gi in grp:
            m, acc = carry[gi]
            m_new = jnp.maximum(m, jnp.max(s3[gi], axis=-1, keepdims=True))
            alpha = jnp.exp2(m - m_new)
            e = jnp.exp2((s3[gi] - m_new).astype(BF16))
            pv = _dot(e.reshape(rows, tk), with_ones(kv(vs_ref, gi, k0, tk)))
            out.append((m_new, alpha * acc + pv.reshape(hpg, tq, 2 * d)))
        return tuple(out)

    def with_ones(v_tile):
        return jnp.concatenate([v_tile, jnp.ones(v_tile.shape, BF16)], axis=1)

    n_full = t0 // tk
    init = (jnp.full((hpg, tq, 1), NEG_INF, F32), jnp.zeros((hpg, tq, 2 * d), F32))
    carry = lax.fori_loop(0, n_full, functools.partial(sel_step, diagonal=False), tuple(init for _ in grp))
    carry = sel_step(n_full, carry, diagonal=True)
    o_sel = [carry[gi][1][:, :, :d] / carry[gi][1][:, :, d:] for gi in grp]

    span = WINDOW + tq
    ws = pl.multiple_of(jnp.maximum(t0 - WINDOW, 0), tq)
    kpos = ws + lax.broadcasted_iota(jnp.int32, (1, span), 1)
    rel = trow - kpos
    b_w = jnp.where((rel >= 0) & (rel < WINDOW), 0.0, NEG_INF)[None]
    s_w = [_dot_nt(q4[gi], kv(kw_ref, gi, ws, span)).reshape(hpg, tq, span) + b_w for gi in grp]
    o_win = []
    for gi in grp:
        e = jnp.exp2((s_w[gi] - jnp.max(s_w[gi], axis=-1, keepdims=True)).astype(BF16))
        pv = _dot(e.reshape(rows, span), with_ones(kv(vw_ref, gi, ws, span))).reshape(hpg, tq, 2 * d)
        o_win.append(pv[:, :, :d] / pv[:, :, d:])

    for gi in grp:
        gates = _sigmoid(gate_ref[gi])
        for hh in range(hpg):
            g0 = gates[:, 3 * hh:3 * hh + 1]
            g1 = gates[:, 3 * hh + 1:3 * hh + 2]
            g2 = gates[:, 3 * hh + 2:3 * hh + 3]
            o_ref[:, gi * gw + hh * d:gi * gw + (hh + 1) * d] = (
                g0 * o_cmp[gi][hh] + g1 * o_sel[gi][hh] + g2 * o_win[gi][hh]).astype(o_ref.dtype)


def _nsa_mixer(proj, q_col, ks_col, vs_col, kw_col, vw_col, k_cmp, v_cmp, gates, batch, seq, groups, hpg,
               tq=256, tk=1024, gps=2):
    d = HEAD_DIM
    n = batch * seq
    nq = seq // tq
    n_sel = seq // SEL_BLOCK
    n_top = min(SEL_TOP_N, n_sel)
    n_cmp = (seq - CMP_BLOCK) // CMP_STRIDE + 1
    n_cmp_pad = k_cmp.shape[1]
    assert n_sel % 8 == 0 and seq >= WINDOW + tq and WINDOW % tq == 0 and tk % tq == 0 and seq % tk == 0

    cmp_start = np.arange(n_cmp_pad) * CMP_STRIDE
    sel_start = np.arange(n_sel) * SEL_BLOCK
    overlap = (np.minimum(cmp_start[None, :] + CMP_BLOCK, sel_start[:, None] + SEL_BLOCK)
               - np.maximum(cmp_start[None, :], sel_start[:, None]))
    sel_map_t = np.clip(overlap, 0, None) / CMP_STRIDE
    sel_map_t[:, n_cmp:] = 0.0
    expand = (np.arange(seq)[None, :] // SEL_BLOCK == np.arange(n_sel)[:, None]).astype(np.float32)

    gb = groups // gps
    qw = gps * hpg * d
    assert groups % gps == 0 and all(c % gps == 0 for c in (ks_col, vs_col, kw_col, vw_col)) and q_col % (gps * hpg) == 0

    def kv(col):
        return pl.BlockSpec((seq, gps * d), lambda b, g, t: (b, col // gps + g))

    cmp_spec = pl.BlockSpec((gps, n_cmp_pad, d), lambda b, g, t: (b * gb + g, 0, 0))
    return pl.pallas_call(
        functools.partial(_nsa_kernel, hpg=hpg, gps=gps, tk=tk, n_sel=n_sel, n_top=n_top),
        out_shape=jax.ShapeDtypeStruct((n, groups * hpg * d), BF16),
        grid=(batch, gb, nq),
        in_specs=[pl.BlockSpec((tq, qw), lambda b, g, t: (b * nq + t, q_col // (gps * hpg) + g)),
                  cmp_spec, cmp_spec, kv(ks_col), kv(vs_col), kv(kw_col), kv(vw_col),
                  pl.BlockSpec((gps, tq, 3 * hpg), lambda b, g, t: (b * gb + g, t, 0)),
                  pl.BlockSpec((n_sel, n_cmp_pad), lambda b, g, t: (0, 0)),
                  pl.BlockSpec((n_sel, seq), lambda b, g, t: (0, 0))],
        out_specs=pl.BlockSpec((tq, qw), lambda b, g, t: (b * nq + t, g)),
        scratch_shapes=[pltpu.VMEM((gps, n_sel, tq), jnp.int32)],
        compiler_params=_cparams(("parallel", "parallel", "arbitrary"), 48),
        name="nsa_attention",
    )(proj, k_cmp, v_cmp, proj, proj, proj, proj, gates, jnp.asarray(sel_map_t, F32), jnp.asarray(expand, BF16))


def _router_kernel(h_ref, g_ref, wr_ref, wrl_ref, br_ref, xn_ref, ids_ref, wts_ref, cnt_ref):
    x = h_ref[...]
    xn = x * lax.rsqrt(jnp.mean(x * x, axis=-1, keepdims=True) + RMS_EPS) * g_ref[...]
    xn_ref[...] = xn
    x_hi = xn.astype(BF16)
    x_lo = (xn - x_hi.astype(F32)).astype(BF16)
    logits = (_dot(x_hi, wr_ref[...]) + _dot(x_hi, wrl_ref[...]) + _dot(x_lo, wr_ref[...])) + br_ref[...]
    lane = lax.broadcasted_iota(jnp.int32, logits.shape, 1)
    big = jnp.int32(2 * LANES)
    is_g = lane < N_GROUPS
    gl = jnp.where(is_g, logits, NEG_INF)
    gm = jnp.max(gl, axis=-1, keepdims=True)
    p_g = 1.0 / jnp.sum(jnp.where(is_g, jnp.exp(gl - gm), 0.0), axis=-1, keepdims=True)
    grp = jnp.min(jnp.where(gl == gm, lane, big), axis=-1, keepdims=True)
    eidx = lane - N_GROUPS
    in_grp = (eidx >= 0) & (eidx // EXPERTS_PER_GROUP == grp) & (eidx < N_GROUPS * EXPERTS_PER_GROUP)
    el = jnp.where(in_grp, logits, NEG_INF)
    em = jnp.max(el, axis=-1, keepdims=True)
    ee = jnp.where(in_grp, jnp.exp(el - em), 0.0)
    pe = ee / jnp.sum(ee, axis=-1, keepdims=True)
    p1 = jnp.max(jnp.where(in_grp, pe, -1.0), axis=-1, keepdims=True)
    i1 = jnp.min(jnp.where(in_grp & (pe == p1), lane, big), axis=-1, keepdims=True)
    rest = in_grp & (lane != i1)
    p2 = jnp.max(jnp.where(rest, pe, -1.0), axis=-1, keepdims=True)
    i2 = jnp.min(jnp.where(rest & (pe == p2), lane, big), axis=-1, keepdims=True)
    denom = p1 + p2
    wts_ref[...] = jnp.where(lane == 0, p_g * p1 / denom, p_g * p2 / denom)

    @pl.when(pl.program_id(0) == 0)
    def _():
        cnt_ref[...] = jnp.zeros_like(cnt_ref)

    tm = logits.shape[0]
    hot = jnp.where((lane == i1) | (lane == i2), 1.0, 0.0)
    below = jnp.where(lax.broadcasted_iota(jnp.int32, (tm, tm), 0) > lax.broadcasted_iota(jnp.int32, (tm, tm), 1),
                      1.0, 0.0).astype(BF16)
    before = _dot(below, hot.astype(BF16)) + cnt_ref[...]
    r1 = jnp.sum(jnp.where(lane == i1, before, 0.0), axis=-1, keepdims=True).astype(jnp.int32)
    r2 = jnp.sum(jnp.where(lane == i2, before, 0.0), axis=-1, keepdims=True).astype(jnp.int32)
    cnt_ref[...] += jnp.sum(hot, axis=0, keepdims=True)
    ids_ref[...] = jnp.where(lane == 0, i1 - N_GROUPS, jnp.where(lane == 1, i2 - N_GROUPS, jnp.where(lane == 2, r1, r2)))


def _router(h, g_ffn, w_group, b_group, w_expert, b_expert, tm=256):
    n, d = h.shape
    n_g, n_e = w_group.shape[1], w_expert.shape[1]
    n_r = n_g + n_e
    assert n_r <= LANES and TOP_K == 2 and n_g == N_GROUPS
    wr = jnp.concatenate([w_group, w_expert, jnp.zeros((d, LANES - n_r), F32)], axis=1).astype(F32)
    wr_hi = wr.astype(BF16)
    wr_lo = (wr - wr_hi.astype(F32)).astype(BF16)
    br = jnp.concatenate([b_group, b_expert, jnp.zeros((LANES - n_r,), F32)]).reshape(1, LANES).astype(F32)
    row = lambda i: (i, 0)
    fixed = lambda i: (0, 0)
    xn, ids, wts, cnt = pl.pallas_call(
        _router_kernel,
        out_shape=(jax.ShapeDtypeStruct((n, d), F32), jax.ShapeDtypeStruct((n, LANES), jnp.int32),
                   jax.ShapeDtypeStruct((n, LANES), F32), jax.ShapeDtypeStruct((1, LANES), F32)),
        grid=(n // tm,),
        in_specs=[pl.BlockSpec((tm, d), row), pl.BlockSpec((1, d), fixed),
                  pl.BlockSpec((d, LANES), fixed), pl.BlockSpec((d, LANES), fixed), pl.BlockSpec((1, LANES), fixed)],
        out_specs=(pl.BlockSpec((tm, d), row), pl.BlockSpec((tm, LANES), row), pl.BlockSpec((tm, LANES), row),
                   pl.BlockSpec((1, LANES), fixed)),
        compiler_params=_cparams(("arbitrary",), 40),
        name="moe_router",
    )(h, g_ffn.reshape(1, d).astype(F32), wr_hi, wr_lo, br)
    return xn, ids, wts, cnt[0, n_g:n_r]


def _row_gather(idx_ref, base, src_hbm, dst, sem, n_rows, wait):
    for r in range(n_rows):
        cp = pltpu.make_async_copy(src_hbm.at[pl.ds(idx_ref[base + r], 1), :], dst.at[pl.ds(r, 1), :], sem)
        if wait:
            cp.wait()
        else:
            cp.start()


def _slot_scatter_kernel(dest_ref, tok_ref, *, top_k, unroll):
    n_out = tok_ref.shape[0]
    n_slot = dest_ref.shape[0]

    def clear(j, carry):
        for u in range(unroll):
            tok_ref[j * unroll + u] = 0
        return carry

    def place(j, carry):
        rows = [dest_ref[j * unroll + u] for u in range(unroll)]
        for u in range(unroll):
            tok_ref[rows[u]] = (j * unroll + u) // top_k
        return carry

    lax.fori_loop(0, n_out // unroll, clear, 0)
    lax.fori_loop(0, n_slot // unroll, place, 0)


def _slot_scatter(dest_flat, n_out, top_k, unroll=64):
    assert n_out % unroll == 0 and dest_flat.shape[0] % unroll == 0
    return pl.pallas_call(
        functools.partial(_slot_scatter_kernel, top_k=top_k, unroll=unroll),
        out_shape=jax.ShapeDtypeStruct((n_out,), jnp.int32),
        in_specs=[pl.BlockSpec(memory_space=pltpu.SMEM)],
        out_specs=pl.BlockSpec(memory_space=pltpu.SMEM),
        name="moe_slot_scatter",
    )(dest_flat)


def _expert_kernel(be_ref, tok_ref, nact_ref, x_hbm, wgu_ref, wd_ref, y_ref, xbuf, sem, *, blk, d_ff):
    i = pl.program_id(0)
    n_act = nact_ref[0]
    slot = i % 2

    @pl.when(i == 0)
    def _():
        _row_gather(tok_ref, 0, x_hbm, xbuf.at[0], sem.at[0], blk, wait=False)

    def run_block(prefetch_next):
        _row_gather(tok_ref, i * blk, x_hbm, xbuf.at[slot], sem.at[slot], blk, wait=True)
        if prefetch_next:
            _row_gather(tok_ref, (i + 1) * blk, x_hbm, xbuf.at[1 - slot], sem.at[1 - slot], blk, wait=False)
        xb = xbuf[slot].astype(BF16)
        gu = _dot(xb, wgu_ref[0])
        act = _silu(gu[:, :d_ff]) * gu[:, d_ff:]
        y_ref[...] = _dot(act.astype(BF16), wd_ref[0])

    pl.when(i + 1 < n_act)(functools.partial(run_block, True))
    pl.when(i + 1 == n_act)(functools.partial(run_block, False))

    @pl.when(i >= n_act)
    def _():
        y_ref[...] = jnp.zeros_like(y_ref)


def _experts(xn, w_gate_up, w_down, blk_expert, tok_buf, n_active, blk):
    n, d = xn.shape
    n_exp, _, ff2 = w_gate_up.shape
    d_ff = ff2 // 2
    n_blk = blk_expert.shape[0]
    return pl.pallas_call(
        functools.partial(_expert_kernel, blk=blk, d_ff=d_ff),
        out_shape=jax.ShapeDtypeStruct((n_blk * blk, d), F32),
        grid_spec=pltpu.PrefetchScalarGridSpec(
            num_scalar_prefetch=3,
            grid=(n_blk,),
            in_specs=[pl.BlockSpec(memory_space=pl.ANY),
                      pl.BlockSpec((1, d, ff2), lambda i, be, tok, na: (be[i], 0, 0)),
                      pl.BlockSpec((1, d_ff, d), lambda i, be, tok, na: (be[i], 0, 0))],
            out_specs=pl.BlockSpec((blk, d), lambda i, be, tok, na: (i, 0)),
            scratch_shapes=[pltpu.VMEM((2, blk, d), F32), pltpu.SemaphoreType.DMA((2,))]),
        compiler_params=_cparams(("arbitrary",), 56),
        name="moe_experts",
    )(blk_expert, tok_buf, n_active, xn, w_gate_up, w_down)


def _combine_kernel(pos_ref, h_ref, g_ref, w_ref, y_hbm, o_ref, ybuf, sem, *, tc):
    i = pl.program_id(0)
    n = pl.num_programs(0)
    slot = i % 2
    rows = TOP_K * tc

    @pl.when(i == 0)
    def _():
        _row_gather(pos_ref, 0, y_hbm, ybuf.at[0], sem.at[0], rows, wait=False)

    def run_tile(prefetch_next):
        _row_gather(pos_ref, i * rows, y_hbm, ybuf.at[slot], sem.at[slot], rows, wait=True)
        if prefetch_next:
            _row_gather(pos_ref, (i + 1) * rows, y_hbm, ybuf.at[1 - slot], sem.at[1 - slot], rows, wait=False)
        hh = h_ref[...]
        w = w_ref[...]
        for kk in range(TOP_K):
            hh = hh + w[:, kk:kk + 1] * ybuf[slot, kk * tc:(kk + 1) * tc, :]
        o_ref[...] = hh * lax.rsqrt(jnp.mean(hh * hh, axis=-1, keepdims=True) + RMS_EPS) * g_ref[...]

    pl.when(i + 1 < n)(functools.partial(run_tile, True))
    pl.when(i + 1 == n)(functools.partial(run_tile, False))


def _combine(h, y_buf, pos, wts, g_final, tc=128):
    n, d = h.shape
    return pl.pallas_call(
        functools.partial(_combine_kernel, tc=tc),
        out_shape=jax.ShapeDtypeStruct((n, d), F32),
        grid_spec=pltpu.PrefetchScalarGridSpec(
            num_scalar_prefetch=1,
            grid=(n // tc,),
            in_specs=[pl.BlockSpec((tc, d), lambda i, pos: (i, 0)),
                      pl.BlockSpec((1, d), lambda i, pos: (0, 0)),
                      pl.BlockSpec((tc, LANES), lambda i, pos: (i, 0)),
                      pl.BlockSpec(memory_space=pl.ANY)],
            out_specs=pl.BlockSpec((tc, d), lambda i, pos: (i, 0)),
            scratch_shapes=[pltpu.VMEM((2, TOP_K * tc, d), F32), pltpu.SemaphoreType.DMA((2,))]),
        compiler_params=_cparams(("arbitrary",), 40),
        name="moe_combine_final_norm",
    )(pos, h, g_final.reshape(1, d).astype(F32), wts, y_buf)


def _dispatch_plan(ids, counts, blk):
    n_tok = ids.shape[0]
    n_experts = counts.shape[0]
    n_slot = n_tok * TOP_K
    counts = counts.astype(jnp.int32)
    padded = (counts + blk - 1) // blk * blk
    pad_end = jnp.cumsum(padded)
    pad_start = pad_end - padded
    dest = pad_start[ids[:, :TOP_K]] + ids[:, TOP_K:2 * TOP_K]
    n_blk = -(-n_slot // blk) + n_experts
    blk_start = jnp.arange(n_blk, dtype=jnp.int32) * blk
    blk_expert = jnp.minimum(jnp.sum(pad_end[None, :] <= blk_start[:, None], axis=1), n_experts - 1).astype(jnp.int32)
    n_active = (pad_end[-1] // blk).astype(jnp.int32).reshape(1)
    return dest.astype(jnp.int32), blk_expert, n_active, n_blk


def _layer(x, g_mix, w_in, conv_w, a_log, dt_bias, norm_w, pe_k, w1_k, w2_k, pe_v, w1_v, w2_v,
           w_a, w_b, w_out, g_ffn, w_group, b_group, w_expert, b_expert, w_gate_up, w_down, g_out):
    batch, seq, dm = x.shape
    n = batch * seq
    d = HEAD_DIM
    gdn_w = w_a.shape[0]
    nsa_w = w_b.shape[0]
    heads = gdn_w // d
    nsa_heads = nsa_w // d
    groups = NSA_GROUPS
    hpg = nsa_heads // groups
    kvw = groups * d

    sizes = (3 * gdn_w, gdn_w, heads, heads, nsa_w, 6 * kvw, 3 * nsa_heads, 2 * dm)
    assert sum(sizes) == w_in.shape[1]
    offs = np.concatenate([[0], np.cumsum(sizes)])
    tn_in = 512
    wide = (0, 1, 4, 5, 7)
    assert all(sizes[i] % tn_in == 0 for i in wide) and sizes[2] + sizes[3] <= tn_in and sizes[6] <= tn_in
    starts = [c for i in wide for c in range(offs[i], offs[i + 1], tn_in)]
    n_big = len(starts) * tn_in
    w_pack = _pack_columns(w_in.T, starts + [offs[2], offs[6]], tn=tn_in)
    w_small = jnp.concatenate([w_pack[:, n_big:n_big + sizes[2] + sizes[3]],
                               w_pack[:, n_big + tn_in:n_big + tn_in + sizes[6]]], axis=1)
    z_col = 3 * heads
    nq_col = z_col + heads
    kv_col = nq_col + nsa_heads
    gate_col0 = (kv_col + 6 * groups) * d

    x2 = x.reshape(n, dm)
    xn = _rmsnorm(x2, g_mix, BF16)
    n_experts, _, ff2 = w_gate_up.shape
    proj, wgu_bf, wd_bf = _matmul_with_casts(
        xn, w_pack, n_big, BF16, tm=1024, tn=tn_in, name="in_proj",
        riders=[w_gate_up.reshape(n_experts * dm, ff2), w_down.reshape(n_experts * (ff2 // 2), dm)])
    wgu_bf = wgu_bf.reshape(w_gate_up.shape)
    wd_bf = wd_bf.reshape(w_down.shape)
    small = _matmul(xn, w_small, F32, tm=1024, tn=w_small.shape[1], name="in_proj_small")
    a_raw, b_raw, nsa_gate = small[:, :heads], small[:, heads:2 * heads], small[:, 2 * heads:]

    gc, beta, gl = _gdn_gates(a_raw, b_raw, a_log, dt_bias, GDN_CHUNK)
    gct = gc.reshape(batch, seq, heads).transpose(0, 2, 1).reshape(batch * heads, 1, seq)
    o_gdn = _gdn_mixer(proj, conv_w, gc, beta, gl, gct, norm_w, batch, seq, heads, GDN_CHUNK)

    k_cmp, v_cmp = _nsa_compress(proj, kv_col, kv_col + groups, pe_k, w1_k, w2_k, pe_v, w1_v, w2_v,
                                 batch, seq, groups)
    gates = nsa_gate.reshape(batch, seq, groups, 3 * hpg).transpose(0, 2, 1, 3).reshape(batch * groups, seq, 3 * hpg)
    o_nsa = _nsa_mixer(proj, nq_col, kv_col + 2 * groups, kv_col + 3 * groups, kv_col + 4 * groups,
                       kv_col + 5 * groups, k_cmp, v_cmp, gates, batch, seq, groups, hpg)

    merged = _merge(o_gdn, o_nsa, w_a.astype(BF16), w_b.astype(BF16), proj, gate_col0)
    h = _resid_matmul(merged, w_out.astype(BF16), x2)

    xn2, ids, wts, counts = _router(h, g_ffn, w_group, b_group, w_expert, b_expert)
    dest, blk_expert, n_active, n_blk = _dispatch_plan(ids, counts, MOE_BLOCK)
    tok_buf = _slot_scatter(dest.reshape(n * TOP_K), n_blk * MOE_BLOCK, TOP_K)
    y_buf = _experts(xn2, wgu_bf, wd_bf, blk_expert, tok_buf, n_active, MOE_BLOCK)
    tc = 128
    pos_tiles = dest.reshape(n // tc, tc, TOP_K).transpose(0, 2, 1).reshape(n * TOP_K)
    out = _combine(h, y_buf, pos_tiles, wts, g_out, tc)
    return out.reshape(batch, seq, dm)


def kernel(x, g_mix, w_in, gdn_conv_w, gdn_a_log, gdn_dt_bias, gdn_norm_w, cmp_pe_k, cmp_w1_k, cmp_w2_k,
           cmp_pe_v, cmp_w1_v, cmp_w2_v, w_branch_gdn, w_branch_nsa, w_out, g_ffn, w_group, b_group,
           w_expert, b_expert, w_gate_up, w_down, g_final):
    depth = g_mix.shape[0]
    assert depth == 1, "the fused final norm assumes a single layer"
    return _layer(x, g_mix[0], w_in[0], gdn_conv_w[0], gdn_a_log[0], gdn_dt_bias[0], gdn_norm_w[0],
                  cmp_pe_k[0], cmp_w1_k[0], cmp_w2_k[0], cmp_pe_v[0], cmp_w1_v[0], cmp_w2_v[0],
                  w_branch_gdn[0], w_branch_nsa[0], w_out[0], g_ffn[0], w_group[0], b_group[0],
                  w_expert[0], b_expert[0], w_gate_up[0], w_down[0], g_final)
```

```python
import functools
import math

import numpy as np
import jax
import jax.numpy as jnp
from jax import lax
from jax.experimental import pallas as pl
from jax.experimental.pallas import tpu as pltpu

F32 = jnp.float32
BF16 = jnp.bfloat16

RMS_EPS = 1e-6
NEG_INF = -1e30
FORCE_SCORE = 1e6

HEAD_DIM = 128
GDN_CHUNK = 64
NSA_GROUPS = 4
CMP_BLOCK = 32
CMP_STRIDE = 16
SEL_BLOCK = 64
SEL_TOP_N = 16
WINDOW = 512
N_GROUPS = 8
EXPERTS_PER_GROUP = 8
TOP_K = 2
MOE_BLOCK = 128
LANES = 128

V7X_VMEM_BYTES = 64 * 1024 * 1024


def _cparams(semantics, vmem_mb, flags=None):
    assert vmem_mb * 1024 * 1024 < V7X_VMEM_BYTES
    return pltpu.CompilerParams(dimension_semantics=semantics, vmem_limit_bytes=vmem_mb * 1024 * 1024, flags=flags)


def _dot(a, b, **kw):
    return jnp.dot(a, b, preferred_element_type=F32, **kw)


def _dot_nt(a, b):
    return lax.dot_general(a, b, (((1,), (1,)), ((), ())), preferred_element_type=F32)


def _dot_tn(a, b):
    return lax.dot_general(a, b, (((0,), (0,)), ((), ())), preferred_element_type=F32)


def _sigmoid(x):
    return 1.0 / (1.0 + jnp.exp(-x))


def _silu(x):
    return x * _sigmoid(x)


def _rmsnorm_kernel(x_ref, g_ref, o_ref):
    x = x_ref[...].astype(F32)
    ms = jnp.mean(x * x, axis=-1, keepdims=True)
    o_ref[...] = (x * lax.rsqrt(ms + RMS_EPS) * g_ref[...]).astype(o_ref.dtype)


def _rmsnorm(x, gain, out_dtype, tm=512):
    n, d = x.shape
    return pl.pallas_call(
        _rmsnorm_kernel,
        out_shape=jax.ShapeDtypeStruct((n, d), out_dtype),
        grid=(n // tm,),
        in_specs=[pl.BlockSpec((tm, d), lambda i: (i, 0)), pl.BlockSpec((1, d), lambda i: (0, 0))],
        out_specs=pl.BlockSpec((tm, d), lambda i: (i, 0)),
        compiler_params=_cparams(("parallel",), 40),
        name="rmsnorm",
    )(x, gain.reshape(1, d).astype(F32))


def _mm_kernel(x_ref, w_ref, o_ref):
    o_ref[...] = _dot(x_ref[...], w_ref[...]).astype(o_ref.dtype)


def _matmul(x, w, out_dtype, tm, tn, name):
    m, k = x.shape
    n = w.shape[1]
    return pl.pallas_call(
        _mm_kernel,
        out_shape=jax.ShapeDtypeStruct((m, n), out_dtype),
        grid=(m // tm, n // tn),
        in_specs=[pl.BlockSpec((tm, k), lambda i, j: (i, 0)), pl.BlockSpec((k, tn), lambda i, j: (0, j))],
        out_specs=pl.BlockSpec((tm, tn), lambda i, j: (i, j)),
        compiler_params=_cparams(("parallel", "arbitrary"), 48),
        name=name,
    )(x, w)


def _pack_cols_kernel(shift_ref, base_ref, *refs, shifts, tn):
    o_ref = refs[-1]
    tall = jnp.concatenate([r[...] for r in refs[:-1]], axis=0)
    j = pl.program_id(1)
    for s in shifts:
        @pl.when(shift_ref[j] == s)
        def _(s=s):
            o_ref[...] = tall[s:s + tn, :].T.astype(o_ref.dtype)


def _pack_columns(w_t, starts, tn=512, tr=2048):
    n_in, k = w_t.shape
    tr = min(tr, k)
    nb = tn // LANES + 1
    starts = np.asarray(starts)
    assert np.all(starts + tn <= n_in) and k % tr == 0 and np.all(starts % 8 == 0)
    base = jnp.asarray(starts // LANES, jnp.int32)
    shift = jnp.asarray(starts % LANES, jnp.int32)
    shifts = tuple(sorted(set(int(s) for s in starts % LANES)))
    last_blk = (n_in - 1) // LANES
    in_specs = [pl.BlockSpec((LANES, tr), lambda i, j, sh, bs, b=b: (jnp.minimum(bs[j] + b, last_blk), i))
                for b in range(nb)]
    return pl.pallas_call(
        functools.partial(_pack_cols_kernel, shifts=shifts, tn=tn),
        out_shape=jax.ShapeDtypeStruct((k, len(starts) * tn), BF16),
        grid_spec=pltpu.PrefetchScalarGridSpec(
            num_scalar_prefetch=2,
            grid=(k // tr, len(starts)),
            in_specs=in_specs,
            out_specs=pl.BlockSpec((tr, tn), lambda i, j, sh, bs: (i, j))),
        compiler_params=_cparams(("parallel", "arbitrary"), 32),
        name="pack_in_proj_weights",
    )(shift, base, *([w_t] * nb))


def _mm_cast_kernel(x_ref, w_ref, *refs, n_riders):
    for r in range(n_riders):
        refs[n_riders + 1 + r][...] = refs[r][...].astype(BF16)
    o_ref = refs[n_riders]
    o_ref[...] = _dot(x_ref[...], w_ref[...]).astype(o_ref.dtype)


def _rider_rows(n_rows, max_blocks):
    rb = 16
    while n_rows % rb or n_rows // rb > max_blocks:
        rb += 16
    return rb


def _matmul_with_casts(x, w, n, out_dtype, tm, tn, name, riders):
    m, k = x.shape
    ni, nj = m // tm, n // tn
    in_specs = [pl.BlockSpec((tm, k), lambda i, j: (i, 0)), pl.BlockSpec((k, tn), lambda i, j: (0, j))]
    out_specs = [pl.BlockSpec((tm, tn), lambda i, j: (i, j))]
    out_shape = [jax.ShapeDtypeStruct((m, n), out_dtype)]
    rider_specs = []
    for r in riders:
        rows, cols = r.shape
        rb = _rider_rows(rows, ni * nj)
        last = rows // rb - 1
        spec = pl.BlockSpec((rb, cols), lambda i, j, last=last: (jnp.minimum(i * nj + j, last), 0))
        rider_specs.append(spec)
        out_shape.append(jax.ShapeDtypeStruct((rows, cols), BF16))
    outs = pl.pallas_call(
        functools.partial(_mm_cast_kernel, n_riders=len(riders)),
        out_shape=tuple(out_shape),
        grid=(ni, nj),
        in_specs=in_specs[:2] + rider_specs,
        out_specs=tuple(out_specs + rider_specs),
        compiler_params=_cparams(("arbitrary", "arbitrary"), 48),
        name=name,
    )(x, w, *riders)
    return outs


def _merge_kernel(a_ref, b_ref, wa_ref, wb_ref, ga_ref, gb_ref, o_ref):
    ya = _dot(a_ref[...], wa_ref[...])
    yb = _dot(b_ref[...], wb_ref[...])
    o_ref[...] = (_sigmoid(ga_ref[...].astype(F32)) * ya + _sigmoid(gb_ref[...].astype(F32)) * yb).astype(o_ref.dtype)


def _merge(o_gdn, o_nsa, w_a, w_b, proj, gate_col0, tm=1024, tn=512):
    n, ka = o_gdn.shape
    kb = o_nsa.shape[1]
    d = w_a.shape[1]
    ja, jb = gate_col0 // tn, (gate_col0 + d) // tn
    return pl.pallas_call(
        _merge_kernel,
        out_shape=jax.ShapeDtypeStruct((n, d), BF16),
        grid=(n // tm, d // tn),
        in_specs=[pl.BlockSpec((tm, ka), lambda i, j: (i, 0)),
                  pl.BlockSpec((tm, kb), lambda i, j: (i, 0)),
                  pl.BlockSpec((ka, tn), lambda i, j: (0, j)),
                  pl.BlockSpec((kb, tn), lambda i, j: (0, j)),
                  pl.BlockSpec((tm, tn), lambda i, j: (i, ja + j)),
                  pl.BlockSpec((tm, tn), lambda i, j: (i, jb + j))],
        out_specs=pl.BlockSpec((tm, tn), lambda i, j: (i, j)),
        compiler_params=_cparams(("parallel", "arbitrary"), 48),
        name="branch_merge",
    )(o_gdn, o_nsa, w_a, w_b, proj, proj)


def _resid_mm_kernel(x_ref, w_ref, r_ref, o_ref):
    o_ref[...] = r_ref[...] + _dot(x_ref[...], w_ref[...])


def _resid_matmul(x, w, resid, tm=1024, tn=512):
    m, k = x.shape
    n = w.shape[1]
    return pl.pallas_call(
        _resid_mm_kernel,
        out_shape=jax.ShapeDtypeStruct((m, n), F32),
        grid=(m // tm, n // tn),
        in_specs=[pl.BlockSpec((tm, k), lambda i, j: (i, 0)),
                  pl.BlockSpec((k, tn), lambda i, j: (0, j)),
                  pl.BlockSpec((tm, tn), lambda i, j: (i, j))],
        out_specs=pl.BlockSpec((tm, tn), lambda i, j: (i, j)),
        compiler_params=_cparams(("parallel", "arbitrary"), 48),
        name="out_proj_residual",
    )(x, w, resid)


def _gdn_gate_kernel(a_ref, b_ref, alog_ref, dtb_ref, gc_ref, beta_ref, gl_ref, *, chunk):
    tm = a_ref.shape[0]
    x = a_ref[...] + dtb_ref[...]
    softplus = jnp.maximum(x, 0.0) + jnp.log(1.0 + jnp.exp(-jnp.abs(x)))
    g = -jnp.exp(alog_ref[...]) * softplus
    row = lax.broadcasted_iota(jnp.int32, (tm, tm), 0)
    col = lax.broadcasted_iota(jnp.int32, (tm, tm), 1)
    same = row // chunk == col // chunk
    tri = jnp.where((col <= row) & same, 1.0, 0.0).astype(F32)
    gc_ref[...] = _dot(tri, g, precision=lax.Precision.HIGHEST)
    gl_ref[...] = _dot(jnp.where(same, 1.0, 0.0).astype(F32), g, precision=lax.Precision.HIGHEST)
    beta_ref[...] = _sigmoid(b_ref[...])


def _gdn_gates(a_raw, b_raw, a_log, dt_bias, chunk, tm=512):
    n, h = a_raw.shape
    spec = pl.BlockSpec((tm, h), lambda i: (i, 0))
    vec = pl.BlockSpec((1, h), lambda i: (0, 0))
    sds = jax.ShapeDtypeStruct((n, h), F32)
    return pl.pallas_call(
        functools.partial(_gdn_gate_kernel, chunk=chunk),
        out_shape=(sds, sds, sds),
        grid=(n // tm,),
        in_specs=[spec, spec, vec, vec],
        out_specs=(spec, spec, spec),
        compiler_params=_cparams(("parallel",), 32),
        name="gdn_gates",
    )(a_raw, b_raw, a_log.reshape(1, h).astype(F32), dt_bias.reshape(1, h).astype(F32))


_CONV_PAD = 8


def _gdn_kernel(q_ref, k_ref, v_ref, z_ref, cwq_ref, cwk_ref, cwv_ref, gc_ref, beta_ref, gl_ref, gct_ref, nw_ref,
                o_ref, state, xbuf, *, chunk, hps):
    hblk = pl.program_id(1)
    t = pl.program_id(2)
    tt = q_ref.shape[0]
    d = HEAD_DIM
    width = cwq_ref.shape[0]
    c = chunk
    n_ch = tt // c
    n_lvl = int(math.log2(c))
    assert 2 ** n_lvl == c

    @pl.when(t == 0)
    def _():
        state[...] = jnp.zeros_like(state)
        xbuf[:, 0:_CONV_PAD, :] = jnp.zeros((3 * hps, _CONV_PAD, d), F32)

    def conv_silu(s, x_bf16, w):
        xbuf[s, _CONV_PAD:_CONV_PAD + tt, :] = x_bf16.astype(F32)
        y = None
        for j in range(width):
            term = xbuf[s, pl.ds(_CONV_PAD - (width - 1) + j, tt), :] * w[j:j + 1, :]
            y = term if y is None else y + term
        xbuf[s, 0:_CONV_PAD, :] = xbuf[s, tt:tt + _CONV_PAD, :]
        return _silu(y)

    def l2n(x):
        return x * lax.rsqrt(jnp.sum(x * x, axis=-1, keepdims=True) + RMS_EPS)

    ri = lax.broadcasted_iota(jnp.int32, (tt, tt), 0)
    ci = lax.broadcasted_iota(jnp.int32, (tt, tt), 1)
    incl = (ri >= ci) & (ri // c == ci // c)
    strict = ri > ci
    lane = lax.broadcasted_iota(jnp.int32, gc_ref.shape, 1)
    nw = nw_ref[...]

    hd = [dict() for _ in range(hps)]
    for hp, e in enumerate(hd):
        h = hblk * hps + hp
        hs = slice(hp * d, (hp + 1) * d)
        q = l2n(conv_silu(3 * hp, q_ref[:, hs], cwq_ref[:, hs])) * (d ** -0.5)
        k = l2n(conv_silu(3 * hp + 1, k_ref[:, hs], cwk_ref[:, hs]))
        v = conv_silu(3 * hp + 2, v_ref[:, hs], cwv_ref[:, hs])
        pick = lambda ref: jnp.sum(jnp.where(lane == h, ref[...], 0.0), axis=1, keepdims=True)
        gcc, bc, glc = pick(gc_ref), pick(beta_ref), pick(gl_ref)
        egc = jnp.exp(gcc)
        kb = k.astype(BF16)
        qk = _dot_nt(jnp.concatenate([q.astype(BF16), kb], axis=0), kb)
        dm = jnp.exp(jnp.where(incl, gcc - gct_ref[hp], NEG_INF))
        e["npow"] = jnp.where(strict, -(qk[tt:] * dm * bc), 0.0)
        e["x"] = jnp.concatenate([v * bc, k * (bc * egc)], axis=1)
        e["attn"] = (qk[:tt] * dm).astype(BF16)
        e["qe"] = q * egc
        e["kout"] = (k * jnp.exp(glc - gcc)).astype(BF16)
        e["dch"] = jnp.exp(glc)

    for lvl in range(n_lvl):
        for e in hd:
            nb = e["npow"].astype(BF16)
            if lvl + 1 < n_lvl:
                r = _dot(nb, jnp.concatenate([nb, e["x"].astype(BF16)], axis=1))
                e["npow"], e["x"] = r[:, :tt], e["x"] + r[:, tt:]
            else:
                e["x"] = e["x"] + _dot(nb, e["x"].astype(BF16))

    for hp, e in enumerate(hd):
        e["xb"] = e["x"].astype(BF16)
        au = _dot(e["attn"], e["xb"])
        e["o0"] = au[:, :d]
        e["qe"] = (e["qe"] - au[:, d:]).astype(BF16)
        e["s"] = state[hp]

    for ch in range(n_ch):
        sl = slice(ch * c, (ch + 1) * c)
        for hp, e in enumerate(hd):
            hs = slice(hp * d, (hp + 1) * d)
            ktx = _dot_tn(e["kout"][sl], e["xb"][sl])
            sb = e["s"].astype(BF16)
            o = _dot(e["qe"][sl], sb) + e["o0"][sl]
            e["s"] = e["s"] * e["dch"][ch * c:ch * c + 1, :] - _dot(ktx[:, d:].astype(BF16), sb) + ktx[:, :d]
            on = o * lax.rsqrt(jnp.mean(o * o, axis=-1, keepdims=True) + RMS_EPS) * nw
            o_ref[sl, hs] = (on * _silu(z_ref[sl, hs].astype(F32))).astype(o_ref.dtype)

    for hp, e in enumerate(hd):
        state[hp] = e["s"]


def _gdn_mixer(proj, conv_w, gc, beta, gl, gct, norm_w, batch, seq, heads, chunk, tt=256, hps=8):
    d = HEAD_DIM
    n = batch * seq
    nt = seq // tt
    hps = math.gcd(hps, heads)
    hb = heads // hps
    width = conv_w.shape[0]
    assert width - 1 <= _CONV_PAD and tt % chunk == 0 and seq % tt == 0 and heads % hps == 0

    def slab(off):
        return pl.BlockSpec((tt, hps * d), lambda b, h, t: (b * nt + t, off + h))

    def cw(off):
        return pl.BlockSpec((width, hps * d), lambda b, h, t: (0, off + h))

    gspec = pl.BlockSpec((tt, heads), lambda b, h, t: (b * nt + t, 0))
    return pl.pallas_call(
        functools.partial(_gdn_kernel, chunk=chunk, hps=hps),
        out_shape=jax.ShapeDtypeStruct((n, heads * d), BF16),
        grid=(batch, hb, nt),
        in_specs=[slab(0), slab(hb), slab(2 * hb), slab(3 * hb),
                  cw(0), cw(hb), cw(2 * hb),
                  gspec, gspec, gspec,
                  pl.BlockSpec((hps, 1, tt), lambda b, h, t: (b * hb + h, 0, t)),
                  pl.BlockSpec((1, d), lambda b, h, t: (0, 0))],
        out_specs=pl.BlockSpec((tt, hps * d), lambda b, h, t: (b * nt + t, h)),
        scratch_shapes=[pltpu.VMEM((hps, d, d), F32), pltpu.VMEM((3 * hps, tt + _CONV_PAD, d), F32)],
        compiler_params=_cparams(("parallel", "parallel", "arbitrary"), 40),
        name="gdn_mixer",
    )(proj, proj, proj, proj, conv_w, conv_w, conv_w, gc, beta, gl, gct, norm_w.reshape(1, d).astype(F32))


def _nsa_cmp_kernel(kc_ref, vc_ref, pek_ref, w1k_ref, w2k_ref, pev_ref, w1v_ref, w2v_ref,
                    ko_ref, vo_ref, xf, sh, *, stride, block):
    seq, d = kc_ref.shape
    nh = seq // stride
    reps = block // stride
    assert nh % 8 == 0

    def compress(src_ref, pe_ref, w1_ref, w2_ref, dst_ref):
        xf[...] = src_ref[...].astype(F32)
        pe = pe_ref[...]
        hid = None
        for r in range(reps):
            acc = None
            for l in range(stride):
                li = r * stride + l
                rows = xf[pl.ds(l, nh, stride=stride), :] + pe[li:li + 1, :]
                term = _dot(rows.astype(BF16), w1_ref[li].astype(BF16))
                acc = term if acc is None else acc + term
            if r == 0:
                hid = acc
            else:
                sh[0:nh, :] = acc
                sh[nh:nh + 8, :] = jnp.zeros((8, d), F32)
                hid = hid + sh[pl.ds(r, nh), :]
        hid = _silu(hid)
        dst_ref[0] = _dot(hid.astype(BF16), w2_ref[...].astype(BF16)).astype(dst_ref.dtype)

    compress(kc_ref, pek_ref, w1k_ref, w2k_ref, ko_ref)
    compress(vc_ref, pev_ref, w1v_ref, w2v_ref, vo_ref)


def _nsa_compress(proj, kc_col, vc_col, pe_k, w1_k, w2_k, pe_v, w1_v, w2_v, batch, seq, groups):
    d = HEAD_DIM
    nh = seq // CMP_STRIDE
    full2 = lambda b, g: (0, 0)
    full3 = lambda b, g: (0, 0, 0)
    out_sds = jax.ShapeDtypeStruct((batch * groups, nh, d), BF16)
    out_spec = pl.BlockSpec((1, nh, d), lambda b, g: (b * groups + g, 0, 0))
    return pl.pallas_call(
        functools.partial(_nsa_cmp_kernel, stride=CMP_STRIDE, block=CMP_BLOCK),
        out_shape=(out_sds, out_sds),
        grid=(batch, groups),
        in_specs=[pl.BlockSpec((seq, d), lambda b, g: (b, kc_col + g)),
                  pl.BlockSpec((seq, d), lambda b, g: (b, vc_col + g)),
                  pl.BlockSpec((CMP_BLOCK, d), full2), pl.BlockSpec((CMP_BLOCK, d, d), full3), pl.BlockSpec((d, d), full2),
                  pl.BlockSpec((CMP_BLOCK, d), full2), pl.BlockSpec((CMP_BLOCK, d, d), full3), pl.BlockSpec((d, d), full2)],
        out_specs=(out_spec, out_spec),
        scratch_shapes=[pltpu.VMEM((seq, d), F32), pltpu.VMEM((nh + 8, d), F32)],
        compiler_params=_cparams(("parallel", "parallel"), 32),
        name="nsa_compress",
    )(proj, proj, pe_k, w1_k, w2_k, pe_v, w1_v, w2_v)


def _nsa_kernel(q_ref, kc_ref, vc_ref, ks_ref, vs_ref, kw_ref, vw_ref, gate_ref, selmap_ref, expand_ref,
                o_ref, score_t, *, hpg, gps, tk, n_sel, n_top):
    tq = q_ref.shape[0]
    d = HEAD_DIM
    gw = hpg * d
    n_cmp_pad = kc_ref.shape[1]
    t0 = pl.program_id(2) * tq
    rows = hpg * tq
    grp = list(range(gps))
    kv = lambda ref, gi, r0, n: ref[pl.ds(r0, n), gi * d:(gi + 1) * d]

    q4 = []
    for gi in grp:
        qg = jnp.concatenate([q_ref[:, gi * gw + hh * d:gi * gw + (hh + 1) * d] for hh in range(hpg)], axis=0)
        q4.append((qg.astype(F32) * (d ** -0.5 * math.log2(math.e))).astype(BF16))
    trow = t0 + lax.broadcasted_iota(jnp.int32, (tq, 1), 0)

    ncol = lax.broadcasted_iota(jnp.int32, (1, n_cmp_pad), 1)
    vis_c = ((ncol * CMP_STRIDE + (CMP_BLOCK - 1)) <= trow)[None]
    s_c = [_dot_nt(q4[gi], kc_ref[gi]).reshape(hpg, tq, n_cmp_pad) for gi in grp]
    p_c = []
    for gi in grp:
        s3 = jnp.where(vis_c, s_c[gi], NEG_INF)
        e = jnp.where(vis_c, jnp.exp2(s3 - jnp.max(s3, axis=-1, keepdims=True)), 0.0)
        den = jnp.sum(e, axis=-1, keepdims=True)
        p_c.append(e / jnp.where(den > 0.0, den, 1.0))
    o_cmp = [_dot(p_c[gi].reshape(rows, n_cmp_pad).astype(BF16), vc_ref[gi]).reshape(hpg, tq, d) for gi in grp]

    jrow = lax.broadcasted_iota(jnp.int32, (n_sel, 1), 0)
    tcol = t0 + lax.broadcasted_iota(jnp.int32, (1, tq), 1)
    cur = tcol // SEL_BLOCK
    forced = (jrow == 0) | (jrow == cur) | (jrow == cur - 1)
    causal_blk = jrow * SEL_BLOCK <= tcol
    keys = []
    for gi in grp:
        imp_t = lax.dot_general(selmap_ref[...], jnp.sum(p_c[gi], axis=0), (((1,), (1,)), ((), ())),
                                preferred_element_type=F32, precision=lax.Precision.HIGHEST)
        sc = jnp.where(forced, FORCE_SCORE, jnp.where(causal_blk, imp_t, NEG_INF))
        bits = pltpu.bitcast(sc, jnp.int32)
        keys.append(bits ^ ((bits >> 31) & jnp.int32(0x7FFFFFFF)))
        score_t[gi] = keys[gi]

    def rank_step(i, cnts):
        tie = jnp.where(jrow > i, 1, 0)
        return tuple(cnts[gi] + jnp.where(score_t[gi, pl.ds(i, 1), :] + tie > keys[gi], 1.0, 0.0) for gi in grp)

    n_rank = jnp.minimum((t0 + tq - 1) // SEL_BLOCK + 1, n_sel)
    cnts = lax.fori_loop(0, n_rank, rank_step, tuple(jnp.zeros((n_sel, tq), F32) for _ in grp))
    eye = jnp.where(lax.broadcasted_iota(jnp.int32, (n_sel, n_sel), 0)
                    == lax.broadcasted_iota(jnp.int32, (n_sel, n_sel), 1), 1.0, 0.0).astype(BF16)
    sel_bias = [_dot_tn(jnp.where(cnts[gi] < n_top, 0.0, NEG_INF).astype(BF16), eye).astype(BF16) for gi in grp]

    def sel_step(kt, carry, diagonal):
        k0 = pl.multiple_of(kt * tk, tk)
        ex = expand_ref[:, pl.ds(k0, tk)]
        s3 = []
        for gi in grp:
            b = _dot(sel_bias[gi], ex)
            if diagonal:
                kpos = k0 + lax.broadcasted_iota(jnp.int32, (1, tk), 1)
                b = jnp.where(kpos <= trow, b, NEG_INF)
            s3.append(_dot_nt(q4[gi], kv(ks_ref, gi, k0, tk)).reshape(hpg, tq, tk) + b[None])
        out = []
        for gi in grp:
            m, acc = carry[gi]
            m_new = jnp.maximum(m, jnp.max(s3[gi], axis=-1, keepdims=True))
            alpha = jnp.exp2(m - m_new)
            e = jnp.exp2((s3[gi] - m_new).astype(BF16))
            pv = _dot(e.reshape(rows, tk), with_ones(kv(vs_ref, gi, k0, tk)))
            out.append((m_new, alpha * acc + pv.reshape(hpg, tq, 2 * d)))
        return tuple(out)

    def with_ones(v_tile):
        return jnp.concatenate([v_tile, jnp.ones(v_tile.shape, BF16)], axis=1)

    n_full = t0 // tk
    init = (jnp.full((hpg, tq, 1), NEG_INF, F32), jnp.zeros((hpg, tq, 2 * d), F32))
    carry = lax.fori_loop(0, n_full, functools.partial(sel_step, diagonal=False), tuple(init for _ in grp))
    carry = sel_step(n_full, carry, diagonal=True)
    o_sel = [carry[gi][1][:, :, :d] / carry[gi][1][:, :, d:] for gi in grp]

    span = WINDOW + tq
    ws = pl.multiple_of(jnp.maximum(t0 - WINDOW, 0), tq)
    kpos = ws + lax.broadcasted_iota(jnp.int32, (1, span), 1)
    rel = trow - kpos
    b_w = jnp.where((rel >= 0) & (rel < WINDOW), 0.0, NEG_INF)[None]
    s_w = [_dot_nt(q4[gi], kv(kw_ref, gi, ws, span)).reshape(hpg, tq, span) + b_w for gi in grp]
    o_win = []
    for gi in grp:
        e = jnp.exp2((s_w[gi] - jnp.max(s_w[gi], axis=-1, keepdims=True)).astype(BF16))
        pv = _dot(e.reshape(rows, span), with_ones(kv(vw_ref, gi, ws, span))).reshape(hpg, tq, 2 * d)
        o_win.append(pv[:, :, :d] / pv[:, :, d:])

    for gi in grp:
        gates = _sigmoid(gate_ref[gi])
        for hh in range(hpg):
            g0 = gates[:, 3 * hh:3 * hh + 1]
            g1 = gates[:, 3 * hh + 1:3 * hh + 2]
            g2 = gates[:, 3 * hh + 2:3 * hh + 3]
            o_ref[:, gi * gw + hh * d:gi * gw + (hh + 1) * d] = (
                g0 * o_cmp[gi][hh] + g1 * o_sel[gi][hh] + g2 * o_win[gi][hh]).astype(o_ref.dtype)


def _nsa_mixer(proj, q_col, ks_col, vs_col, kw_col, vw_col, k_cmp, v_cmp, gates, batch, seq, groups, hpg,
               tq=256, tk=1024, gps=2):
    d = HEAD_DIM
    n = batch * seq
    nq = seq // tq
    n_sel = seq // SEL_BLOCK
    n_top = min(SEL_TOP_N, n_sel)
    n_cmp = (seq - CMP_BLOCK) // CMP_STRIDE + 1
    n_cmp_pad = k_cmp.shape[1]
    assert n_sel % 8 == 0 and seq >= WINDOW + tq and WINDOW % tq == 0 and tk % tq == 0 and seq % tk == 0

    cmp_start = np.arange(n_cmp_pad) * CMP_STRIDE
    sel_start = np.arange(n_sel) * SEL_BLOCK
    overlap = (np.minimum(cmp_start[None, :] + CMP_BLOCK, sel_start[:, None] + SEL_BLOCK)
               - np.maximum(cmp_start[None, :], sel_start[:, None]))
    sel_map_t = np.clip(overlap, 0, None) / CMP_STRIDE
    sel_map_t[:, n_cmp:] = 0.0
    expand = (np.arange(seq)[None, :] // SEL_BLOCK == np.arange(n_sel)[:, None]).astype(np.float32)

    gb = groups // gps
    qw = gps * hpg * d
    assert groups % gps == 0 and all(c % gps == 0 for c in (ks_col, vs_col, kw_col, vw_col)) and q_col % (gps * hpg) == 0

    def kv(col):
        return pl.BlockSpec((seq, gps * d), lambda b, g, t: (b, col // gps + g))

    cmp_spec = pl.BlockSpec((gps, n_cmp_pad, d), lambda b, g, t: (b * gb + g, 0, 0))
    return pl.pallas_call(
        functools.partial(_nsa_kernel, hpg=hpg, gps=gps, tk=tk, n_sel=n_sel, n_top=n_top),
        out_shape=jax.ShapeDtypeStruct((n, groups * hpg * d), BF16),
        grid=(batch, gb, nq),
        in_specs=[pl.BlockSpec((tq, qw), lambda b, g, t: (b * nq + t, q_col // (gps * hpg) + g)),
                  cmp_spec, cmp_spec, kv(ks_col), kv(vs_col), kv(kw_col), kv(vw_col),
                  pl.BlockSpec((gps, tq, 3 * hpg), lambda b, g, t: (b * gb + g, t, 0)),
                  pl.BlockSpec((n_sel, n_cmp_pad), lambda b, g, t: (0, 0)),
                  pl.BlockSpec((n_sel, seq), lambda b, g, t: (0, 0))],
        out_specs=pl.BlockSpec((tq, qw), lambda b, g, t: (b * nq + t, g)),
        scratch_shapes=[pltpu.VMEM((gps, n_sel, tq), jnp.int32)],
        compiler_params=_cparams(("parallel", "parallel", "arbitrary"), 48),
        name="nsa_attention",
    )(proj, k_cmp, v_cmp, proj, proj, proj, proj, gates, jnp.asarray(sel_map_t, F32), jnp.asarray(expand, BF16))


def _router_kernel(h_ref, g_ref, wr_ref, wrl_ref, br_ref, xn_ref, ids_ref, wts_ref, cnt_ref):
    x = h_ref[...]
    xn = x * lax.rsqrt(jnp.mean(x * x, axis=-1, keepdims=True) + RMS_EPS) * g_ref[...]
    xn_ref[...] = xn
    x_hi = xn.astype(BF16)
    x_lo = (xn - x_hi.astype(F32)).astype(BF16)
    logits = (_dot(x_hi, wr_ref[...]) + _dot(x_hi, wrl_ref[...]) + _dot(x_lo, wr_ref[...])) + br_ref[...]
    lane = lax.broadcasted_iota(jnp.int32, logits.shape, 1)
    big = jnp.int32(2 * LANES)
    is_g = lane < N_GROUPS
    gl = jnp.where(is_g, logits, NEG_INF)
    gm = jnp.max(gl, axis=-1, keepdims=True)
    p_g = 1.0 / jnp.sum(jnp.where(is_g, jnp.exp(gl - gm), 0.0), axis=-1, keepdims=True)
    grp = jnp.min(jnp.where(gl == gm, lane, big), axis=-1, keepdims=True)
    eidx = lane - N_GROUPS
    in_grp = (eidx >= 0) & (eidx // EXPERTS_PER_GROUP == grp) & (eidx < N_GROUPS * EXPERTS_PER_GROUP)
    el = jnp.where(in_grp, logits, NEG_INF)
    em = jnp.max(el, axis=-1, keepdims=True)
    ee = jnp.where(in_grp, jnp.exp(el - em), 0.0)
    pe = ee / jnp.sum(ee, axis=-1, keepdims=True)
    p1 = jnp.max(jnp.where(in_grp, pe, -1.0), axis=-1, keepdims=True)
    i1 = jnp.min(jnp.where(in_grp & (pe == p1), lane, big), axis=-1, keepdims=True)
    rest = in_grp & (lane != i1)
    p2 = jnp.max(jnp.where(rest, pe, -1.0), axis=-1, keepdims=True)
    i2 = jnp.min(jnp.where(rest & (pe == p2), lane, big), axis=-1, keepdims=True)
    denom = p1 + p2
    wts_ref[...] = jnp.where(lane == 0, p_g * p1 / denom, p_g * p2 / denom)

    @pl.when(pl.program_id(0) == 0)
    def _():
        cnt_ref[...] = jnp.zeros_like(cnt_ref)

    tm = logits.shape[0]
    hot = jnp.where((lane == i1) | (lane == i2), 1.0, 0.0)
    below = jnp.where(lax.broadcasted_iota(jnp.int32, (tm, tm), 0) > lax.broadcasted_iota(jnp.int32, (tm, tm), 1),
                      1.0, 0.0).astype(BF16)
    before = _dot(below, hot.astype(BF16)) + cnt_ref[...]
    r1 = jnp.sum(jnp.where(lane == i1, before, 0.0), axis=-1, keepdims=True).astype(jnp.int32)
    r2 = jnp.sum(jnp.where(lane == i2, before, 0.0), axis=-1, keepdims=True).astype(jnp.int32)
    cnt_ref[...] += jnp.sum(hot, axis=0, keepdims=True)
    ids_ref[...] = jnp.where(lane == 0, i1 - N_GROUPS, jnp.where(lane == 1, i2 - N_GROUPS, jnp.where(lane == 2, r1, r2)))


def _router(h, g_ffn, w_group, b_group, w_expert, b_expert, tm=256):
    n, d = h.shape
    n_g, n_e = w_group.shape[1], w_expert.shape[1]
    n_r = n_g + n_e
    assert n_r <= LANES and TOP_K == 2 and n_g == N_GROUPS
    wr = jnp.concatenate([w_group, w_expert, jnp.zeros((d, LANES - n_r), F32)], axis=1).astype(F32)
    wr_hi = wr.astype(BF16)
    wr_lo = (wr - wr_hi.astype(F32)).astype(BF16)
    br = jnp.concatenate([b_group, b_expert, jnp.zeros((LANES - n_r,), F32)]).reshape(1, LANES).astype(F32)
    row = lambda i: (i, 0)
    fixed = lambda i: (0, 0)
    xn, ids, wts, cnt = pl.pallas_call(
        _router_kernel,
        out_shape=(jax.ShapeDtypeStruct((n, d), F32), jax.ShapeDtypeStruct((n, LANES), jnp.int32),
                   jax.ShapeDtypeStruct((n, LANES), F32), jax.ShapeDtypeStruct((1, LANES), F32)),
        grid=(n // tm,),
        in_specs=[pl.BlockSpec((tm, d), row), pl.BlockSpec((1, d), fixed),
                  pl.BlockSpec((d, LANES), fixed), pl.BlockSpec((d, LANES), fixed), pl.BlockSpec((1, LANES), fixed)],
        out_specs=(pl.BlockSpec((tm, d), row), pl.BlockSpec((tm, LANES), row), pl.BlockSpec((tm, LANES), row),
                   pl.BlockSpec((1, LANES), fixed)),
        compiler_params=_cparams(("arbitrary",), 40),
        name="moe_router",
    )(h, g_ffn.reshape(1, d).astype(F32), wr_hi, wr_lo, br)
    return xn, ids, wts, cnt[0, n_g:n_r]


def _row_gather(idx_ref, base, src_hbm, dst, sem, n_rows, wait, both_queues=False):
    for r in range(n_rows):
        cp = pltpu.make_async_copy(src_hbm.at[pl.ds(idx_ref[base + r], 1), :], dst.at[pl.ds(r, 1), :], sem)
        if wait:
            cp.wait()
        else:
            cp.start(priority=r % 2 if both_queues else 0)


def _slot_scatter_kernel(dest_ref, tok_ref, *, top_k, unroll):
    n_out = tok_ref.shape[0]
    n_slot = dest_ref.shape[0]

    def clear(j, carry):
        for u in range(unroll):
            tok_ref[j * unroll + u] = 0
        return carry

    def place(j, carry):
        rows = [dest_ref[j * unroll + u] for u in range(unroll)]
        for u in range(unroll):
            tok_ref[rows[u]] = (j * unroll + u) // top_k
        return carry

    lax.fori_loop(0, n_out // unroll, clear, 0)
    lax.fori_loop(0, n_slot // unroll, place, 0)


def _slot_scatter(dest_flat, n_out, top_k, unroll=64):
    assert n_out % unroll == 0 and dest_flat.shape[0] % unroll == 0
    return pl.pallas_call(
        functools.partial(_slot_scatter_kernel, top_k=top_k, unroll=unroll),
        out_shape=jax.ShapeDtypeStruct((n_out,), jnp.int32),
        in_specs=[pl.BlockSpec(memory_space=pltpu.SMEM)],
        out_specs=pl.BlockSpec(memory_space=pltpu.SMEM),
        name="moe_slot_scatter",
    )(dest_flat)


def _expert_kernel(be_ref, tok_ref, nact_ref, x_hbm, wgu_ref, wd_ref, y_ref, xbuf, sem, *, blk, d_ff):
    i = pl.program_id(0)
    n_act = nact_ref[0]
    slot = i % 2

    @pl.when(i == 0)
    def _():
        _row_gather(tok_ref, 0, x_hbm, xbuf.at[0], sem.at[0], blk, wait=False)

    def run_block(prefetch_next):
        _row_gather(tok_ref, i * blk, x_hbm, xbuf.at[slot], sem.at[slot], blk, wait=True)
        if prefetch_next:
            _row_gather(tok_ref, (i + 1) * blk, x_hbm, xbuf.at[1 - slot], sem.at[1 - slot], blk, wait=False)
        xb = xbuf[slot].astype(BF16)
        gu = _dot(xb, wgu_ref[0])
        act = _silu(gu[:, :d_ff]) * gu[:, d_ff:]
        y_ref[...] = _dot(act.astype(BF16), wd_ref[0])

    pl.when(i + 1 < n_act)(functools.partial(run_block, True))
    pl.when(i + 1 == n_act)(functools.partial(run_block, False))

    @pl.when(i >= n_act)
    def _():
        y_ref[...] = jnp.zeros_like(y_ref)


def _experts(xn, w_gate_up, w_down, blk_expert, tok_buf, n_active, blk):
    n, d = xn.shape
    n_exp, _, ff2 = w_gate_up.shape
    d_ff = ff2 // 2
    n_blk = blk_expert.shape[0]
    return pl.pallas_call(
        functools.partial(_expert_kernel, blk=blk, d_ff=d_ff),
        out_shape=jax.ShapeDtypeStruct((n_blk * blk, d), F32),
        grid_spec=pltpu.PrefetchScalarGridSpec(
            num_scalar_prefetch=3,
            grid=(n_blk,),
            in_specs=[pl.BlockSpec(memory_space=pl.ANY),
                      pl.BlockSpec((1, d, ff2), lambda i, be, tok, na: (be[i], 0, 0)),
                      pl.BlockSpec((1, d_ff, d), lambda i, be, tok, na: (be[i], 0, 0))],
            out_specs=pl.BlockSpec((blk, d), lambda i, be, tok, na: (i, 0)),
            scratch_shapes=[pltpu.VMEM((2, blk, d), F32), pltpu.SemaphoreType.DMA((2,))]),
        compiler_params=_cparams(("arbitrary",), 56),
        name="moe_experts",
    )(blk_expert, tok_buf, n_active, xn, w_gate_up, w_down)


def _combine_kernel(pos_ref, h_ref, g_ref, w_ref, y_hbm, o_ref, ybuf, sem, *, tc):
    i = pl.program_id(0)
    n = pl.num_programs(0)
    slot = i % 2
    rows = TOP_K * tc

    @pl.when(i == 0)
    def _():
        _row_gather(pos_ref, 0, y_hbm, ybuf.at[0], sem.at[0], rows, wait=False, both_queues=True)

    def run_tile(prefetch_next):
        _row_gather(pos_ref, i * rows, y_hbm, ybuf.at[slot], sem.at[slot], rows, wait=True)
        if prefetch_next:
            _row_gather(pos_ref, (i + 1) * rows, y_hbm, ybuf.at[1 - slot], sem.at[1 - slot], rows, wait=False,
                        both_queues=True)
        hh = h_ref[...]
        w = w_ref[...]
        for kk in range(TOP_K):
            hh = hh + w[:, kk:kk + 1] * ybuf[slot, kk * tc:(kk + 1) * tc, :]
        o_ref[...] = hh * lax.rsqrt(jnp.mean(hh * hh, axis=-1, keepdims=True) + RMS_EPS) * g_ref[...]

    pl.when(i + 1 < n)(functools.partial(run_tile, True))
    pl.when(i + 1 == n)(functools.partial(run_tile, False))


def _combine(h, y_buf, pos, wts, g_final, tc=128):
    n, d = h.shape
    return pl.pallas_call(
        functools.partial(_combine_kernel, tc=tc),
        out_shape=jax.ShapeDtypeStruct((n, d), F32),
        grid_spec=pltpu.PrefetchScalarGridSpec(
            num_scalar_prefetch=1,
            grid=(n // tc,),
            in_specs=[pl.BlockSpec((tc, d), lambda i, pos: (i, 0)),
                      pl.BlockSpec((1, d), lambda i, pos: (0, 0)),
                      pl.BlockSpec((tc, LANES), lambda i, pos: (i, 0)),
                      pl.BlockSpec(memory_space=pl.ANY)],
            out_specs=pl.BlockSpec((tc, d), lambda i, pos: (i, 0)),
            scratch_shapes=[pltpu.VMEM((2, TOP_K * tc, d), F32), pltpu.SemaphoreType.DMA((2,))]),
        compiler_params=_cparams(("arbitrary",), 40),
        name="moe_combine_final_norm",
    )(pos, h, g_final.reshape(1, d).astype(F32), wts, y_buf)


def _dispatch_plan(ids, counts, blk):
    n_tok = ids.shape[0]
    n_experts = counts.shape[0]
    n_slot = n_tok * TOP_K
    counts = counts.astype(jnp.int32)
    padded = (counts + blk - 1) // blk * blk
    pad_end = jnp.cumsum(padded)
    pad_start = pad_end - padded
    dest = pad_start[ids[:, :TOP_K]] + ids[:, TOP_K:2 * TOP_K]
    n_blk = -(-n_slot // blk) + n_experts
    blk_start = jnp.arange(n_blk, dtype=jnp.int32) * blk
    blk_expert = jnp.minimum(jnp.sum(pad_end[None, :] <= blk_start[:, None], axis=1), n_experts - 1).astype(jnp.int32)
    n_active = (pad_end[-1] // blk).astype(jnp.int32).reshape(1)
    return dest.astype(jnp.int32), blk_expert, n_active, n_blk


def _layer(x, g_mix, w_in, conv_w, a_log, dt_bias, norm_w, pe_k, w1_k, w2_k, pe_v, w1_v, w2_v,
           w_a, w_b, w_out, g_ffn, w_group, b_group, w_expert, b_expert, w_gate_up, w_down, g_out):
    batch, seq, dm = x.shape
    n = batch * seq
    d = HEAD_DIM
    gdn_w = w_a.shape[0]
    nsa_w = w_b.shape[0]
    heads = gdn_w // d
    nsa_heads = nsa_w // d
    groups = NSA_GROUPS
    hpg = nsa_heads // groups
    kvw = groups * d

    sizes = (3 * gdn_w, gdn_w, heads, heads, nsa_w, 6 * kvw, 3 * nsa_heads, 2 * dm)
    assert sum(sizes) == w_in.shape[1]
    offs = np.concatenate([[0], np.cumsum(sizes)])
    tn_in = 512
    wide = (0, 1, 4, 5, 7)
    assert all(sizes[i] % tn_in == 0 for i in wide) and sizes[2] + sizes[3] <= tn_in and sizes[6] <= tn_in
    starts = [c for i in wide for c in range(offs[i], offs[i + 1], tn_in)]
    n_big = len(starts) * tn_in
    w_pack = _pack_columns(w_in.T, starts + [offs[2], offs[6]], tn=tn_in)
    w_small = jnp.concatenate([w_pack[:, n_big:n_big + sizes[2] + sizes[3]],
                               w_pack[:, n_big + tn_in:n_big + tn_in + sizes[6]]], axis=1)
    z_col = 3 * heads
    nq_col = z_col + heads
    kv_col = nq_col + nsa_heads
    gate_col0 = (kv_col + 6 * groups) * d

    x2 = x.reshape(n, dm)
    xn = _rmsnorm(x2, g_mix, BF16)
    n_experts, _, ff2 = w_gate_up.shape
    proj, wgu_bf, wd_bf = _matmul_with_casts(
        xn, w_pack, n_big, BF16, tm=1024, tn=tn_in, name="in_proj",
        riders=[w_gate_up.reshape(n_experts * dm, ff2), w_down.reshape(n_experts * (ff2 // 2), dm)])
    wgu_bf = wgu_bf.reshape(w_gate_up.shape)
    wd_bf = wd_bf.reshape(w_down.shape)
    small = _matmul(xn, w_small, F32, tm=1024, tn=w_small.shape[1], name="in_proj_small")
    a_raw, b_raw, nsa_gate = small[:, :heads], small[:, heads:2 * heads], small[:, 2 * heads:]

    gc, beta, gl = _gdn_gates(a_raw, b_raw, a_log, dt_bias, GDN_CHUNK)
    gct = gc.reshape(batch, seq, heads).transpose(0, 2, 1).reshape(batch * heads, 1, seq)
    o_gdn = _gdn_mixer(proj, conv_w, gc, beta, gl, gct, norm_w, batch, seq, heads, GDN_CHUNK)

    k_cmp, v_cmp = _nsa_compress(proj, kv_col, kv_col + groups, pe_k, w1_k, w2_k, pe_v, w1_v, w2_v,
                                 batch, seq, groups)
    gates = nsa_gate.reshape(batch, seq, groups, 3 * hpg).transpose(0, 2, 1, 3).reshape(batch * groups, seq, 3 * hpg)
    o_nsa = _nsa_mixer(proj, nq_col, kv_col + 2 * groups, kv_col + 3 * groups, kv_col + 4 * groups,
                       kv_col + 5 * groups, k_cmp, v_cmp, gates, batch, seq, groups, hpg)

    merged = _merge(o_gdn, o_nsa, w_a.astype(BF16), w_b.astype(BF16), proj, gate_col0)
    h = _resid_matmul(merged, w_out.astype(BF16), x2)

    xn2, ids, wts, counts = _router(h, g_ffn, w_group, b_group, w_expert, b_expert)
    dest, blk_expert, n_active, n_blk = _dispatch_plan(ids, counts, MOE_BLOCK)
    tok_buf = _slot_scatter(dest.reshape(n * TOP_K), n_blk * MOE_BLOCK, TOP_K)
    y_buf = _experts(xn2, wgu_bf, wd_bf, blk_expert, tok_buf, n_active, MOE_BLOCK)
    tc = 128
    pos_tiles = dest.reshape(n // tc, tc, TOP_K).transpose(0, 2, 1).reshape(n * TOP_K)
    out = _combine(h, y_buf, pos_tiles, wts, g_out, tc)
    return out.reshape(batch, seq, dm)


def kernel(x, g_mix, w_in, gdn_conv_w, gdn_a_log, gdn_dt_bias, gdn_norm_w, cmp_pe_k, cmp_w1_k, cmp_w2_k,
           cmp_pe_v, cmp_w1_v, cmp_w2_v, w_branch_gdn, w_branch_nsa, w_out, g_ffn, w_group, b_group,
           w_expert, b_expert, w_gate_up, w_down, g_final):
    depth = g_mix.shape[0]
    assert depth == 1, "the fused final norm assumes a single layer"
    return _layer(x, g_mix[0], w_in[0], gdn_conv_w[0], gdn_a_log[0], gdn_dt_bias[0], gdn_norm_w[0],
                  cmp_pe_k[0], cmp_w1_k[0], cmp_w2_k[0], cmp_pe_v[0], cmp_w1_v[0], cmp_w2_v[0],
                  w_branch_gdn[0], w_branch_nsa[0], w_out[0], g_ffn[0], w_group[0], b_group[0],
                  w_expert[0], b_expert[0], w_gate_up[0], w_down[0], g_final)
```
